```python
import jax
import jax.numpy as jnp
from jax import lax
import numpy as np

D_MODEL = 2048
BATCH = 2
SEQ = 4096
DEPTH = 2
DEC_BATCH = 32
DEC_SEQ = 8
PAST_LEN = 8192
PAGE_SIZE = 128

HEAD_DIM = 128
N_EVEN = (DEPTH + 1) // 2
N_ODD = DEPTH // 2
NSA_HEADS = D_MODEL // (2 * HEAD_DIM)
NSA_KV_HEADS = NSA_HEADS // 4
NSA_GROUP = NSA_HEADS // NSA_KV_HEADS
CMP_BLOCK = 64
SEL_BLOCK = 64
SEL_TOPK = 16
WINDOW = 512
CMP_HIDDEN = 256
RET_HEADS = D_MODEL // (2 * HEAD_DIM)
RET_DK = HEAD_DIM
RET_DV = HEAD_DIM
RET_CHUNK = 128
FOX_HEADS = D_MODEL // HEAD_DIM
FOX_KV_HEADS = FOX_HEADS // 4
FOX_GROUP = FOX_HEADS // FOX_KV_HEADS
Q_BLOCK = 128
PEER_HEADS = 8
PEER_DK = 256
N_KEYS = 128
N_EXPERTS = N_KEYS * N_KEYS
PEER_TOPK = 16
PEER_CHUNK = 128
EPS = 1e-6
GN_EPS = 1e-5
NEG = -1e30

kernel_name = 'nsa_retnet_fox_peer_decode_step'


def ab_sizes():
    qa = NSA_HEADS * HEAD_DIM
    kv = NSA_KV_HEADS * HEAD_DIM
    return (qa, kv, kv, kv, kv, kv, kv, NSA_HEADS * 3,
            RET_HEADS * RET_DK, RET_HEADS * RET_DK, RET_HEADS * RET_DV, RET_HEADS * RET_DV)


def c_sizes():
    return (FOX_HEADS * HEAD_DIM, FOX_KV_HEADS * HEAD_DIM, FOX_KV_HEADS * HEAD_DIM, FOX_HEADS)


def rmsnorm(x, w):
    xf = x.astype(jnp.float32)
    y = xf * lax.rsqrt(jnp.mean(xf * xf, axis=-1, keepdims=True) + EPS) * w.astype(jnp.float32)
    return y.astype(x.dtype)


def split_cols(h, sizes):
    out = []
    start = 0
    for n in sizes:
        out.append(h[..., start:start + n])
        start += n
    return out


def masked_softmax(s, mask):
    s = jnp.where(mask, s.astype(jnp.float32), NEG)
    m = jnp.max(s, axis=-1, keepdims=True)
    e = jnp.where(mask, jnp.exp(s - m), 0.0)
    return e / jnp.maximum(jnp.sum(e, axis=-1, keepdims=True), 1e-30)


def alibi_slopes():
    return jnp.exp2(-8.0 * (jnp.arange(NSA_HEADS, dtype=jnp.float32) + 1.0) / NSA_HEADS)


def compress_blocks(blocks, pe, w1, w2):
    w1r = w1.reshape(CMP_BLOCK, HEAD_DIM, CMP_HIDDEN)
    h = jax.nn.gelu(jnp.einsum('bnlgd,ldh->bngh', blocks + pe[None, None, :, None, :], w1r))
    return jnp.einsum('bngh,he->bnge', h, w2)


def nsa_attend(q, gates, rows, win_arr, offset, wstart, pe_k, w1_k, w2_k, pe_v, w1_v, w2_v):
    B, Tq, H, hd = q.shape
    Tk = rows.shape[1]
    G, R = NSA_KV_HEADS, NSA_GROUP
    nb = Tk // CMP_BLOCK
    blocks = rows[:, :nb * CMP_BLOCK].reshape(B, nb, CMP_BLOCK, 4, G, hd)
    kc = compress_blocks(blocks[:, :, :, 0], pe_k, w1_k, w2_k)
    vc = compress_blocks(blocks[:, :, :, 1], pe_v, w1_v, w2_v)
    n_sel = -(-Tk // SEL_BLOCK)
    sel = jnp.pad(rows[:, :, 2:4], ((0, 0), (0, n_sel * SEL_BLOCK - Tk), (0, 0), (0, 0), (0, 0)))
    sel = sel.reshape(B, n_sel, SEL_BLOCK, 2, G, hd).transpose(0, 4, 1, 2, 3, 5)
    k_top = min(SEL_TOPK, n_sel)
    slopes = alibi_slopes().reshape(G, R)
    qb = Q_BLOCK if Tq % Q_BLOCK == 0 else Tq
    nq = Tq // qb
    qs = (q * hd ** -0.5).reshape(B, nq, qb, G, R, hd).transpose(1, 0, 2, 3, 4, 5)
    gs = jax.nn.sigmoid(gates.astype(jnp.float32)).reshape(B, nq, qb, G, R, 3).transpose(1, 0, 2, 3, 4, 5)
    bi = jnp.arange(B)[:, None, None, None]
    gi = jnp.arange(G)[None, :, None, None]
    blk_end = jnp.arange(nb) * CMP_BLOCK + (CMP_BLOCK - 1)
    jsel = jnp.arange(n_sel)
    in_blk = jnp.arange(SEL_BLOCK)

    def one_block(xs):
        qg, g, i = xs
        a = offset + i * qb
        t = a + jnp.arange(qb)
        d_c = t[:, None] - blk_end[None, :]
        s = jnp.einsum('bqgrd,bngd->bgrqn', qg, kc).astype(jnp.float32) - slopes[:, :, None, None] * d_c
        p_c = masked_softmax(s, d_c >= 0)
        o_c = jnp.einsum('bgrqn,bngd->bqgrd', p_c, vc)
        imp = jnp.pad(p_c.sum(axis=2), ((0, 0), (0, 0), (0, 0), (0, n_sel - nb)))
        cur = t // SEL_BLOCK
        forced = (jsel[None, :] == 0) | (jsel[None, :] == cur[:, None]) | (jsel[None, :] == cur[:, None] - 1)
        score = jnp.where(forced, R + 1.0, jnp.where(jsel[None, :] <= cur[:, None], imp, -1.0))
        vals, idx = lax.top_k(score, k_top)
        kv_sel = sel[bi, gi, idx]
        pos = idx[..., None] * SEL_BLOCK + in_blk
        d_s = t[None, None, :, None, None] - pos
        m_s = (vals >= 0)[..., None] & (d_s >= 0)
        s = jnp.einsum('bqgrd,bgqkld->bgrqkl', qg, kv_sel[..., 0, :]).astype(jnp.float32)
        s = s - slopes[None, :, :, None, None, None] * d_s[:, :, None]
        p_s = masked_softmax(s.reshape(B, G, R, qb, k_top * SEL_BLOCK),
                             m_s[:, :, None].reshape(B, G, 1, qb, k_top * SEL_BLOCK))
        o_s = jnp.einsum('bgrqkl,bgqkld->bqgrd', p_s.reshape(B, G, R, qb, k_top, SEL_BLOCK), kv_sel[..., 1, :])
        kv_w = lax.dynamic_slice_in_dim(win_arr, a - wstart, WINDOW + qb, axis=1)
        pos_w = a - WINDOW + jnp.arange(WINDOW + qb)
        d_w = t[:, None] - pos_w[None, :]
        m_w = (d_w >= 0) & (d_w <= WINDOW) & (pos_w >= wstart)[None, :]
        s = jnp.einsum('bqgrd,bkgd->bgrqk', qg, kv_w[:, :, 0]).astype(jnp.float32) - slopes[:, :, None, None] * d_w
        p_w = masked_softmax(s, m_w)
        o_w = jnp.einsum('bgrqk,bkgd->bqgrd', p_w, kv_w[:, :, 1])
        o = g[..., 0:1] * o_c + g[..., 1:2] * o_s + g[..., 2:3] * o_w
        return o.reshape(B, qb, H, hd).astype(q.dtype)

    out = lax.map(one_block, (qs, gs, jnp.arange(nq, dtype=jnp.int32)))
    return out.transpose(1, 0, 2, 3, 4).reshape(B, Tq, H, hd)


def retention(q, k, v, S0):
    B, T, H, dk = q.shape
    C = RET_CHUNK if T % RET_CHUNK == 0 else T
    n = T // C
    lg = jnp.log1p(-jnp.exp2(-5.0 - jnp.arange(H, dtype=jnp.float32)))
    i = jnp.arange(C, dtype=jnp.float32)
    diff = i[:, None] - i[None, :]
    dmask = jnp.where(diff >= 0, jnp.exp(jnp.maximum(diff, 0.0)[None] * lg[:, None, None]), 0.0)
    cross = jnp.exp((i + 1.0)[:, None] * lg[None, :])
    kdec = jnp.exp((C - 1.0 - i)[:, None] * lg[None, :])
    cdec = jnp.exp(C * lg)

    def to_chunks(a):
        return a.astype(jnp.float32).reshape(B, n, C, H, a.shape[-1]).transpose(1, 0, 2, 3, 4)

    qs, ks, vs = to_chunks(q), to_chunks(k * dk ** -0.5), to_chunks(v)

    def step(S, xs):
        qc, kc, vc = xs
        att = jnp.einsum('bihd,bjhd->bhij', qc, kc) * dmask
        o = jnp.einsum('bhij,bjhe->bihe', att, vc) + jnp.einsum('bihd,bhde->bihe', qc, S) * cross[None, :, :, None]
        S = S * cdec[None, :, None, None] + jnp.einsum('bjhd,bjhe->bhde', kc * kdec[None, :, :, None], vc)
        return S, o

    S, o = lax.scan(step, S0.astype(jnp.float32), (qs, ks, vs))
    return o.transpose(1, 0, 2, 3, 4).reshape(B, T, H, -1), S


def fox_attend(q, rows, F, offset):
    B, Tq, H, hd = q.shape
    Tk = rows.shape[1]
    G, R = FOX_KV_HEADS, FOX_GROUP
    qb = Q_BLOCK if Tq % Q_BLOCK == 0 else Tq
    nq = Tq // qb
    qs = (q * hd ** -0.5).reshape(B, nq, qb, G, R, hd).transpose(1, 0, 2, 3, 4, 5)
    Fq = F[:, offset:].reshape(B, nq, qb, G, R).transpose(1, 0, 2, 3, 4)
    Fk = F.reshape(B, Tk, G, R).transpose(0, 2, 3, 1)
    k_pos = jnp.arange(Tk)

    def one_block(xs):
        qg, fq, i = xs
        t = offset + i * qb + jnp.arange(qb)
        s = jnp.einsum('bqgrd,bkgd->bgrqk', qg, rows[:, :, 0]).astype(jnp.float32)
        s = s + fq.transpose(0, 2, 3, 1)[..., None] - Fk[:, :, :, None, :]
        p = masked_softmax(s, k_pos[None, :] <= t[:, None])
        o = jnp.einsum('bgrqk,bkgd->bqgrd', p, rows[:, :, 1])
        return o.reshape(B, qb, H, hd).astype(q.dtype)

    out = lax.map(one_block, (qs, Fq, jnp.arange(nq, dtype=jnp.int32)))
    return out.transpose(1, 0, 2, 3, 4).reshape(B, Tq, H, hd)


def peer_ffn(xn, wq, k1, k2, u, v):
    B, T, D = xn.shape
    n = B * T
    c = PEER_CHUNK
    pad = (-n) % c
    xt = jnp.pad(xn.reshape(n, D), ((0, pad), (0, 0))).reshape(-1, c, D)
    half = PEER_DK // 2

    def one_chunk(xc):
        q = (xc @ wq).reshape(c, PEER_HEADS, PEER_DK)
        s1 = jnp.einsum('chd,nd->chn', q[..., :half], k1).astype(jnp.float32)
        s2 = jnp.einsum('chd,nd->chn', q[..., half:], k2).astype(jnp.float32)
        v1, i1 = lax.top_k(s1, PEER_TOPK)
        v2, i2 = lax.top_k(s2, PEER_TOPK)
        cand = (v1[..., :, None] + v2[..., None, :]).reshape(c, PEER_HEADS, PEER_TOPK * PEER_TOPK)
        vals, ci = lax.top_k(cand, PEER_TOPK)
        e = (jnp.take_along_axis(i1, ci // PEER_TOPK, axis=-1) * N_KEYS
             + jnp.take_along_axis(i2, ci % PEER_TOPK, axis=-1))
        g = jax.nn.softmax(vals, axis=-1)
        act = jax.nn.gelu(jnp.einsum('cd,chkd->chk', xc, u[e]).astype(jnp.float32))
        return jnp.einsum('chk,chkd->cd', g * act, v[e]).astype(xc.dtype)

    out = lax.map(one_chunk, xt).reshape(-1, D)[:n]
    return out.reshape(B, T, D)


def even_mixer(xn, w_in, w_out, pe_k, w1_k, w2_k, pe_v, w1_v, w2_v, gn_w, kv_past, win_buf, S0):
    B, T, _ = xn.shape
    hd, G = HEAD_DIM, NSA_KV_HEADS
    q, kc, vc, ks, vs, kw, vw, gt, rq, rk, rv, rg = split_cols(xn @ w_in, ab_sizes())
    q = q.reshape(B, T, NSA_HEADS, hd)
    new_rows = jnp.stack([a.reshape(B, T, G, hd) for a in (kc, vc, ks, vs)], axis=2)
    new_win = jnp.stack([kw.reshape(B, T, G, hd), vw.reshape(B, T, G, hd)], axis=2)
    pad = jnp.zeros((B, WINDOW, 2, G, hd), new_win.dtype)
    if kv_past is None:
        offset, wstart = 0, 0
        rows = new_rows
        win_arr = jnp.concatenate([pad, new_win], axis=1)
        win_state = new_win[:, -min(WINDOW, T):]
        S0 = jnp.zeros((B, RET_HEADS, RET_DK, RET_DV), jnp.float32)
    else:
        offset = kv_past.shape[1]
        wb = win_buf.shape[1]
        wstart = offset - wb
        rows = jnp.concatenate([kv_past.astype(new_rows.dtype), new_rows], axis=1)
        buf_all = jnp.concatenate([win_buf.astype(new_win.dtype), new_win], axis=1)
        win_arr = jnp.concatenate([pad, buf_all], axis=1)
        win_state = buf_all[:, -wb:]
    o_a = nsa_attend(q, gt.reshape(B, T, NSA_HEADS, 3), rows, win_arr, offset, wstart,
                     pe_k, w1_k, w2_k, pe_v, w1_v, w2_v)
    o_b, S = retention(rq.reshape(B, T, RET_HEADS, RET_DK), rk.reshape(B, T, RET_HEADS, RET_DK),
                       rv.reshape(B, T, RET_HEADS, RET_DV), S0)
    mu = jnp.mean(o_b, axis=-1, keepdims=True)
    var = jnp.mean(jnp.square(o_b - mu), axis=-1, keepdims=True)
    o_b = ((o_b - mu) * lax.rsqrt(var + GN_EPS)).reshape(B, T, RET_HEADS * RET_DV) * gn_w.astype(jnp.float32)
    o_b = (jax.nn.silu(rg.astype(jnp.float32)) * o_b).astype(xn.dtype)
    out = jnp.concatenate([o_a.reshape(B, T, NSA_HEADS * hd), o_b], axis=-1) @ w_out
    return out, new_rows, win_state, S


def odd_mixer(xn, w_in, b_f, w_out, kv_past, logf_past):
    B, T, _ = xn.shape
    hd, H, G = HEAD_DIM, FOX_HEADS, FOX_KV_HEADS
    q, k, v, fl = split_cols(xn @ w_in, c_sizes())
    new_rows = jnp.stack([k.reshape(B, T, G, hd), v.reshape(B, T, G, hd)], axis=2)
    new_logf = jax.nn.log_sigmoid(fl.astype(jnp.float32) + b_f.astype(jnp.float32))
    if kv_past is None:
        offset = 0
        rows, logf_all = new_rows, new_logf
    else:
        offset = kv_past.shape[1]
        rows = jnp.concatenate([kv_past.astype(new_rows.dtype), new_rows], axis=1)
        logf_all = jnp.concatenate([logf_past.astype(jnp.float32), new_logf], axis=1)
    F = jnp.cumsum(logf_all, axis=1)
    o = fox_attend(q.reshape(B, T, H, hd), rows, F, offset)
    return o.reshape(B, T, H * hd) @ w_out, new_rows, new_logf


def setup_inputs(seed: int = 0) -> dict:
    key = jax.random.key(seed)
    ks = iter(jax.random.split(key, 32))

    def nrm(shape, scale):
        return jax.random.normal(next(ks), shape, jnp.float32) * scale

    n_pages = PAST_LEN // PAGE_SIZE
    n_used = DEC_BATCH * n_pages
    n_pool = n_used + n_used // 4
    wb = min(WINDOW, PAST_LEN)
    G, Gf, hd = NSA_KV_HEADS, FOX_KV_HEADS, HEAD_DIM
    ab_cols = sum(ab_sizes())
    c_cols = sum(c_sizes())
    mix_w = NSA_HEADS * hd + RET_HEADS * RET_DV
    return {
        'x_prompt': nrm((BATCH, SEQ, D_MODEL), 1.0),
        'x_sample': nrm((DEC_BATCH, DEC_SEQ, D_MODEL), 1.0),
        'cache_nsa_kv': nrm((N_EVEN, n_pool, PAGE_SIZE, 4, G, hd), 1.0),
        'cache_nsa_win': nrm((N_EVEN, DEC_BATCH, wb, 2, G, hd), 1.0),
        'state_ret': nrm((N_EVEN, DEC_BATCH, RET_HEADS, RET_DK, RET_DV), 1.0),
        'cache_fox_kv': nrm((N_ODD, n_pool, PAGE_SIZE, 2, Gf, hd), 1.0),
        'cache_fox_logf': jax.nn.log_sigmoid(nrm((N_ODD, n_pool, PAGE_SIZE, FOX_HEADS), 0.5) + 3.0),
        'page_table': jax.random.permutation(next(ks), n_pool)[:n_used].reshape(DEC_BATCH, n_pages).astype(jnp.int32),
        'norm_mix': 1.0 + nrm((DEPTH, D_MODEL), 0.05),
        'norm_ffn': 1.0 + nrm((DEPTH, D_MODEL), 0.05),
        'norm_final': 1.0 + nrm((D_MODEL,), 0.05),
        'w_in_ab': nrm((N_EVEN, D_MODEL, ab_cols), D_MODEL ** -0.5),
        'w_out_ab': nrm((N_EVEN, mix_w, D_MODEL), mix_w ** -0.5),
        'cmp_pe_k': nrm((N_EVEN, CMP_BLOCK, hd), 0.1),
        'cmp_w1_k': nrm((N_EVEN, CMP_BLOCK * hd, CMP_HIDDEN), (CMP_BLOCK * hd) ** -0.5),
        'cmp_w2_k': nrm((N_EVEN, CMP_HIDDEN, hd), CMP_HIDDEN ** -0.5),
        'cmp_pe_v': nrm((N_EVEN, CMP_BLOCK, hd), 0.1),
        'cmp_w1_v': nrm((N_EVEN, CMP_BLOCK * hd, CMP_HIDDEN), (CMP_BLOCK * hd) ** -0.5),
        'cmp_w2_v': nrm((N_EVEN, CMP_HIDDEN, hd), CMP_HIDDEN ** -0.5),
        'ret_gn': 1.0 + nrm((N_EVEN, RET_HEADS * RET_DV), 0.05),
        'w_in_c': nrm((N_ODD, D_MODEL, c_cols), D_MODEL ** -0.5),
        'b_forget': 3.0 + nrm((N_ODD, FOX_HEADS), 0.5),
        'w_out_c': nrm((N_ODD, FOX_HEADS * hd, D_MODEL), (FOX_HEADS * hd) ** -0.5),
        'peer_wq': nrm((DEPTH, D_MODEL, PEER_HEADS * PEER_DK), D_MODEL ** -0.5),
        'peer_k1': nrm((DEPTH, N_KEYS, PEER_DK // 2), (PEER_DK // 2) ** -0.5),
        'peer_k2': nrm((DEPTH, N_KEYS, PEER_DK // 2), (PEER_DK // 2) ** -0.5),
        'peer_u': nrm((DEPTH, N_EXPERTS, D_MODEL), D_MODEL ** -0.5),
        'peer_v': nrm((DEPTH, N_EXPERTS, D_MODEL), (PEER_HEADS * PEER_TOPK) ** -0.5),
    }


def reference(x_prompt, x_sample, cache_nsa_kv, cache_nsa_win, state_ret, cache_fox_kv, cache_fox_logf,
              page_table, norm_mix, norm_ffn, norm_final, w_in_ab, w_out_ab, cmp_pe_k, cmp_w1_k, cmp_w2_k,
              cmp_pe_v, cmp_w1_v, cmp_w2_v, ret_gn, w_in_c, b_forget, w_out_c, peer_wq, peer_k1, peer_k2,
              peer_u, peer_v):
    n_pages = page_table.shape[1]

    def gather_pages(pool):
        g = pool[page_table]
        return g.reshape((g.shape[0], n_pages * g.shape[2]) + g.shape[3:])

    xp, xs = x_prompt, x_sample
    nsa_p, win_p, ret_p, nsa_s, win_s, ret_s = [], [], [], [], [], []
    fox_p, logf_p, fox_s, logf_s = [], [], [], []
    for l in range(DEPTH):
        hp = rmsnorm(xp, norm_mix[l])
        hs = rmsnorm(xs, norm_mix[l])
        if l % 2 == 0:
            e = l // 2
            w = (w_in_ab[e], w_out_ab[e], cmp_pe_k[e], cmp_w1_k[e], cmp_w2_k[e],
                 cmp_pe_v[e], cmp_w1_v[e], cmp_w2_v[e], ret_gn[e])
            op, r_p, b_p, st_p = even_mixer(hp, *w, None, None, None)
            os_, r_s, b_s, st_s = even_mixer(hs, *w, gather_pages(cache_nsa_kv[e]), cache_nsa_win[e], state_ret[e])
            nsa_p.append(r_p)
            win_p.append(b_p)
            ret_p.append(st_p)
            nsa_s.append(r_s)
            win_s.append(b_s)
            ret_s.append(st_s)
        else:
            o = l // 2
            op, r_p, lf_p = odd_mixer(hp, w_in_c[o], b_forget[o], w_out_c[o], None, None)
            os_, r_s, lf_s = odd_mixer(hs, w_in_c[o], b_forget[o], w_out_c[o],
                                       gather_pages(cache_fox_kv[o]), gather_pages(cache_fox_logf[o]))
            fox_p.append(r_p)
            logf_p.append(lf_p)
            fox_s.append(r_s)
            logf_s.append(lf_s)
        xp = xp + op
        xs = xs + os_
        xp = xp + peer_ffn(rmsnorm(xp, norm_ffn[l]), peer_wq[l], peer_k1[l], peer_k2[l], peer_u[l], peer_v[l])
        xs = xs + peer_ffn(rmsnorm(xs, norm_ffn[l]), peer_wq[l], peer_k1[l], peer_k2[l], peer_u[l], peer_v[l])
    y_prompt = rmsnorm(xp, norm_final)
    y_sample = rmsnorm(xs, norm_final)
    p_nsa_kv = jnp.stack(nsa_p)
    p_nsa_win = jnp.stack(win_p)
    p_ret = jnp.stack(ret_p)
    p_fox_kv = jnp.stack(fox_p)
    p_fox_logf = jnp.stack(logf_p)
    s_nsa_kv = jnp.stack(nsa_s)
    s_nsa_win = jnp.stack(win_s)
    s_ret = jnp.stack(ret_s)
    s_fox_kv = jnp.stack(fox_s)
    s_fox_logf = jnp.stack(logf_s)
    return (y_prompt, y_sample, p_nsa_kv, p_nsa_win, p_ret, p_fox_kv, p_fox_logf,
            s_nsa_kv, s_nsa_win, s_ret, s_fox_kv, s_fox_logf)
```

```python
import functools
import math

import numpy as np
import jax
import jax.numpy as jnp
from jax import lax
from jax.experimental import pallas as pl
from jax.experimental.pallas import tpu as pltpu

F32 = jnp.float32
BF16 = jnp.bfloat16
I32 = jnp.int32

HEAD_DIM = 128
CMP_BLOCK = 64
SEL_BLOCK = 64
SEL_TOPK = 16
WINDOW = 512
CMP_HIDDEN = 256
RET_CHUNK = 128
Q_BLOCK = 128
PEER_HEADS = 8
PEER_DK = 256
PEER_TOPK = 16
EPS = 1e-6
GN_EPS = 1e-5
NEG = -1e30

LANES = 128
SUBLANES = 8
VMEM_LIMIT = 56 * 1024 * 1024

SDS = jax.ShapeDtypeStruct


def _cparams(sem):
    return pltpu.CompilerParams(dimension_semantics=sem, vmem_limit_bytes=VMEM_LIMIT)


def _dot(a, b):
    return jnp.dot(a, b, preferred_element_type=F32)


def _dot_nt(a, b):
    return lax.dot_general(a, b, (((1,), (1,)), ((), ())), preferred_element_type=F32)


def _dot_tn(a, b):
    return lax.dot_general(a, b, (((0,), (0,)), ((), ())), preferred_element_type=F32)


def _pick(n, cands):
    for c in cands:
        if c <= n and n % c == 0:
            return c
    raise ValueError(f"no tile for {n} in {cands}")


def _gelu(x):
    c = math.sqrt(2.0 / math.pi)
    return x * (0.5 * (1.0 + jnp.tanh(c * (x + 0.044715 * (x * x * x)))))


def _split3(x):
    hi = x.astype(BF16)
    r1 = x - hi.astype(F32)
    mid = r1.astype(BF16)
    lo = (r1 - mid.astype(F32)).astype(BF16)
    return hi, mid, lo


def _stack_heads(x, nh, hd):
    return jnp.concatenate([x[:, r * hd:(r + 1) * hd] for r in range(nh)], axis=0)


def _unstack_heads(x, nh, t):
    return jnp.concatenate([x[r * t:(r + 1) * t, :] for r in range(nh)], axis=1)


def _rep_rows(x, k):
    return jnp.concatenate([x] * k, axis=0)


def _mm_norm_body(x_ref, nw_ref, w_ref, o_ref, xn_ref):
    @pl.when(pl.program_id(1) == 0)
    def _():
        x = x_ref[...]
        ms = jnp.mean(x * x, axis=-1, keepdims=True)
        xn_ref[...] = (x * lax.rsqrt(ms + EPS) * nw_ref[...]).astype(BF16)

    o_ref[...] = _dot(xn_ref[...], w_ref[...])


def mm_norm(x, nw, w_bf, tn):
    n, d = x.shape
    nn = w_bf.shape[1]
    tm = _pick(n, (768, 512, 256, 128, 64, 32, 16))
    return pl.pallas_call(
        _mm_norm_body,
        grid=(n // tm, nn // tn),
        in_specs=[pl.BlockSpec((tm, d), lambda i, j: (i, 0)),
                  pl.BlockSpec((1, d), lambda i, j: (0, 0)),
                  pl.BlockSpec((d, tn), lambda i, j: (0, j))],
        out_specs=[pl.BlockSpec((tm, tn), lambda i, j: (i, j)),
                   pl.BlockSpec((tm, d), lambda i, j: (i, 0))],
        out_shape=[SDS((n, nn), F32), SDS((n, d), BF16)],
        compiler_params=_cparams(("parallel", "arbitrary")),
        name="mm_norm",
    )(x, nw.reshape(1, d), w_bf)


def _mm_res_body(a_ref, w_ref, r_ref, o_ref):
    o_ref[...] = r_ref[...] + _dot(a_ref[...], w_ref[...])


def mm_res(a_bf, w_bf, res):
    n, k = a_bf.shape
    nn = w_bf.shape[1]
    tm = _pick(n, (768, 512, 256, 128, 64, 32, 16))
    tn = _pick(nn, (1024, 512, 256, 128))
    return pl.pallas_call(
        _mm_res_body,
        grid=(n // tm, nn // tn),
        in_specs=[pl.BlockSpec((tm, k), lambda i, j: (i, 0)),
                  pl.BlockSpec((k, tn), lambda i, j: (0, j)),
                  pl.BlockSpec((tm, tn), lambda i, j: (i, j))],
        out_specs=pl.BlockSpec((tm, tn), lambda i, j: (i, j)),
        out_shape=SDS((n, nn), F32),
        compiler_params=_cparams(("parallel", "arbitrary")),
        name="mm_res",
    )(a_bf, w_bf, res)


def _rms_body(x_ref, nw_ref, o_ref):
    x = x_ref[...]
    ms = jnp.mean(x * x, axis=-1, keepdims=True)
    o_ref[...] = x * lax.rsqrt(ms + EPS) * nw_ref[...]


def rms_final(x, nw):
    n, d = x.shape
    tm = _pick(n, (768, 512, 256, 128, 64, 32, 16, 8))
    return pl.pallas_call(
        _rms_body,
        grid=(n // tm,),
        in_specs=[pl.BlockSpec((tm, d), lambda i: (i, 0)), pl.BlockSpec((1, d), lambda i: (0, 0))],
        out_specs=pl.BlockSpec((tm, d), lambda i: (i, 0)),
        out_shape=SDS((n, d), F32),
        compiler_params=_cparams(("parallel",)),
        name="rms_final",
    )(x, nw.reshape(1, d))


def _batcher_pairs(n):
    pairs = []
    p = 1
    while p < n:
        k = p
        while k >= 1:
            for j in range(k % p, n - k, 2 * k):
                for i in range(min(k, n - j - k)):
                    if (i + j) // (2 * p) == (i + j + k) // (2 * p):
                        pairs.append((i + j, i + j + k))
            k //= 2
        p *= 2
    return pairs


_SORT16 = _batcher_pairs(16)


def _sort16_desc(xs):
    xs = list(xs)
    for i, j in _SORT16:
        hi = jnp.maximum(xs[i], xs[j])
        lo = jnp.minimum(xs[i], xs[j])
        xs[i], xs[j] = hi, lo
    return xs


def _bitonic16_desc(c):
    c = list(c)
    for stride in (8, 4, 2, 1):
        for i in range(16):
            if i & stride == 0:
                hi = jnp.maximum(c[i], c[i + stride])
                lo = jnp.minimum(c[i], c[i + stride])
                c[i], c[i + stride] = hi, lo
    return c


def _merge16_desc(a, b):
    return _bitonic16_desc([jnp.maximum(a[i], b[15 - i]) for i in range(16)])


def _top16_sorted(s):
    cols = _sort16_desc([s[v * SUBLANES:(v + 1) * SUBLANES, :] for v in range(16)])
    for shift in (4, 2, 1):
        other = [pltpu.roll(x, shift, 0) for x in cols]
        cols = _merge16_desc(cols, other)
    return cols


def _peer_score_body(q_ref, k1_ref, k2_ref, s1_ref, s2_ref, st_ref):
    tm = q_ref.shape[0]
    half = PEER_DK // 2
    k1 = k1_ref[...].astype(BF16)
    k2 = k2_ref[...].astype(BF16)
    sub = lax.broadcasted_iota(I32, (SUBLANES, tm), 0)
    a_top = None
    b_top = None
    for h in range(PEER_HEADS):
        qh = q_ref[:, h * PEER_DK:(h + 1) * PEER_DK]
        s1 = _dot_nt(k1, qh[:, :half].astype(BF16))
        s2 = _dot_nt(k2, qh[:, half:].astype(BF16))
        s1_ref[h] = s1
        s2_ref[h] = s2
        a_h = _top16_sorted(s1)
        b_h = _top16_sorted(s2)
        if h == 0:
            a_top, b_top = a_h, b_h
        else:
            a_top = [jnp.where(sub == h, x, y) for x, y in zip(a_h, a_top)]
            b_top = [jnp.where(sub == h, x, y) for x, y in zip(b_h, b_top)]
    ninf = jnp.full((SUBLANES, tm), -jnp.inf, F32)
    row0 = [a_top[0] + b_top[b] for b in range(16)]
    col0 = [a_top[a] + b_top[0] for a in range(1, 16)] + [ninf]
    mid = ([a_top[1] + b_top[b] for b in range(1, 8)] + [a_top[a] + b_top[1] for a in range(2, 8)]
           + [a_top[2] + b_top[b] for b in range(2, 5)])
    mid = _sort16_desc(mid)
    v0 = a_top[3] + b_top[2]
    v1 = a_top[4] + b_top[2]
    v2 = a_top[3] + b_top[3]
    tail = [v0, jnp.maximum(v1, v2), jnp.minimum(v1, v2)] + [ninf] * 13
    top = _merge16_desc(_merge16_desc(_merge16_desc(row0, col0), mid), tail)
    z = jnp.zeros((SUBLANES, tm), F32)
    for i in range(16):
        z = z + jnp.exp(top[i] - top[0])
    st_ref[0] = top[15]
    st_ref[1] = a_top[0]
    st_ref[2] = b_top[0]
    st_ref[3] = z


def peer_score(q, k1, k2):
    n, d = q.shape
    nk = k1.shape[0]
    assert nk == 128 and d == PEER_HEADS * PEER_DK
    tm = _pick(n, (256, 128))
    return pl.pallas_call(
        _peer_score_body,
        grid=(n // tm,),
        in_specs=[pl.BlockSpec((tm, d), lambda i: (i, 0)),
                  pl.BlockSpec(k1.shape, lambda i: (0, 0)),
                  pl.BlockSpec(k2.shape, lambda i: (0, 0))],
        out_specs=[pl.BlockSpec((PEER_HEADS, nk, tm), lambda i: (0, 0, i)),
                   pl.BlockSpec((PEER_HEADS, nk, tm), lambda i: (0, 0, i)),
                   pl.BlockSpec((4, PEER_HEADS, tm), lambda i: (0, 0, i))],
        out_shape=[SDS((PEER_HEADS, nk, n), F32), SDS((PEER_HEADS, nk, n), F32),
                   SDS((4, PEER_HEADS, n), F32)],
        compiler_params=_cparams(("parallel",)),
        name="peer_score",
    )(q, k1, k2)


def _peer_dense_body(xn_ref, u_ref, vt_ref, s1_ref, s2_ref, st_ref, res_ref, o_ref,
                     acc_ref, e2_ref, h_ref, act_ref, *, nc, n_steps):
    c = pl.program_id(1)
    nk = s2_ref.shape[1]

    @pl.when(c == 0)
    def _():
        acc_ref[...] = jnp.zeros_like(acc_ref)
        for h in range(PEER_HEADS):
            e2_ref[h] = jnp.exp(s2_ref[h] - st_ref[2, h:h + 1, :])

    h_ref[...] = _dot_nt(u_ref[...], xn_ref[...])
    for s in range(nc):
        e1 = c * nc + s
        w = jnp.zeros((nk, xn_ref.shape[0]), F32)
        for h in range(PEER_HEADS):
            s1row = s1_ref[h, pl.ds(e1, 1), :]
            p1 = jnp.exp(s1row - st_ref[1, h:h + 1, :]) / st_ref[3, h:h + 1, :]
            a = s1row + s2_ref[h]
            w = w + jnp.where(a >= st_ref[0, h:h + 1, :], p1 * e2_ref[h], 0.0)
        act_ref[s * nk:(s + 1) * nk, :] = (_gelu(h_ref[s * nk:(s + 1) * nk, :]) * w).astype(BF16)
    acc_ref[...] += _dot(vt_ref[...], act_ref[...])

    @pl.when(c == n_steps - 1)
    def _():
        o_ref[...] = res_ref[...] + acc_ref[...].T


def peer_dense(xn_bf, u_bf, vt_bf, s1t, s2t, st, res):
    n, d = xn_bf.shape
    ne = u_bf.shape[0]
    nk = s1t.shape[1]
    tm = _pick(n, (768, 512, 256, 128))
    nc = 4
    te = nc * nk
    n_steps = ne // te
    body = functools.partial(_peer_dense_body, nc=nc, n_steps=n_steps)
    once = pl.Buffered(1)
    return pl.pallas_call(
        body,
        grid=(n // tm, n_steps),
        in_specs=[pl.BlockSpec((tm, d), lambda i, c: (i, 0), pipeline_mode=once),
                  pl.BlockSpec((te, d), lambda i, c: (c, 0)),
                  pl.BlockSpec((d, te), lambda i, c: (0, c)),
                  pl.BlockSpec((PEER_HEADS, nk, tm), lambda i, c: (0, 0, i), pipeline_mode=once),
                  pl.BlockSpec((PEER_HEADS, nk, tm), lambda i, c: (0, 0, i), pipeline_mode=once),
                  pl.BlockSpec((4, PEER_HEADS, tm), lambda i, c: (0, 0, i), pipeline_mode=once),
                  pl.BlockSpec((tm, d), lambda i, c: (i, 0), pipeline_mode=once)],
        out_specs=pl.BlockSpec((tm, d), lambda i, c: (i, 0)),
        out_shape=SDS((n, d), F32),
        scratch_shapes=[pltpu.VMEM((d, tm), F32), pltpu.VMEM((PEER_HEADS, nk, tm), F32),
                        pltpu.VMEM((te, tm), F32), pltpu.VMEM((te, tm), BF16)],
        compiler_params=_cparams(("parallel", "arbitrary")),
        name="peer_dense",
    )(xn_bf, u_bf, vt_bf, s1t, s2t, st, res)


def _flash_init(m_ref, l_ref, acc_ref):
    m_ref[...] = jnp.full(m_ref.shape, NEG, F32)
    l_ref[...] = jnp.zeros(l_ref.shape, F32)
    acc_ref[...] = jnp.zeros(acc_ref.shape, F32)


def _flash_update(s, mask, v_bf, m_ref, l_ref, acc_ref):
    s = jnp.where(mask, s, NEG)
    m_prev = m_ref[...]
    m_new = jnp.maximum(m_prev, jnp.max(s, axis=-1, keepdims=True))
    alpha = jnp.exp(m_prev - m_new)
    p = jnp.where(mask, jnp.exp(s - m_new), 0.0)
    l_ref[...] = alpha * l_ref[...] + jnp.sum(p, axis=-1, keepdims=True)
    acc_ref[...] = alpha * acc_ref[...] + _dot(p.astype(BF16), v_bf)
    m_ref[...] = m_new


def _flash_final(l_ref, acc_ref):
    return acc_ref[...] / jnp.maximum(l_ref[...], 1e-30)


def _softmax_masked(s, mask):
    s = jnp.where(mask, s, NEG)
    m = jnp.max(s, axis=-1, keepdims=True)
    e = jnp.where(mask, jnp.exp(s - m), 0.0)
    return e / jnp.maximum(jnp.sum(e, axis=-1, keepdims=True), 1e-30)


def _topk_mask(score, k):
    n = score.shape[-1]
    lane = lax.broadcasted_iota(I32, score.shape, 1).astype(F32)
    sel = jnp.zeros(score.shape, F32)
    for _ in range(k):
        m = jnp.max(score, axis=-1, keepdims=True)
        idx = jnp.min(jnp.where(score == m, lane, float(n)), axis=-1, keepdims=True)
        hit = lane == idx
        sel = jnp.where(hit, jnp.where(m >= 0.0, 1.0, 0.0), sel)
        score = jnp.where(hit, -jnp.inf, score)
    return sel


def _compress_body(x_ref, pe_ref, w1_ref, w2_ref, o_ref):
    xb = (x_ref[...] + pe_ref[...]).astype(BF16)
    h = _gelu(_dot(xb, w1_ref[...]))
    o_ref[...] = _dot(h.astype(BF16), w2_ref[...])


def compress(x, pe, w1_bf, w2_bf):
    rows, k = x.shape
    tr = _pick(rows, (256, 128, 64, 32, 16, 8))
    hid = w1_bf.shape[1]
    hd = w2_bf.shape[1]
    return pl.pallas_call(
        _compress_body,
        grid=(rows // tr,),
        in_specs=[pl.BlockSpec((tr, k), lambda i: (i, 0)),
                  pl.BlockSpec((1, k), lambda i: (0, 0)),
                  pl.BlockSpec((k, hid), lambda i: (0, 0)),
                  pl.BlockSpec((hid, hd), lambda i: (0, 0))],
        out_specs=pl.BlockSpec((tr, hd), lambda i: (i, 0)),
        out_shape=SDS((rows, hd), F32),
        compiler_params=_cparams(("parallel",)),
        name="nsa_compress",
    )(x, pe.reshape(1, k), w1_bf, w2_bf)


def _nsa_cmp_and_select(qs, slope, t1, kc_ref, vc_ref, *, r, nb, n_sel, ns_pad, k_top):
    tq = t1.shape[0]
    t = _rep_rows(t1, r)
    blk_end = lax.broadcasted_iota(I32, (1, nb), 1) * CMP_BLOCK + (CMP_BLOCK - 1)
    d_c = t - blk_end
    s = _dot_nt(qs, kc_ref[...].astype(BF16)) - slope * d_c.astype(F32)
    p_c = _softmax_masked(s, d_c >= 0)
    o_c = _dot(p_c.astype(BF16), vc_ref[...].astype(BF16))
    imp = p_c[0:tq]
    for i in range(1, r):
        imp = imp + p_c[i * tq:(i + 1) * tq]
    if ns_pad > nb:
        imp = jnp.concatenate([imp, jnp.zeros((tq, ns_pad - nb), F32)], axis=1)
    jsel = lax.broadcasted_iota(I32, (1, ns_pad), 1)
    cur = jnp.right_shift(t1, 6)
    forced = (jsel == 0) | (jsel == cur) | (jsel == cur - 1)
    score = jnp.where(forced, r + 1.0, jnp.where(jsel <= cur, imp, -1.0))
    score = jnp.where(jsel < n_sel, score, -2.0)
    return o_c, _topk_mask(score, k_top)


def _block_mask(sel_bf, first_pos, tk):
    ns_pad = sel_bf.shape[1]
    blk = lax.broadcasted_iota(I32, (ns_pad, tk), 0)
    key_blk = jnp.right_shift(first_pos + lax.broadcasted_iota(I32, (ns_pad, tk), 1), 6)
    expand = jnp.where(blk == key_blk, 1.0, 0.0).astype(BF16)
    return _dot(sel_bf, expand)


def _gate_cols(gsig, c, r):
    return jnp.concatenate([gsig[:, c * r + i:c * r + i + 1] for i in range(r)], axis=0)


def _nsa_p_body(q_ref, gt_ref, kc_ref, vc_ref, ks_ref, vs_ref, kw_ref, vw_ref, sl_ref, o_ref,
                m_ref, l_ref, acc_ref, *, tq, tk, nb, n_sel, k_top, scale, r):
    qi = pl.program_id(2)
    a = qi * tq
    qs = (_stack_heads(q_ref[...], r, HEAD_DIM) * scale).astype(BF16)
    slope = sl_ref[...]
    t1 = a + lax.broadcasted_iota(I32, (tq, 1), 0)
    t = _rep_rows(t1, r)
    o_c, sel = _nsa_cmp_and_select(qs, slope, t1, kc_ref, vc_ref, r=r, nb=nb, n_sel=n_sel,
                                   ns_pad=n_sel, k_top=k_top)
    sel_bf = sel.astype(BF16)
    n_it = (a + tq - 1) // tk + 1

    def sel_step(j, carry):
        start = pl.multiple_of(j * tk, tk)
        k = ks_ref[pl.ds(start, tk), :].astype(BF16)
        v = vs_ref[pl.ds(start, tk), :].astype(BF16)
        d = t - (start + lax.broadcasted_iota(I32, (1, tk), 1))
        s = _dot_nt(qs, k) - slope * d.astype(F32)
        bm = _rep_rows(_block_mask(sel_bf, start, tk), r)
        _flash_update(s, (bm > 0.5) & (d >= 0), v, m_ref, l_ref, acc_ref)
        return carry

    _flash_init(m_ref, l_ref, acc_ref)
    lax.fori_loop(0, n_it, sel_step, 0)
    o_s = _flash_final(l_ref, acc_ref)

    def win_step(j, carry):
        start = pl.multiple_of(j * tk, tk)
        k = kw_ref[pl.ds(start, tk), :].astype(BF16)
        v = vw_ref[pl.ds(start, tk), :].astype(BF16)
        d = t - (start + lax.broadcasted_iota(I32, (1, tk), 1))
        s = _dot_nt(qs, k) - slope * d.astype(F32)
        _flash_update(s, (d >= 0) & (d <= WINDOW), v, m_ref, l_ref, acc_ref)
        return carry

    _flash_init(m_ref, l_ref, acc_ref)
    lax.fori_loop(jnp.maximum(a - WINDOW, 0) // tk, n_it, win_step, 0)
    o_w = _flash_final(l_ref, acc_ref)

    gsig = 1.0 / (1.0 + jnp.exp(-gt_ref[...]))
    o = _gate_cols(gsig, 0, r) * o_c + _gate_cols(gsig, 1, r) * o_s + _gate_cols(gsig, 2, r) * o_w
    o_ref[...] = _unstack_heads(o, r, tq).astype(BF16)


def nsa_prompt(hab, kc, vc, slopes, cols, *, b, t, g, r):
    hd = HEAD_DIM
    nb = kc.shape[2]
    n_sel = -(-t // SEL_BLOCK)
    assert n_sel == nb and t % SEL_BLOCK == 0
    tq = Q_BLOCK if t % Q_BLOCK == 0 else t
    tk = 128 if t % 128 == 0 else t
    nq = t // tq
    qw = r * hd
    body = functools.partial(_nsa_p_body, tq=tq, tk=tk, nb=nb, n_sel=n_sel, k_top=min(SEL_TOPK, n_sel),
                             scale=hd ** -0.5, r=r)

    def seq_spec(col0):
        return pl.BlockSpec((t, hd), lambda bb, gg, qi: (bb, col0 // hd + gg))

    return pl.pallas_call(
        body,
        grid=(b, g, nq),
        in_specs=[pl.BlockSpec((tq, qw), lambda bb, gg, qi: (bb * nq + qi, gg)),
                  pl.BlockSpec((tq, LANES), lambda bb, gg, qi: (bb * nq + qi, cols["gt"] // LANES + gg)),
                  pl.BlockSpec((None, None, nb, hd), lambda bb, gg, qi: (bb, gg, 0, 0)),
                  pl.BlockSpec((None, None, nb, hd), lambda bb, gg, qi: (bb, gg, 0, 0)),
                  seq_spec(cols["ks"]), seq_spec(cols["vs"]), seq_spec(cols["kw"]), seq_spec(cols["vw"]),
                  pl.BlockSpec((None, r * tq, 1), lambda bb, gg, qi: (gg, 0, 0))],
        out_specs=pl.BlockSpec((tq, qw), lambda bb, gg, qi: (bb * nq + qi, gg)),
        out_shape=SDS((b * t, g * qw), BF16),
        scratch_shapes=[pltpu.VMEM((r * tq, 1), F32), pltpu.VMEM((r * tq, 1), F32),
                        pltpu.VMEM((r * tq, hd), F32)],
        compiler_params=_cparams(("parallel", "parallel", "arbitrary")),
        name="nsa_prompt",
    )(hab, hab, kc, vc, hab, hab, hab, hab, jnp.repeat(slopes, tq, axis=1).reshape(g, r * tq, 1))


def _pad_rows(x, rows):
    return jnp.concatenate([x, jnp.zeros((rows - x.shape[0], x.shape[1]), x.dtype)], axis=0)


def _nsa_s_body(pt_ref, q_ref, gt_ref, ksn_ref, vsn_ref, kwn_ref, vwn_ref, kc_ref, vc_ref, win_ref,
                sl_ref, *rest, pp, n_chunks, ts, offset, wb, nb, n_sel, ns_pad, k_top, scale, g, r):
    pages = rest[:pp]
    o_ref = rest[pp]
    qs_ref, oc_ref, sel_ref, m_ref, l_ref, acc_ref = rest[pp + 1:]
    del pt_ref
    c = pl.program_id(1)
    hd = HEAD_DIM
    page = pages[0].shape[1]
    t1 = offset + lax.broadcasted_iota(I32, (ts, 1), 0)
    t = _rep_rows(t1, r)
    lane = lax.broadcasted_iota(I32, (1, page), 1)

    @pl.when(c == 0)
    def _():
        for gg in range(g):
            qs = (_stack_heads(q_ref[:, gg * r * hd:(gg + 1) * r * hd], r, hd) * scale).astype(BF16)
            qs_ref[gg] = qs
            o_c, sel = _nsa_cmp_and_select(qs, sl_ref[gg], t1, kc_ref.at[gg], vc_ref.at[gg], r=r, nb=nb,
                                           n_sel=n_sel, ns_pad=ns_pad, k_top=k_top)
            oc_ref[gg] = o_c
            sel_ref[gg] = _rep_rows(sel, r).astype(BF16)
            _flash_init(m_ref.at[gg], l_ref.at[gg], acc_ref.at[gg])

    def attend(gg, k, v, first_pos, extra_mask):
        d = t - (first_pos + lane)
        s = _dot_nt(qs_ref[gg], k) - sl_ref[gg] * d.astype(F32)
        bm = _block_mask(sel_ref[gg], first_pos, page)
        mask = (bm > 0.5) & (d >= 0)
        if extra_mask is not None:
            mask = mask & extra_mask
        _flash_update(s, mask, v, m_ref.at[gg], l_ref.at[gg], acc_ref.at[gg])

    for gg in range(g):
        for i in range(pp):
            k = pages[i][0, :, gg * hd:(gg + 1) * hd].astype(BF16)
            v = pages[i][0, :, (g + gg) * hd:(g + gg + 1) * hd].astype(BF16)
            attend(gg, k, v, (c * pp + i) * page, None)

    @pl.when(c == n_chunks - 1)
    def _():
        gsig = 1.0 / (1.0 + jnp.exp(-gt_ref[...]))
        outs = []
        for gg in range(g):
            k = _pad_rows(ksn_ref[:, gg * hd:(gg + 1) * hd], page).astype(BF16)
            v = _pad_rows(vsn_ref[:, gg * hd:(gg + 1) * hd], page).astype(BF16)
            attend(gg, k, v, offset, lane < ts)
            o_s = _flash_final(l_ref.at[gg], acc_ref.at[gg])
            _flash_init(m_ref.at[gg], l_ref.at[gg], acc_ref.at[gg])
            for j in range(wb // page):
                k = win_ref[j * page:(j + 1) * page, gg * hd:(gg + 1) * hd].astype(BF16)
                v = win_ref[j * page:(j + 1) * page, (g + gg) * hd:(g + gg + 1) * hd].astype(BF16)
                d = t - (offset - wb + j * page + lane)
                s = _dot_nt(qs_ref[gg], k) - sl_ref[gg] * d.astype(F32)
                _flash_update(s, (d >= 0) & (d <= WINDOW), v, m_ref.at[gg], l_ref.at[gg], acc_ref.at[gg])
            k = _pad_rows(kwn_ref[:, gg * hd:(gg + 1) * hd], page).astype(BF16)
            v = _pad_rows(vwn_ref[:, gg * hd:(gg + 1) * hd], page).astype(BF16)
            d = t - (offset + lane)
            s = _dot_nt(qs_ref[gg], k) - sl_ref[gg] * d.astype(F32)
            _flash_update(s, (d >= 0) & (d <= WINDOW) & (lane < ts), v, m_ref.at[gg], l_ref.at[gg],
                          acc_ref.at[gg])
            o_w = _flash_final(l_ref.at[gg], acc_ref.at[gg])
            gs = gsig[:, gg * LANES:(gg + 1) * LANES]
            o = _gate_cols(gs, 0, r) * oc_ref[gg] + _gate_cols(gs, 1, r) * o_s + _gate_cols(gs, 2, r) * o_w
            outs.append(_unstack_heads(o, r, ts))
        o_ref[...] = jnp.concatenate(outs, axis=1)


def nsa_sample(hab, kc, vc, pool, win, page_table, slopes, cols, *, n_p, bs, ts, g, r):
    hd = HEAD_DIM
    page = pool.shape[1]
    n_pages = page_table.shape[1]
    offset = n_pages * page
    wb = win.shape[1]
    assert offset % SEL_BLOCK == 0 and wb % page == 0 and offset - wb >= 0 and wb == WINDOW
    nb = kc.shape[2]
    n_sel = -(-(offset + ts) // SEL_BLOCK)
    ns_pad = -(-n_sel // LANES) * LANES
    pp = _pick(n_pages, (8, 4, 2, 1))
    n_chunks = n_pages // pp
    half = 2 * g * hd
    rb = n_p // ts
    body = functools.partial(_nsa_s_body, pp=pp, n_chunks=n_chunks, ts=ts, offset=offset, wb=wb, nb=nb,
                             n_sel=n_sel, ns_pad=ns_pad, k_top=min(SEL_TOPK, n_sel), scale=hd ** -0.5,
                             g=g, r=r)

    def row_spec(width, col0):
        return pl.BlockSpec((ts, width), lambda b, c, pt: (rb + b, col0 // width))

    def page_spec(i):
        return pl.BlockSpec((1, page, half), lambda b, c, pt: (pt[b, c * pp + i], 0, 1))

    in_specs = [row_spec(g * r * hd, cols["q"]), row_spec(g * LANES, cols["gt"]),
                row_spec(g * hd, cols["ks"]), row_spec(g * hd, cols["vs"]),
                row_spec(g * hd, cols["kw"]), row_spec(g * hd, cols["vw"]),
                pl.BlockSpec((None, g, nb, hd), lambda b, c, pt: (b, 0, 0, 0)),
                pl.BlockSpec((None, g, nb, hd), lambda b, c, pt: (b, 0, 0, 0)),
                pl.BlockSpec((None, wb, half), lambda b, c, pt: (b, 0, 0)),
                pl.BlockSpec((g, r * ts, 1), lambda b, c, pt: (0, 0, 0))]
    in_specs += [page_spec(i) for i in range(pp)]
    grid_spec = pltpu.PrefetchScalarGridSpec(
        num_scalar_prefetch=1, grid=(bs, n_chunks), in_specs=in_specs,
        out_specs=pl.BlockSpec((ts, g * r * hd), lambda b, c, pt: (b, 0)),
        scratch_shapes=[pltpu.VMEM((g, r * ts, hd), BF16), pltpu.VMEM((g, r * ts, hd), F32),
                        pltpu.VMEM((g, r * ts, ns_pad), BF16), pltpu.VMEM((g, r * ts, 1), F32),
                        pltpu.VMEM((g, r * ts, 1), F32), pltpu.VMEM((g, r * ts, hd), F32)])
    return pl.pallas_call(
        body, grid_spec=grid_spec, out_shape=SDS((bs * ts, g * r * hd), F32),
        compiler_params=_cparams(("parallel", "arbitrary")), name="nsa_sample",
    )(page_table, hab, hab, hab, hab, hab, hab, kc, vc, win,
      jnp.repeat(slopes, ts, axis=1).reshape(g, r * ts, 1), *([pool] * pp))


def _ret_body(q_ref, k_ref, v_ref, g_ref, gn_ref, s0_ref, dm_ref, cr_ref, kd_ref, cd_ref, o_ref, s_ref,
              st_ref, *, scale, n_chunks):
    c = pl.program_id(2)

    @pl.when(c == 0)
    def _():
        st_ref[...] = s0_ref[...]

    qb = q_ref[...].astype(BF16)
    ks = k_ref[...] * scale
    kb = ks.astype(BF16)
    vb = v_ref[...].astype(BF16)
    st = st_ref[...]
    att = _dot_nt(qb, kb) * dm_ref[...]
    o = _dot(att.astype(BF16), vb) + _dot(qb, st.astype(BF16)) * cr_ref[...]
    st_new = st * cd_ref[...] + _dot_tn((ks * kd_ref[...]).astype(BF16), vb)
    st_ref[...] = st_new
    mu = jnp.mean(o, axis=-1, keepdims=True)
    var = jnp.mean(jnp.square(o - mu), axis=-1, keepdims=True)
    on = (o - mu) * lax.rsqrt(var + GN_EPS) * gn_ref[...]
    gate = g_ref[...]
    o_ref[...] = ((gate * (1.0 / (1.0 + jnp.exp(-gate)))) * on).astype(o_ref.dtype)

    @pl.when(c == n_chunks - 1)
    def _():
        s_ref[...] = st_new


def retention(hab, gn_w, s0, cols, *, row0, b, t, nh):
    dk = HEAD_DIM
    ch = RET_CHUNK if t % RET_CHUNK == 0 else t
    n_chunks = t // ch
    lg = jnp.log1p(-jnp.exp2(-5.0 - jnp.arange(nh, dtype=F32)))
    i = jnp.arange(ch, dtype=F32)
    diff = i[:, None] - i[None, :]
    dmask = jnp.where(diff >= 0, jnp.exp(jnp.maximum(diff, 0.0)[None] * lg[:, None, None]), 0.0)
    cross = jnp.exp((i + 1.0)[None, :] * lg[:, None]).reshape(nh, ch, 1)
    kdec = jnp.exp((ch - 1.0 - i)[None, :] * lg[:, None]).reshape(nh, ch, 1)
    cdec = jnp.exp(ch * lg).reshape(nh, 1, 1)
    rb = row0 // ch
    body = functools.partial(_ret_body, scale=dk ** -0.5, n_chunks=n_chunks)

    def col_spec(col0):
        return pl.BlockSpec((ch, dk), lambda bb, h, c: (rb + bb * n_chunks + c, col0 // dk + h))

    return pl.pallas_call(
        body,
        grid=(b, nh, n_chunks),
        in_specs=[col_spec(cols["rq"]), col_spec(cols["rk"]), col_spec(cols["rv"]), col_spec(cols["rg"]),
                  pl.BlockSpec((1, dk), lambda bb, h, c: (0, h)),
                  pl.BlockSpec((None, None, dk, dk), lambda bb, h, c: (bb, h, 0, 0)),
                  pl.BlockSpec((None, ch, ch), lambda bb, h, c: (h, 0, 0)),
                  pl.BlockSpec((None, ch, 1), lambda bb, h, c: (h, 0, 0)),
                  pl.BlockSpec((None, ch, 1), lambda bb, h, c: (h, 0, 0)),
                  pl.BlockSpec((None, 1, 1), lambda bb, h, c: (h, 0, 0))],
        out_specs=[pl.BlockSpec((ch, dk), lambda bb, h, c: (bb * n_chunks + c, h)),
                   pl.BlockSpec((None, None, dk, dk), lambda bb, h, c: (bb, h, 0, 0))],
        out_shape=[SDS((b * t, nh * dk), BF16 if ch % 16 == 0 else F32), SDS((b, nh, dk, dk), F32)],
        scratch_shapes=[pltpu.VMEM((dk, dk), F32)],
        compiler_params=_cparams(("parallel", "parallel", "arbitrary")),
        name="retention",
    )(hab, hab, hab, hab, gn_w.reshape(1, nh * dk), s0, dmask, cross, kdec, cdec)


def _logf_body(x_ref, b_ref, o_ref):
    x = x_ref[...] + b_ref[...]
    o_ref[...] = -(jnp.maximum(-x, 0.0) + jnp.log1p(jnp.exp(-jnp.abs(x))))


def fox_logf(hc, b_pad, col0):
    n = hc.shape[0]
    tm = _pick(n, (768, 512, 256, 128, 64, 32, 16, 8))
    return pl.pallas_call(
        _logf_body,
        grid=(n // tm,),
        in_specs=[pl.BlockSpec((tm, LANES), lambda i: (i, col0 // LANES)),
                  pl.BlockSpec((1, LANES), lambda i: (0, 0))],
        out_specs=pl.BlockSpec((tm, LANES), lambda i: (i, 0)),
        out_shape=SDS((n, LANES), F32),
        compiler_params=_cparams(("parallel",)),
        name="fox_logf",
    )(hc, b_pad)


def _cumsum_rows_body(x_ref, o_ref, carry_ref):
    @pl.when(pl.program_id(1) == 0)
    def _():
        carry_ref[...] = jnp.zeros_like(carry_ref)

    tc = x_ref.shape[0]
    tri = jnp.where(lax.broadcasted_iota(I32, (tc, tc), 1) <= lax.broadcasted_iota(I32, (tc, tc), 0),
                    1.0, 0.0).astype(BF16)
    hi, mid, lo = _split3(x_ref[...])
    f = (_dot(tri, hi) + _dot(tri, mid)) + _dot(tri, lo) + carry_ref[...]
    o_ref[...] = f
    carry_ref[...] = f[tc - 1:tc, :]


def cumsum_rows(x, *, b, t):
    tc = 128 if t % 128 == 0 else t
    nc = t // tc
    return pl.pallas_call(
        _cumsum_rows_body,
        grid=(b, nc),
        in_specs=[pl.BlockSpec((tc, LANES), lambda bb, c: (bb * nc + c, 0))],
        out_specs=pl.BlockSpec((tc, LANES), lambda bb, c: (bb * nc + c, 0)),
        out_shape=SDS((b * t, LANES), F32),
        scratch_shapes=[pltpu.VMEM((1, LANES), F32)],
        compiler_params=_cparams(("parallel", "arbitrary")),
        name="fox_cumsum",
    )(x)


def _fox_p_body(q_ref, k_ref, v_ref, fq_ref, fk_ref, o_ref, qs_ref, m_ref, l_ref, acc_ref,
                *, tq, tk, nk, scale, r):
    qi = pl.program_id(2)
    kj = pl.program_id(3)

    @pl.when(kj == 0)
    def _():
        qs_ref[...] = (_stack_heads(q_ref[...], r, HEAD_DIM) * scale).astype(BF16)
        _flash_init(m_ref, l_ref, acc_ref)

    @pl.when(kj * tk <= qi * tq + tq - 1)
    def _():
        s = _dot_nt(qs_ref[...], k_ref[...].astype(BF16))
        fq = jnp.concatenate([fq_ref[:, i:i + 1] for i in range(r)], axis=0)
        fk = jnp.concatenate([jnp.broadcast_to(fk_ref[i:i + 1, :], (tq, tk)) for i in range(r)], axis=0)
        s = s + fq - fk
        t = _rep_rows(qi * tq + lax.broadcasted_iota(I32, (tq, 1), 0), r)
        p = kj * tk + lax.broadcasted_iota(I32, (1, tk), 1)
        _flash_update(s, p <= t, v_ref[...].astype(BF16), m_ref, l_ref, acc_ref)

    @pl.when(kj == nk - 1)
    def _():
        o_ref[...] = _unstack_heads(_flash_final(l_ref, acc_ref), r, tq).astype(BF16)


def fox_prompt(hc, fq, fkt, cols, *, b, t, g, r):
    hd = HEAD_DIM
    tq = 256 if t % 256 == 0 else t
    tk = tq
    nq = t // tq
    nk = t // tk
    qw = r * hd
    body = functools.partial(_fox_p_body, tq=tq, tk=tk, nk=nk, scale=hd ** -0.5, r=r)

    def kblk(qi, kj):
        return jnp.minimum(kj, (qi * tq + tq - 1) // tk)

    return pl.pallas_call(
        body,
        grid=(b, g, nq, nk),
        in_specs=[pl.BlockSpec((tq, qw), lambda bb, gg, qi, kj: (bb * nq + qi, gg)),
                  pl.BlockSpec((tk, hd), lambda bb, gg, qi, kj: (bb * nk + kblk(qi, kj), cols["k"] // hd + gg)),
                  pl.BlockSpec((tk, hd), lambda bb, gg, qi, kj: (bb * nk + kblk(qi, kj), cols["v"] // hd + gg)),
                  pl.BlockSpec((None, None, tq, r), lambda bb, gg, qi, kj: (bb, gg, qi, 0)),
                  pl.BlockSpec((None, None, r, tk), lambda bb, gg, qi, kj: (bb, gg, 0, kblk(qi, kj)))],
        out_specs=pl.BlockSpec((tq, qw), lambda bb, gg, qi, kj: (bb * nq + qi, gg)),
        out_shape=SDS((b * t, g * qw), BF16),
        scratch_shapes=[pltpu.VMEM((r * tq, hd), BF16), pltpu.VMEM((r * tq, 1), F32),
                        pltpu.VMEM((r * tq, 1), F32), pltpu.VMEM((r * tq, hd), F32)],
        compiler_params=_cparams(("parallel", "parallel", "parallel", "arbitrary")),
        name="fox_prompt",
    )(hc, hc, hc, fq, fkt)


def _fox_f_body(pt_ref, new_ref, *rest, pp, n_chunks):
    pages = rest[:pp]
    fk_ref, fn_ref, carry_ref = rest[pp:]
    del pt_ref
    c = pl.program_id(1)
    page = pages[0].shape[2]

    @pl.when(c == 0)
    def _():
        carry_ref[...] = jnp.zeros_like(carry_ref)

    ut = jnp.where(lax.broadcasted_iota(I32, (page, page), 0) <= lax.broadcasted_iota(I32, (page, page), 1),
                   1.0, 0.0).astype(BF16)

    def csum(x, carry):
        hi, mid, lo = _split3(x)
        return (_dot(hi, ut) + _dot(mid, ut)) + _dot(lo, ut) + carry

    carry = carry_ref[...]
    for i in range(pp):
        f = csum(pages[i][0], carry)
        fk_ref[:, i * page:(i + 1) * page] = f
        carry = f[:, page - 1:page]
    carry_ref[...] = carry

    @pl.when(c == n_chunks - 1)
    def _():
        fn_ref[...] = csum(new_ref[...], carry)


def fox_f_sample(logf_pool_t, new_t, page_table):
    bs, n_pages = page_table.shape
    _, nh, page = logf_pool_t.shape
    pp = _pick(n_pages, (16, 8, 4, 2, 1))
    n_chunks = n_pages // pp
    body = functools.partial(_fox_f_body, pp=pp, n_chunks=n_chunks)
    in_specs = [pl.BlockSpec((None, nh, page), lambda b, c, pt: (b, 0, 0))]
    in_specs += [pl.BlockSpec((1, nh, page), functools.partial(lambda b, c, pt, i: (pt[b, c * pp + i], 0, 0), i=i))
                 for i in range(pp)]
    grid_spec = pltpu.PrefetchScalarGridSpec(
        num_scalar_prefetch=1, grid=(bs, n_chunks), in_specs=in_specs,
        out_specs=[pl.BlockSpec((None, nh, pp * page), lambda b, c, pt: (b, 0, c)),
                   pl.BlockSpec((None, nh, page), lambda b, c, pt: (b, 0, 0))],
        scratch_shapes=[pltpu.VMEM((nh, 1), F32)])
    return pl.pallas_call(
        body, grid_spec=grid_spec,
        out_shape=[SDS((bs, nh, n_pages * page), F32), SDS((bs, nh, page), F32)],
        compiler_params=_cparams(("parallel", "arbitrary")), name="fox_f_sample",
    )(page_table, new_t, *([logf_pool_t] * pp))


def _fox_s_body(pt_ref, q_ref, kn_ref, vn_ref, fq_ref, fk_ref, fn_ref, *rest, pp, n_chunks, ts, offset,
                scale, g, r):
    pages = rest[:pp]
    o_ref = rest[pp]
    qs_ref, m_ref, l_ref, acc_ref = rest[pp + 1:]
    del pt_ref
    c = pl.program_id(1)
    hd = HEAD_DIM
    page = pages[0].shape[1]

    @pl.when(c == 0)
    def _():
        for gg in range(g):
            qs_ref[gg] = (_stack_heads(q_ref[:, gg * r * hd:(gg + 1) * r * hd], r, hd) * scale).astype(BF16)
            _flash_init(m_ref.at[gg], l_ref.at[gg], acc_ref.at[gg])

    def fk_rows(f, gg):
        return jnp.concatenate([jnp.broadcast_to(f[gg * r + i:gg * r + i + 1, :], (ts, page))
                                for i in range(r)], axis=0)

    all_keys = jnp.full((r * ts, page), True)
    for gg in range(g):
        for i in range(pp):
            k = pages[i][0, :, gg * hd:(gg + 1) * hd].astype(BF16)
            v = pages[i][0, :, (g + gg) * hd:(g + gg + 1) * hd].astype(BF16)
            s = _dot_nt(qs_ref[gg], k) + fq_ref[gg] - fk_rows(fk_ref[:, i * page:(i + 1) * page], gg)
            _flash_update(s, all_keys, v, m_ref.at[gg], l_ref.at[gg], acc_ref.at[gg])

    @pl.when(c == n_chunks - 1)
    def _():
        t = _rep_rows(offset + lax.broadcasted_iota(I32, (ts, 1), 0), r)
        lane = lax.broadcasted_iota(I32, (1, page), 1)
        outs = []
        for gg in range(g):
            k = _pad_rows(kn_ref[:, gg * hd:(gg + 1) * hd], page).astype(BF16)
            v = _pad_rows(vn_ref[:, gg * hd:(gg + 1) * hd], page).astype(BF16)
            s = _dot_nt(qs_ref[gg], k) + fq_ref[gg] - fk_rows(fn_ref[...], gg)
            _flash_update(s, (offset + lane <= t) & (lane < ts), v, m_ref.at[gg], l_ref.at[gg], acc_ref.at[gg])
            outs.append(_unstack_heads(_flash_final(l_ref.at[gg], acc_ref.at[gg]), r, ts))
        o_ref[...] = jnp.concatenate(outs, axis=1)


def fox_sample(hc, fq, fk, fn, pool, page_table, cols, *, n_p, bs, ts, g, r):
    hd = HEAD_DIM
    page = pool.shape[1]
    n_pages = page_table.shape[1]
    nh = g * r
    offset = n_pages * page
    pp = _pick(n_pages, (8, 4, 2, 1))
    n_chunks = n_pages // pp
    rb = n_p // ts
    body = functools.partial(_fox_s_body, pp=pp, n_chunks=n_chunks, ts=ts, offset=offset, scale=hd ** -0.5,
                             g=g, r=r)

    def row_spec(width, col0):
        return pl.BlockSpec((ts, width), lambda b, c, pt: (rb + b, col0 // width))

    in_specs = [row_spec(nh * hd, cols["q"]), row_spec(g * hd, cols["k"]), row_spec(g * hd, cols["v"]),
                pl.BlockSpec((None, g, r * ts, 1), lambda b, c, pt: (b, 0, 0, 0)),
                pl.BlockSpec((None, nh, pp * page), lambda b, c, pt: (b, 0, c)),
                pl.BlockSpec((None, nh, page), lambda b, c, pt: (b, 0, 0))]
    in_specs += [pl.BlockSpec((1, page, 2 * g * hd),
                              functools.partial(lambda b, c, pt, i: (pt[b, c * pp + i], 0, 0), i=i))
                 for i in range(pp)]
    grid_spec = pltpu.PrefetchScalarGridSpec(
        num_scalar_prefetch=1, grid=(bs, n_chunks), in_specs=in_specs,
        out_specs=pl.BlockSpec((ts, nh * hd), lambda b, c, pt: (b, 0)),
        scratch_shapes=[pltpu.VMEM((g, r * ts, hd), BF16), pltpu.VMEM((g, r * ts, 1), F32),
                        pltpu.VMEM((g, r * ts, 1), F32), pltpu.VMEM((g, r * ts, hd), F32)])
    return pl.pallas_call(
        body, grid_spec=grid_spec, out_shape=SDS((bs * ts, nh * hd), F32),
        compiler_params=_cparams(("parallel", "arbitrary")), name="fox_sample",
    )(page_table, hc, hc, hc, fq, fk, fn, *([pool] * pp))


def _ab_layout(nh_a, g, r, nh_r):
    hd = HEAD_DIM
    qa, kv = nh_a * hd, g * hd
    sizes = [("q", qa), ("kc", kv), ("vc", kv), ("ks", kv), ("vs", kv), ("kw", kv), ("vw", kv),
             ("gt_src", nh_a * 3), ("rq", nh_r * hd), ("rk", nh_r * hd), ("rv", nh_r * hd), ("rg", nh_r * hd)]
    src = {}
    pos = 0
    for name, w in sizes:
        src[name] = pos
        pos += w
    order = ["q", "kc", "vc", "ks", "vs", "kw", "vw", "rq", "rk", "rv", "rg"]
    widths = dict(sizes)
    idx = []
    cols = {}
    for name in order:
        cols[name] = len(idx)
        idx += list(range(src[name], src[name] + widths[name]))
    cols["gt"] = len(idx)
    for gg in range(g):
        blk = [-1] * LANES
        for c in range(3):
            for i in range(r):
                blk[c * r + i] = src["gt_src"] + (gg * r + i) * 3 + c
        idx += blk
    return np.array(idx, np.int32), cols


def _even_layer(x, n_p, b, t, bs, ts, p, cache_kv, cache_win, state, page_table):
    hd = HEAD_DIM
    d = x.shape[1]
    nh_a = d // (2 * hd)
    g = nh_a // 4
    r = nh_a // g
    nh_r = d // (2 * hd)
    idx, cols = _ab_layout(nh_a, g, r, nh_r)
    tn = 768
    ncol = -(-len(idx) // tn) * tn
    idx = np.concatenate([idx, np.full(ncol - len(idx), -1, np.int32)])
    w_in = jnp.where(idx[None, :] >= 0, jnp.take(p["w_in"], np.maximum(idx, 0), axis=1), 0.0).astype(BF16)
    hab, _ = mm_norm(x, p["norm"], w_in, tn)

    kv4 = 4 * g * hd
    c_rows = cols["kc"]
    c_win = cols["kw"]
    new_rows_p = hab[:n_p, c_rows:c_rows + kv4].reshape(b, t, 4, g, hd)
    new_rows_s = hab[n_p:, c_rows:c_rows + kv4].reshape(bs, ts, 4, g, hd)
    new_win_p = hab[:n_p, c_win:c_win + 2 * g * hd].reshape(b, t, 2, g, hd)
    new_win_s = hab[n_p:, c_win:c_win + 2 * g * hd].reshape(bs, ts, 2, g, hd)
    win_state_p = new_win_p[:, -min(WINDOW, t):]
    wb = cache_win.shape[1]
    win_state_s = jnp.concatenate([cache_win, new_win_s], axis=1)[:, -wb:]

    slopes = jnp.exp2(-8.0 * (jnp.arange(nh_a, dtype=F32) + 1.0) / nh_a).reshape(g, r)
    w1k, w2k = p["w1_k"].astype(BF16), p["w2_k"].astype(BF16)
    w1v, w2v = p["w1_v"].astype(BF16), p["w2_v"].astype(BF16)

    def cmp_pair(blocks):
        bb, nb = blocks.shape[:2]
        flat = blocks.transpose(3, 0, 1, 4, 2, 5).reshape(2, bb * nb * g, CMP_BLOCK * hd)
        kc = compress(flat[0], p["pe_k"].reshape(-1), w1k, w2k).reshape(bb, nb, g, hd).transpose(0, 2, 1, 3)
        vc = compress(flat[1], p["pe_v"].reshape(-1), w1v, w2v).reshape(bb, nb, g, hd).transpose(0, 2, 1, 3)
        return kc, vc

    nb_p = t // CMP_BLOCK
    kc_p, vc_p = cmp_pair(new_rows_p[:, :nb_p * CMP_BLOCK, 0:2].reshape(b, nb_p, CMP_BLOCK, 2, g, hd))
    o_a_p = nsa_prompt(hab, kc_p, vc_p, slopes, cols, b=b, t=t, g=g, r=r)

    page = cache_kv.shape[1]
    n_pages = page_table.shape[1]
    past = n_pages * page
    nb_s = (past + ts) // CMP_BLOCK
    assert nb_s * CMP_BLOCK <= past
    past_cmp = cache_kv[page_table][:, :, :, 0:2].reshape(bs, past, 2, g, hd)
    kc_s, vc_s = cmp_pair(past_cmp[:, :nb_s * CMP_BLOCK].reshape(bs, nb_s, CMP_BLOCK, 2, g, hd))
    o_a_s = nsa_sample(hab, kc_s, vc_s, cache_kv.reshape(cache_kv.shape[0], page, kv4),
                       cache_win.reshape(bs, wb, 2 * g * hd), page_table, slopes, cols,
                       n_p=n_p, bs=bs, ts=ts, g=g, r=r)

    o_b_p, st_p = retention(hab, p["gn"], jnp.zeros((b, nh_r, hd, hd), F32), cols, row0=0, b=b, t=t, nh=nh_r)
    o_b_s, st_s = retention(hab, p["gn"], state, cols, row0=n_p, b=bs, t=ts, nh=nh_r)

    o = jnp.concatenate([jnp.concatenate([o_a_p, o_b_p.astype(BF16)], axis=1),
                         jnp.concatenate([o_a_s.astype(BF16), o_b_s.astype(BF16)], axis=1)], axis=0)
    x = mm_res(o, p["w_out"].astype(BF16), x)
    return x, (new_rows_p, win_state_p, st_p, new_rows_s, win_state_s, st_s)


def _odd_layer(x, n_p, b, t, bs, ts, p, cache_kv, cache_logf, page_table):
    hd = HEAD_DIM
    d = x.shape[1]
    nh = d // hd
    g = nh // 4
    r = nh // g
    cols = {"q": 0, "k": nh * hd, "v": (nh + g) * hd, "f": (nh + 2 * g) * hd}
    ncol_src = p["w_in"].shape[1]
    tn = 640
    ncol = -(-(cols["f"] + LANES) // tn) * tn
    w_in = jnp.pad(p["w_in"], ((0, 0), (0, ncol - ncol_src))).astype(BF16)
    hc, _ = mm_norm(x, p["norm"], w_in, tn)

    new_rows_p = hc[:n_p, cols["k"]:cols["f"]].reshape(b, t, 2, g, hd)
    new_rows_s = hc[n_p:, cols["k"]:cols["f"]].reshape(bs, ts, 2, g, hd)
    b_pad = jnp.pad(p["b_f"].astype(F32), (0, LANES - nh)).reshape(1, LANES)
    logf = fox_logf(hc, b_pad, cols["f"])
    new_logf_p = logf[:n_p, :nh].reshape(b, t, nh)
    new_logf_s = logf[n_p:, :nh].reshape(bs, ts, nh)

    f_p = cumsum_rows(logf, b=b, t=t)[:, :nh].reshape(b, t, g, r)
    o_p = fox_prompt(hc, f_p.transpose(0, 2, 1, 3), f_p.transpose(0, 2, 3, 1), cols, b=b, t=t, g=g, r=r)

    page = cache_kv.shape[1]
    new_t = jnp.pad(new_logf_s.transpose(0, 2, 1), ((0, 0), (0, 0), (0, page - ts)))
    fk, fn = fox_f_sample(cache_logf.transpose(0, 2, 1), new_t, page_table)
    fq = fn[:, :, :ts].reshape(bs, g, r * ts, 1)
    o_s = fox_sample(hc, fq, fk, fn, cache_kv.reshape(cache_kv.shape[0], page, 2 * g * hd), page_table, cols,
                     n_p=n_p, bs=bs, ts=ts, g=g, r=r)

    x = mm_res(jnp.concatenate([o_p, o_s.astype(BF16)], axis=0), p["w_out"].astype(BF16), x)
    return x, (new_rows_p, new_logf_p, new_rows_s, new_logf_s)


def _peer_layer(x, nw, wq, k1, k2, u, v):
    q, xn = mm_norm(x, nw, wq.astype(BF16), _pick(wq.shape[1], (1024, 512, 256, 128)))
    s1t, s2t, st = peer_score(q, k1, k2)
    return peer_dense(xn, u.astype(BF16), v.T.astype(BF16), s1t, s2t, st, x)


def kernel(x_prompt, x_sample, cache_nsa_kv, cache_nsa_win, state_ret, cache_fox_kv, cache_fox_logf,
           page_table, norm_mix, norm_ffn, norm_final, w_in_ab, w_out_ab, cmp_pe_k, cmp_w1_k, cmp_w2_k,
           cmp_pe_v, cmp_w1_v, cmp_w2_v, ret_gn, w_in_c, b_forget, w_out_c, peer_wq, peer_k1, peer_k2,
           peer_u, peer_v):
    b, t, d = x_prompt.shape
    bs, ts, _ = x_sample.shape
    n_p, n_s = b * t, bs * ts
    depth = norm_mix.shape[0]
    x = jnp.concatenate([x_prompt.reshape(n_p, d), x_sample.reshape(n_s, d)], axis=0)
    even, odd = [], []
    for l in range(depth):
        if l % 2 == 0:
            e = l // 2
            p = dict(norm=norm_mix[l], w_in=w_in_ab[e], w_out=w_out_ab[e], pe_k=cmp_pe_k[e], w1_k=cmp_w1_k[e],
                     w2_k=cmp_w2_k[e], pe_v=cmp_pe_v[e], w1_v=cmp_w1_v[e], w2_v=cmp_w2_v[e], gn=ret_gn[e])
            x, outs = _even_layer(x, n_p, b, t, bs, ts, p, cache_nsa_kv[e], cache_nsa_win[e], state_ret[e],
                                  page_table)
            even.append(outs)
        else:
            o = l // 2
            p = dict(norm=norm_mix[l], w_in=w_in_c[o], b_f=b_forget[o], w_out=w_out_c[o])
            x, outs = _odd_layer(x, n_p, b, t, bs, ts, p, cache_fox_kv[o], cache_fox_logf[o], page_table)
            odd.append(outs)
        x = _peer_layer(x, norm_ffn[l], peer_wq[l], peer_k1[l], peer_k2[l], peer_u[l], peer_v[l])
    y = rms_final(x, norm_final)
    y_prompt = y[:n_p].reshape(b, t, d)
    y_sample = y[n_p:].reshape(bs, ts, d)

    def stack(group, i):
        return jnp.stack([o[i] for o in group])

    return (y_prompt, y_sample, stack(even, 0), stack(even, 1), stack(even, 2), stack(odd, 0), stack(odd, 1),
            stack(even, 3), stack(even, 4), stack(even, 5), stack(odd, 2), stack(odd, 3))
```

```python
import functools
import math

import numpy as np
import jax
import jax.numpy as jnp
from jax import lax
from jax.experimental import pallas as pl
from jax.experimental.pallas import tpu as pltpu

F32 = jnp.float32
BF16 = jnp.bfloat16
I32 = jnp.int32

HEAD_DIM = 128
CMP_BLOCK = 64
SEL_BLOCK = 64
SEL_TOPK = 16
WINDOW = 512
CMP_HIDDEN = 256
RET_CHUNK = 128
Q_BLOCK = 128
PEER_HEADS = 8
PEER_DK = 256
PEER_TOPK = 16
EPS = 1e-6
GN_EPS = 1e-5
NEG = -1e30

LANES = 128
SUBLANES = 8
VMEM_LIMIT = 56 * 1024 * 1024

SDS = jax.ShapeDtypeStruct


def _cparams(sem):
    return pltpu.CompilerParams(dimension_semantics=sem, vmem_limit_bytes=VMEM_LIMIT)


def _dot(a, b):
    return jnp.dot(a, b, preferred_element_type=F32)


def _dot_nt(a, b):
    return lax.dot_general(a, b, (((1,), (1,)), ((), ())), preferred_element_type=F32)


def _dot_tn(a, b):
    return lax.dot_general(a, b, (((0,), (0,)), ((), ())), preferred_element_type=F32)


def _pick(n, cands):
    for c in cands:
        if c <= n and n % c == 0:
            return c
    raise ValueError(f"no tile for {n} in {cands}")


def _gelu(x):
    c = math.sqrt(2.0 / math.pi)
    return x * (0.5 * (1.0 + jnp.tanh(c * (x + 0.044715 * (x * x * x)))))


def _split3(x):
    hi = x.astype(BF16)
    r1 = x - hi.astype(F32)
    mid = r1.astype(BF16)
    lo = (r1 - mid.astype(F32)).astype(BF16)
    return hi, mid, lo


def _stack_heads(x, nh, hd):
    return jnp.concatenate([x[:, r * hd:(r + 1) * hd] for r in range(nh)], axis=0)


def _unstack_heads(x, nh, t):
    return jnp.concatenate([x[r * t:(r + 1) * t, :] for r in range(nh)], axis=1)


def _rep_rows(x, k):
    return jnp.concatenate([x] * k, axis=0)


def _mm_norm_body(x_ref, nw_ref, w_ref, o_ref, xn_ref):
    @pl.when(pl.program_id(1) == 0)
    def _():
        x = x_ref[...]
        ms = jnp.mean(x * x, axis=-1, keepdims=True)
        xn_ref[...] = (x * lax.rsqrt(ms + EPS) * nw_ref[...]).astype(BF16)

    o_ref[...] = _dot(xn_ref[...], w_ref[...])


def mm_norm(x, nw, w_bf, tn):
    n, d = x.shape
    nn = w_bf.shape[1]
    tm = _pick(n, (768, 512, 256, 128, 64, 32, 16))
    return pl.pallas_call(
        _mm_norm_body,
        grid=(n // tm, nn // tn),
        in_specs=[pl.BlockSpec((tm, d), lambda i, j: (i, 0)),
                  pl.BlockSpec((1, d), lambda i, j: (0, 0)),
                  pl.BlockSpec((d, tn), lambda i, j: (0, j))],
        out_specs=[pl.BlockSpec((tm, tn), lambda i, j: (i, j)),
                   pl.BlockSpec((tm, d), lambda i, j: (i, 0))],
        out_shape=[SDS((n, nn), F32), SDS((n, d), BF16)],
        compiler_params=_cparams(("parallel", "arbitrary")),
        name="mm_norm",
    )(x, nw.reshape(1, d), w_bf)


def _mm_res_body(a_ref, w_ref, r_ref, o_ref):
    o_ref[...] = r_ref[...] + _dot(a_ref[...], w_ref[...])


def mm_res(a_bf, w_bf, res):
    n, k = a_bf.shape
    nn = w_bf.shape[1]
    tm = _pick(n, (768, 512, 256, 128, 64, 32, 16))
    tn = _pick(nn, (1024, 512, 256, 128))
    return pl.pallas_call(
        _mm_res_body,
        grid=(n // tm, nn // tn),
        in_specs=[pl.BlockSpec((tm, k), lambda i, j: (i, 0)),
                  pl.BlockSpec((k, tn), lambda i, j: (0, j)),
                  pl.BlockSpec((tm, tn), lambda i, j: (i, j))],
        out_specs=pl.BlockSpec((tm, tn), lambda i, j: (i, j)),
        out_shape=SDS((n, nn), F32),
        compiler_params=_cparams(("parallel", "arbitrary")),
        name="mm_res",
    )(a_bf, w_bf, res)


def _rms_body(x_ref, nw_ref, o_ref):
    x = x_ref[...]
    ms = jnp.mean(x * x, axis=-1, keepdims=True)
    o_ref[...] = x * lax.rsqrt(ms + EPS) * nw_ref[...]


def rms_final(x, nw):
    n, d = x.shape
    tm = _pick(n, (768, 512, 256, 128, 64, 32, 16, 8))
    return pl.pallas_call(
        _rms_body,
        grid=(n // tm,),
        in_specs=[pl.BlockSpec((tm, d), lambda i: (i, 0)), pl.BlockSpec((1, d), lambda i: (0, 0))],
        out_specs=pl.BlockSpec((tm, d), lambda i: (i, 0)),
        out_shape=SDS((n, d), F32),
        compiler_params=_cparams(("parallel",)),
        name="rms_final",
    )(x, nw.reshape(1, d))


def _batcher_pairs(n):
    pairs = []
    p = 1
    while p < n:
        k = p
        while k >= 1:
            for j in range(k % p, n - k, 2 * k):
                for i in range(min(k, n - j - k)):
                    if (i + j) // (2 * p) == (i + j + k) // (2 * p):
                        pairs.append((i + j, i + j + k))
            k //= 2
        p *= 2
    return pairs


_SORT16 = _batcher_pairs(16)


def _sort16_desc(xs):
    xs = list(xs)
    for i, j in _SORT16:
        hi = jnp.maximum(xs[i], xs[j])
        lo = jnp.minimum(xs[i], xs[j])
        xs[i], xs[j] = hi, lo
    return xs


def _bitonic16_desc(c):
    c = list(c)
    for stride in (8, 4, 2, 1):
        for i in range(16):
            if i & stride == 0:
                hi = jnp.maximum(c[i], c[i + stride])
                lo = jnp.minimum(c[i], c[i + stride])
                c[i], c[i + stride] = hi, lo
    return c


def _merge16_desc(a, b):
    return _bitonic16_desc([jnp.maximum(a[i], b[15 - i]) for i in range(16)])


def _top16_sorted(s):
    cols = _sort16_desc([s[v * SUBLANES:(v + 1) * SUBLANES, :] for v in range(16)])
    for shift in (4, 2, 1):
        other = [pltpu.roll(x, shift, 0) for x in cols]
        cols = _merge16_desc(cols, other)
    return cols


def _peer_score_body(q_ref, k1_ref, k2_ref, s1_ref, s2_ref, st_ref):
    tm = q_ref.shape[0]
    half = PEER_DK // 2
    k1 = k1_ref[...].astype(BF16)
    k2 = k2_ref[...].astype(BF16)
    sub = lax.broadcasted_iota(I32, (SUBLANES, tm), 0)
    a_top = None
    b_top = None
    for h in range(PEER_HEADS):
        qh = q_ref[:, h * PEER_DK:(h + 1) * PEER_DK]
        s1 = _dot_nt(k1, qh[:, :half].astype(BF16))
        s2 = _dot_nt(k2, qh[:, half:].astype(BF16))
        s1_ref[h] = s1
        s2_ref[h] = s2
        a_h = _top16_sorted(s1)
        b_h = _top16_sorted(s2)
        if h == 0:
            a_top, b_top = a_h, b_h
        else:
            a_top = [jnp.where(sub == h, x, y) for x, y in zip(a_h, a_top)]
            b_top = [jnp.where(sub == h, x, y) for x, y in zip(b_h, b_top)]
    ninf = jnp.full((SUBLANES, tm), -jnp.inf, F32)
    row0 = [a_top[0] + b_top[b] for b in range(16)]
    col0 = [a_top[a] + b_top[0] for a in range(1, 16)] + [ninf]
    mid = ([a_top[1] + b_top[b] for b in range(1, 8)] + [a_top[a] + b_top[1] for a in range(2, 8)]
           + [a_top[2] + b_top[b] for b in range(2, 5)])
    mid = _sort16_desc(mid)
    v0 = a_top[3] + b_top[2]
    v1 = a_top[4] + b_top[2]
    v2 = a_top[3] + b_top[3]
    tail = [v0, jnp.maximum(v1, v2), jnp.minimum(v1, v2)] + [ninf] * 13
    top = _merge16_desc(_merge16_desc(_merge16_desc(row0, col0), mid), tail)
    z = jnp.zeros((SUBLANES, tm), F32)
    for i in range(16):
        z = z + jnp.exp(top[i] - top[0])
    st_ref[0] = top[15]
    st_ref[1] = a_top[0]
    st_ref[2] = b_top[0]
    st_ref[3] = z


def peer_score(q, k1, k2):
    n, d = q.shape
    nk = k1.shape[0]
    assert nk == 128 and d == PEER_HEADS * PEER_DK
    tm = _pick(n, (256, 128))
    return pl.pallas_call(
        _peer_score_body,
        grid=(n // tm,),
        in_specs=[pl.BlockSpec((tm, d), lambda i: (i, 0)),
                  pl.BlockSpec(k1.shape, lambda i: (0, 0)),
                  pl.BlockSpec(k2.shape, lambda i: (0, 0))],
        out_specs=[pl.BlockSpec((PEER_HEADS, nk, tm), lambda i: (0, 0, i)),
                   pl.BlockSpec((PEER_HEADS, nk, tm), lambda i: (0, 0, i)),
                   pl.BlockSpec((4, PEER_HEADS, tm), lambda i: (0, 0, i))],
        out_shape=[SDS((PEER_HEADS, nk, n), F32), SDS((PEER_HEADS, nk, n), F32),
                   SDS((4, PEER_HEADS, n), F32)],
        compiler_params=_cparams(("parallel",)),
        name="peer_score",
    )(q, k1, k2)


def _peer_dense_body(xn_ref, u_ref, vt_ref, s1_ref, s2_ref, st_ref, res_ref, o_ref,
                     acc_ref, e2_ref, h_ref, act_ref, *, nc, n_steps):
    c = pl.program_id(1)
    nk = s2_ref.shape[1]

    @pl.when(c == 0)
    def _():
        acc_ref[...] = jnp.zeros_like(acc_ref)
        for h in range(PEER_HEADS):
            e2_ref[h] = jnp.exp(s2_ref[h] - st_ref[2, h:h + 1, :])

    h_ref[...] = _dot_nt(u_ref[...], xn_ref[...])
    for s in range(nc):
        e1 = c * nc + s
        w = jnp.zeros((nk, xn_ref.shape[0]), F32)
        for h in range(PEER_HEADS):
            s1row = s1_ref[h, pl.ds(e1, 1), :]
            p1 = jnp.exp(s1row - st_ref[1, h:h + 1, :]) / st_ref[3, h:h + 1, :]
            a = s1row + s2_ref[h]
            w = w + jnp.where(a >= st_ref[0, h:h + 1, :], p1 * e2_ref[h], 0.0)
        act_ref[s * nk:(s + 1) * nk, :] = (_gelu(h_ref[s * nk:(s + 1) * nk, :]) * w).astype(BF16)
    acc_ref[...] += _dot(vt_ref[...], act_ref[...])

    @pl.when(c == n_steps - 1)
    def _():
        o_ref[...] = res_ref[...] + acc_ref[...].T


def peer_dense(xn_bf, u_bf, vt_bf, s1t, s2t, st, res):
    n, d = xn_bf.shape
    ne = u_bf.shape[0]
    nk = s1t.shape[1]
    tm = _pick(n, (768, 512, 256, 128))
    nc = 4
    te = nc * nk
    n_steps = ne // te
    body = functools.partial(_peer_dense_body, nc=nc, n_steps=n_steps)
    once = pl.Buffered(1)
    return pl.pallas_call(
        body,
        grid=(n // tm, n_steps),
        in_specs=[pl.BlockSpec((tm, d), lambda i, c: (i, 0), pipeline_mode=once),
                  pl.BlockSpec((te, d), lambda i, c: (c, 0)),
                  pl.BlockSpec((d, te), lambda i, c: (0, c)),
                  pl.BlockSpec((PEER_HEADS, nk, tm), lambda i, c: (0, 0, i), pipeline_mode=once),
                  pl.BlockSpec((PEER_HEADS, nk, tm), lambda i, c: (0, 0, i), pipeline_mode=once),
                  pl.BlockSpec((4, PEER_HEADS, tm), lambda i, c: (0, 0, i), pipeline_mode=once),
                  pl.BlockSpec((tm, d), lambda i, c: (i, 0), pipeline_mode=once)],
        out_specs=pl.BlockSpec((tm, d), lambda i, c: (i, 0)),
        out_shape=SDS((n, d), F32),
        scratch_shapes=[pltpu.VMEM((d, tm), F32), pltpu.VMEM((PEER_HEADS, nk, tm), F32),
                        pltpu.VMEM((te, tm), F32), pltpu.VMEM((te, tm), BF16)],
        compiler_params=_cparams(("parallel", "arbitrary")),
        name="peer_dense",
    )(xn_bf, u_bf, vt_bf, s1t, s2t, st, res)


def _flash_init(m_ref, l_ref, acc_ref):
    m_ref[...] = jnp.full(m_ref.shape, NEG, F32)
    l_ref[...] = jnp.zeros(l_ref.shape, F32)
    acc_ref[...] = jnp.zeros(acc_ref.shape, F32)


def _flash_update(s, mask, v_tiles, m_ref, l_ref, acc_ref):
    if mask is not None:
        s = jnp.where(mask, s, NEG)
    m_prev = m_ref[...]
    m_new = jnp.maximum(m_prev, jnp.max(s, axis=-1, keepdims=True))
    alpha = jnp.exp(m_prev - m_new)
    p = jnp.exp(s - m_new)
    if mask is not None:
        p = jnp.where(mask, p, 0.0)
    l_ref[...] = alpha * l_ref[...] + jnp.sum(p, axis=-1, keepdims=True)
    pb = p.astype(BF16)
    pv = None
    off = 0
    for v in v_tiles:
        part = _dot(pb[:, off:off + v.shape[0]], v)
        pv = part if pv is None else pv + part
        off += v.shape[0]
    acc_ref[...] = alpha * acc_ref[...] + pv
    m_ref[...] = m_new


def _flash_final(l_ref, acc_ref):
    return acc_ref[...] / jnp.maximum(l_ref[...], 1e-30)


def _softmax_masked(s, mask):
    s = jnp.where(mask, s, NEG)
    m = jnp.max(s, axis=-1, keepdims=True)
    e = jnp.where(mask, jnp.exp(s - m), 0.0)
    return e / jnp.maximum(jnp.sum(e, axis=-1, keepdims=True), 1e-30)


def _topk_mask(score, k):
    n = score.shape[-1]
    lane = lax.broadcasted_iota(I32, score.shape, 1).astype(F32)
    sel = jnp.zeros(score.shape, F32)
    for _ in range(k):
        m = jnp.max(score, axis=-1, keepdims=True)
        idx = jnp.min(jnp.where(score == m, lane, float(n)), axis=-1, keepdims=True)
        hit = lane == idx
        sel = jnp.where(hit, jnp.where(m >= 0.0, 1.0, 0.0), sel)
        score = jnp.where(hit, -jnp.inf, score)
    return sel


def _compress_body(x_ref, pe_ref, w1_ref, w2_ref, o_ref):
    xb = (x_ref[...] + pe_ref[...]).astype(BF16)
    h = _gelu(_dot(xb, w1_ref[...]))
    o_ref[...] = _dot(h.astype(BF16), w2_ref[...])


def compress(x, pe, w1_bf, w2_bf):
    rows, k = x.shape
    tr = _pick(rows, (256, 128, 64, 32, 16, 8))
    hid = w1_bf.shape[1]
    hd = w2_bf.shape[1]
    return pl.pallas_call(
        _compress_body,
        grid=(rows // tr,),
        in_specs=[pl.BlockSpec((tr, k), lambda i: (i, 0)),
                  pl.BlockSpec((1, k), lambda i: (0, 0)),
                  pl.BlockSpec((k, hid), lambda i: (0, 0)),
                  pl.BlockSpec((hid, hd), lambda i: (0, 0))],
        out_specs=pl.BlockSpec((tr, hd), lambda i: (i, 0)),
        out_shape=SDS((rows, hd), F32),
        compiler_params=_cparams(("parallel",)),
        name="nsa_compress",
    )(x, pe.reshape(1, k), w1_bf, w2_bf)


def _nsa_cmp_and_select(qs, slope, t1, kc_ref, vc_ref, *, r, nb, n_sel, ns_pad, k_top):
    tq = t1.shape[0]
    t = _rep_rows(t1, r)
    blk_end = lax.broadcasted_iota(I32, (1, nb), 1) * CMP_BLOCK + (CMP_BLOCK - 1)
    d_c = t - blk_end
    s = _dot_nt(qs, kc_ref[...].astype(BF16)) - slope * d_c.astype(F32)
    p_c = _softmax_masked(s, d_c >= 0)
    o_c = _dot(p_c.astype(BF16), vc_ref[...].astype(BF16))
    imp = p_c[0:tq]
    for i in range(1, r):
        imp = imp + p_c[i * tq:(i + 1) * tq]
    if ns_pad > nb:
        imp = jnp.concatenate([imp, jnp.zeros((tq, ns_pad - nb), F32)], axis=1)
    jsel = lax.broadcasted_iota(I32, (1, ns_pad), 1)
    cur = jnp.right_shift(t1, 6)
    forced = (jsel == 0) | (jsel == cur) | (jsel == cur - 1)
    score = jnp.where(forced, r + 1.0, jnp.where(jsel <= cur, imp, -1.0))
    score = jnp.where(jsel < n_sel, score, -2.0)
    return o_c, _topk_mask(score, k_top)


def _block_mask(sel_bf, first_pos, tk):
    ns_pad = sel_bf.shape[1]
    blk = lax.broadcasted_iota(I32, (ns_pad, tk), 0)
    key_blk = jnp.right_shift(first_pos + lax.broadcasted_iota(I32, (ns_pad, tk), 1), 6)
    expand = jnp.where(blk == key_blk, 1.0, 0.0).astype(BF16)
    return _dot(sel_bf, expand)


def _gate_cols(gsig, c, r):
    return jnp.concatenate([gsig[:, c * r + i:c * r + i + 1] for i in range(r)], axis=0)


def _nsa_p_body(q_ref, gt_ref, kc_ref, vc_ref, ks_ref, vs_ref, kw_ref, vw_ref, sl_ref, o_ref,
                m_ref, l_ref, acc_ref, *, tq, tk, wk, nb, n_sel, k_top, scale, r):
    qi = pl.program_id(2)
    a = qi * tq
    qs = (_stack_heads(q_ref[...], r, HEAD_DIM) * scale).astype(BF16)
    slope = sl_ref[...]
    t1 = a + lax.broadcasted_iota(I32, (tq, 1), 0)
    t = _rep_rows(t1, r)
    o_c, sel = _nsa_cmp_and_select(qs, slope, t1, kc_ref, vc_ref, r=r, nb=nb, n_sel=n_sel,
                                   ns_pad=n_sel, k_top=k_top)
    sel_bf = sel.astype(BF16)
    n_it = (a + tq - 1) // tk + 1

    def sel_step(j, carry):
        start = pl.multiple_of(j * tk, tk)
        k = ks_ref[pl.ds(start, tk), :].astype(BF16)
        v = vs_ref[pl.ds(start, tk), :].astype(BF16)
        d = t - (start + lax.broadcasted_iota(I32, (1, tk), 1))
        s = _dot_nt(qs, k) - slope * d.astype(F32)
        bm = _rep_rows(_block_mask(sel_bf, start, tk), r)
        _flash_update(s, (bm > 0.5) & (d >= 0), [v], m_ref, l_ref, acc_ref)
        return carry

    _flash_init(m_ref, l_ref, acc_ref)
    lax.fori_loop(0, n_it, sel_step, 0)
    o_s = _flash_final(l_ref, acc_ref)

    start = pl.multiple_of(jnp.maximum(a + tq - wk, 0), SUBLANES)
    k = kw_ref[pl.ds(start, wk), :].astype(BF16)
    v = vw_ref[pl.ds(start, wk), :].astype(BF16)
    d = t - (start + lax.broadcasted_iota(I32, (1, wk), 1))
    s = _dot_nt(qs, k) - slope * d.astype(F32)
    _flash_init(m_ref, l_ref, acc_ref)
    _flash_update(s, (d >= 0) & (d <= WINDOW), [v], m_ref, l_ref, acc_ref)
    o_w = _flash_final(l_ref, acc_ref)

    gsig = 1.0 / (1.0 + jnp.exp(-gt_ref[...]))
    o = _gate_cols(gsig, 0, r) * o_c + _gate_cols(gsig, 1, r) * o_s + _gate_cols(gsig, 2, r) * o_w
    o_ref[...] = _unstack_heads(o, r, tq).astype(BF16)


def nsa_prompt(hab, kc, vc, slopes, cols, *, b, t, g, r):
    hd = HEAD_DIM
    nb = kc.shape[2]
    n_sel = -(-t // SEL_BLOCK)
    assert n_sel == nb and t % SEL_BLOCK == 0
    tq = Q_BLOCK if t % Q_BLOCK == 0 else t
    tk = _pick(t, (512, 256, 128, t))
    wk = min(WINDOW + tq, t)
    nq = t // tq
    qw = r * hd
    body = functools.partial(_nsa_p_body, tq=tq, tk=tk, wk=wk, nb=nb, n_sel=n_sel, k_top=min(SEL_TOPK, n_sel),
                             scale=hd ** -0.5, r=r)

    def seq_spec(col0):
        return pl.BlockSpec((t, hd), lambda bb, gg, qi: (bb, col0 // hd + gg))

    return pl.pallas_call(
        body,
        grid=(b, g, nq),
        in_specs=[pl.BlockSpec((tq, qw), lambda bb, gg, qi: (bb * nq + qi, gg)),
                  pl.BlockSpec((tq, LANES), lambda bb, gg, qi: (bb * nq + qi, cols["gt"] // LANES + gg)),
                  pl.BlockSpec((None, None, nb, hd), lambda bb, gg, qi: (bb, gg, 0, 0)),
                  pl.BlockSpec((None, None, nb, hd), lambda bb, gg, qi: (bb, gg, 0, 0)),
                  seq_spec(cols["ks"]), seq_spec(cols["vs"]), seq_spec(cols["kw"]), seq_spec(cols["vw"]),
                  pl.BlockSpec((None, r * tq, 1), lambda bb, gg, qi: (gg, 0, 0))],
        out_specs=pl.BlockSpec((tq, qw), lambda bb, gg, qi: (bb * nq + qi, gg)),
        out_shape=SDS((b * t, g * qw), BF16),
        scratch_shapes=[pltpu.VMEM((r * tq, 1), F32), pltpu.VMEM((r * tq, 1), F32),
                        pltpu.VMEM((r * tq, hd), F32)],
        compiler_params=_cparams(("parallel", "parallel", "arbitrary")),
        name="nsa_prompt",
    )(hab, hab, kc, vc, hab, hab, hab, hab, jnp.repeat(slopes, tq, axis=1).reshape(g, r * tq, 1))


def _pad_rows(x, rows):
    return jnp.concatenate([x, jnp.zeros((rows - x.shape[0], x.shape[1]), x.dtype)], axis=0)


def _block_expand(ns_pad, first_pos, width):
    blk = lax.broadcasted_iota(I32, (ns_pad, width), 0)
    key_blk = jnp.right_shift(first_pos + lax.broadcasted_iota(I32, (ns_pad, width), 1), 6)
    return jnp.where(blk == key_blk, 1.0, 0.0).astype(BF16)


def _nsa_s_body(pt_ref, q_ref, gt_ref, ksn_ref, vsn_ref, kwn_ref, vwn_ref, kc_ref, vc_ref, win_ref,
                sl_ref, *rest, pp, n_chunks, ts, page, offset, wb, nb, n_sel, ns_pad, k_top, scale, g, r):
    pages = rest[:pp]
    o_ref = rest[pp]
    qs_ref, oc_ref, sel_ref, m_ref, l_ref, acc_ref = rest[pp + 1:]
    del pt_ref
    c = pl.program_id(1)
    hd = HEAD_DIM
    t1 = offset + lax.broadcasted_iota(I32, (ts, 1), 0)
    t = _rep_rows(t1, r)
    lane = lax.broadcasted_iota(I32, (1, page), 1)

    @pl.when(c == 0)
    def _():
        for gg in range(g):
            qs = (_stack_heads(q_ref[:, gg * r * hd:(gg + 1) * r * hd], r, hd) * scale).astype(BF16)
            qs_ref[gg] = qs
            o_c, sel = _nsa_cmp_and_select(qs, sl_ref[gg], t1, kc_ref.at[gg], vc_ref.at[gg], r=r, nb=nb,
                                           n_sel=n_sel, ns_pad=ns_pad, k_top=k_top)
            oc_ref[gg] = o_c
            sel_ref[gg] = _rep_rows(sel, r).astype(BF16)
            _flash_init(m_ref.at[gg], l_ref.at[gg], acc_ref.at[gg])

    width = pp * page
    first = c * width
    d = t - (first + lax.broadcasted_iota(I32, (1, width), 1))
    expand = _block_expand(ns_pad, first, width)
    for gg in range(g):
        ks = [pages[i][0, pl.ds(2 * g + gg, page, stride=4 * g), :].astype(BF16) for i in range(pp)]
        vs = [pages[i][0, pl.ds(3 * g + gg, page, stride=4 * g), :].astype(BF16) for i in range(pp)]
        s = jnp.concatenate([_dot_nt(qs_ref[gg], k) for k in ks], axis=1) - sl_ref[gg] * d.astype(F32)
        bm = _dot(sel_ref[gg], expand)
        _flash_update(s, (bm > 0.5) & (d >= 0), vs, m_ref.at[gg], l_ref.at[gg], acc_ref.at[gg])

    @pl.when(c == n_chunks - 1)
    def _():
        gsig = 1.0 / (1.0 + jnp.exp(-gt_ref[...]))
        dn = t - (offset + lane)
        expand_n = _block_expand(ns_pad, offset, page)
        outs = []
        for gg in range(g):
            k = _pad_rows(ksn_ref[:, gg * hd:(gg + 1) * hd], page).astype(BF16)
            v = _pad_rows(vsn_ref[:, gg * hd:(gg + 1) * hd], page).astype(BF16)
            s = _dot_nt(qs_ref[gg], k) - sl_ref[gg] * dn.astype(F32)
            bm = _dot(sel_ref[gg], expand_n)
            _flash_update(s, (bm > 0.5) & (dn >= 0) & (lane < ts), [v], m_ref.at[gg], l_ref.at[gg],
                          acc_ref.at[gg])
            o_s = _flash_final(l_ref.at[gg], acc_ref.at[gg])
            _flash_init(m_ref.at[gg], l_ref.at[gg], acc_ref.at[gg])
            kw = win_ref[pl.ds(gg, wb, stride=2 * g), :].astype(BF16)
            vw = win_ref[pl.ds(g + gg, wb, stride=2 * g), :].astype(BF16)
            kn = _pad_rows(kwn_ref[:, gg * hd:(gg + 1) * hd], page).astype(BF16)
            vn = _pad_rows(vwn_ref[:, gg * hd:(gg + 1) * hd], page).astype(BF16)
            dw = t - (offset - wb + lax.broadcasted_iota(I32, (1, wb), 1))
            s = jnp.concatenate([_dot_nt(qs_ref[gg], kw) - sl_ref[gg] * dw.astype(F32),
                                 _dot_nt(qs_ref[gg], kn) - sl_ref[gg] * dn.astype(F32)], axis=1)
            mask = jnp.concatenate([(dw >= 0) & (dw <= WINDOW), (dn >= 0) & (dn <= WINDOW) & (lane < ts)], axis=1)
            _flash_update(s, mask, [vw, vn], m_ref.at[gg], l_ref.at[gg], acc_ref.at[gg])
            o_w = _flash_final(l_ref.at[gg], acc_ref.at[gg])
            gs = gsig[:, gg * LANES:(gg + 1) * LANES]
            o = _gate_cols(gs, 0, r) * oc_ref[gg] + _gate_cols(gs, 1, r) * o_s + _gate_cols(gs, 2, r) * o_w
            outs.append(_unstack_heads(o, r, ts))
        o_ref[...] = jnp.concatenate(outs, axis=1)


def nsa_sample(hab, kc, vc, pool, win, page_table, slopes, cols, *, n_p, bs, ts, page, g, r):
    hd = HEAD_DIM
    n_pages = page_table.shape[1]
    offset = n_pages * page
    wb = win.shape[1] // (2 * g)
    assert offset % SEL_BLOCK == 0 and offset - wb >= 0 and wb == WINDOW
    nb = kc.shape[2]
    n_sel = -(-(offset + ts) // SEL_BLOCK)
    ns_pad = -(-n_sel // LANES) * LANES
    pp = _pick(n_pages, (8, 4, 2, 1))
    n_chunks = n_pages // pp
    rb = n_p // ts
    body = functools.partial(_nsa_s_body, pp=pp, n_chunks=n_chunks, ts=ts, page=page, offset=offset, wb=wb,
                             nb=nb, n_sel=n_sel, ns_pad=ns_pad, k_top=min(SEL_TOPK, n_sel),
                             scale=hd ** -0.5, g=g, r=r)

    def row_spec(width, col0):
        return pl.BlockSpec((ts, width), lambda b, c, pt: (rb + b, col0 // width))

    def page_spec(i):
        return pl.BlockSpec((1, page * 4 * g, hd), lambda b, c, pt: (pt[b, c * pp + i], 0, 0))

    in_specs = [row_spec(g * r * hd, cols["q"]), row_spec(g * LANES, cols["gt"]),
                row_spec(g * hd, cols["ks"]), row_spec(g * hd, cols["vs"]),
                row_spec(g * hd, cols["kw"]), row_spec(g * hd, cols["vw"]),
                pl.BlockSpec((None, g, nb, hd), lambda b, c, pt: (b, 0, 0, 0)),
                pl.BlockSpec((None, g, nb, hd), lambda b, c, pt: (b, 0, 0, 0)),
                pl.BlockSpec((None, wb * 2 * g, hd), lambda b, c, pt: (b, 0, 0)),
                pl.BlockSpec((g, r * ts, 1), lambda b, c, pt: (0, 0, 0))]
    in_specs += [page_spec(i) for i in range(pp)]
    grid_spec = pltpu.PrefetchScalarGridSpec(
        num_scalar_prefetch=1, grid=(bs, n_chunks), in_specs=in_specs,
        out_specs=pl.BlockSpec((ts, g * r * hd), lambda b, c, pt: (b, 0)),
        scratch_shapes=[pltpu.VMEM((g, r * ts, hd), BF16), pltpu.VMEM((g, r * ts, hd), F32),
                        pltpu.VMEM((g, r * ts, ns_pad), BF16), pltpu.VMEM((g, r * ts, 1), F32),
                        pltpu.VMEM((g, r * ts, 1), F32), pltpu.VMEM((g, r * ts, hd), F32)])
    return pl.pallas_call(
        body, grid_spec=grid_spec, out_shape=SDS((bs * ts, g * r * hd), F32),
        compiler_params=_cparams(("parallel", "arbitrary")), name="nsa_sample",
    )(page_table, hab, hab, hab, hab, hab, hab, kc, vc, win,
      jnp.repeat(slopes, ts, axis=1).reshape(g, r * ts, 1), *([pool] * pp))


def _cmp_s_body(pt_ref, pek_ref, w1k_ref, w2k_ref, pev_ref, w1v_ref, w2v_ref, *rest, pp, page, g):
    pages = rest[:pp]
    kc_ref, vc_ref, stage_ref = rest[pp:]
    del pt_ref
    hid = w1k_ref.shape[2]
    rows = pp * (page // CMP_BLOCK)
    for cg in range(2 * g):
        for i in range(pp):
            stage_ref[cg, i * page:(i + 1) * page, :] = pages[i][0, pl.ds(cg, page, stride=4 * g), :]
    for comp, (pe_ref, w1_ref, w2_ref, out_ref) in enumerate(
            ((pek_ref, w1k_ref, w2k_ref, kc_ref), (pev_ref, w1v_ref, w2v_ref, vc_ref))):
        def step(l, acc, comp=comp, pe_ref=pe_ref, w1_ref=w1_ref):
            x = jnp.concatenate([stage_ref[comp * g + gg, pl.ds(l, rows, stride=CMP_BLOCK), :]
                                 for gg in range(g)], axis=0)
            xb = (x + pe_ref[pl.ds(l, 1), :]).astype(BF16)
            return acc + _dot(xb, w1_ref[l])

        acc = lax.fori_loop(0, CMP_BLOCK, step, jnp.zeros((g * rows, hid), F32))
        out = _dot(_gelu(acc).astype(BF16), w2_ref[...])
        for gg in range(g):
            out_ref[gg] = out[gg * rows:(gg + 1) * rows]


def compress_sample(pool, page_table, p, *, page, g):
    hd = HEAD_DIM
    bs, n_pages = page_table.shape
    pp = _pick(n_pages, (16, 8, 4))
    rows = pp * (page // CMP_BLOCK)
    nb = n_pages * (page // CMP_BLOCK)
    hid = p["w1_k"].shape[1]
    body = functools.partial(_cmp_s_body, pp=pp, page=page, g=g)
    once = pl.Buffered(1)

    def wspecs():
        return [pl.BlockSpec((CMP_BLOCK, hd), lambda b, c, pt: (0, 0)),
                pl.BlockSpec((CMP_BLOCK, hd, hid), lambda b, c, pt: (0, 0, 0), pipeline_mode=once),
                pl.BlockSpec((hid, hd), lambda b, c, pt: (0, 0))]

    def page_spec(i):
        return pl.BlockSpec((1, page * 4 * g, hd), lambda b, c, pt: (pt[b, c * pp + i], 0, 0))

    grid_spec = pltpu.PrefetchScalarGridSpec(
        num_scalar_prefetch=1, grid=(bs, n_pages // pp),
        in_specs=wspecs() + wspecs() + [page_spec(i) for i in range(pp)],
        out_specs=[pl.BlockSpec((None, g, rows, hd), lambda b, c, pt: (b, 0, c, 0)),
                   pl.BlockSpec((None, g, rows, hd), lambda b, c, pt: (b, 0, c, 0))],
        scratch_shapes=[pltpu.VMEM((2 * g, pp * page, hd), F32)])
    return pl.pallas_call(
        body, grid_spec=grid_spec, out_shape=[SDS((bs, g, nb, hd), F32), SDS((bs, g, nb, hd), F32)],
        compiler_params=_cparams(("parallel", "arbitrary")), name="nsa_compress_sample",
    )(page_table, p["pe_k"], p["w1_k"].astype(BF16).reshape(CMP_BLOCK, hd, hid), p["w2_k"].astype(BF16),
      p["pe_v"], p["w1_v"].astype(BF16).reshape(CMP_BLOCK, hd, hid), p["w2_v"].astype(BF16), *([pool] * pp))


def _ret_body(q_ref, k_ref, v_ref, g_ref, gn_ref, s0_ref, dm_ref, cr_ref, kd_ref, cd_ref, o_ref, s_ref,
              st_ref, *, scale, n_chunks):
    c = pl.program_id(2)

    @pl.when(c == 0)
    def _():
        st_ref[...] = s0_ref[...]

    qb = q_ref[...].astype(BF16)
    ks = k_ref[...] * scale
    kb = ks.astype(BF16)
    vb = v_ref[...].astype(BF16)
    st = st_ref[...]
    att = _dot_nt(qb, kb) * dm_ref[...]
    o = _dot(att.astype(BF16), vb) + _dot(qb, st.astype(BF16)) * cr_ref[...]
    st_new = st * cd_ref[...] + _dot_tn((ks * kd_ref[...]).astype(BF16), vb)
    st_ref[...] = st_new
    mu = jnp.mean(o, axis=-1, keepdims=True)
    var = jnp.mean(jnp.square(o - mu), axis=-1, keepdims=True)
    on = (o - mu) * lax.rsqrt(var + GN_EPS) * gn_ref[...]
    gate = g_ref[...]
    o_ref[...] = ((gate * (1.0 / (1.0 + jnp.exp(-gate)))) * on).astype(o_ref.dtype)

    @pl.when(c == n_chunks - 1)
    def _():
        s_ref[...] = st_new


def retention(hab, gn_w, s0, cols, *, row0, b, t, nh):
    dk = HEAD_DIM
    ch = RET_CHUNK if t % RET_CHUNK == 0 else t
    n_chunks = t // ch
    lg = jnp.log1p(-jnp.exp2(-5.0 - jnp.arange(nh, dtype=F32)))
    i = jnp.arange(ch, dtype=F32)
    diff = i[:, None] - i[None, :]
    dmask = jnp.where(diff >= 0, jnp.exp(jnp.maximum(diff, 0.0)[None] * lg[:, None, None]), 0.0)
    cross = jnp.exp((i + 1.0)[None, :] * lg[:, None]).reshape(nh, ch, 1)
    kdec = jnp.exp((ch - 1.0 - i)[None, :] * lg[:, None]).reshape(nh, ch, 1)
    cdec = jnp.exp(ch * lg).reshape(nh, 1, 1)
    rb = row0 // ch
    body = functools.partial(_ret_body, scale=dk ** -0.5, n_chunks=n_chunks)

    def col_spec(col0):
        return pl.BlockSpec((ch, dk), lambda bb, h, c: (rb + bb * n_chunks + c, col0 // dk + h))

    return pl.pallas_call(
        body,
        grid=(b, nh, n_chunks),
        in_specs=[col_spec(cols["rq"]), col_spec(cols["rk"]), col_spec(cols["rv"]), col_spec(cols["rg"]),
                  pl.BlockSpec((1, dk), lambda bb, h, c: (0, h)),
                  pl.BlockSpec((None, None, dk, dk), lambda bb, h, c: (bb, h, 0, 0)),
                  pl.BlockSpec((None, ch, ch), lambda bb, h, c: (h, 0, 0)),
                  pl.BlockSpec((None, ch, 1), lambda bb, h, c: (h, 0, 0)),
                  pl.BlockSpec((None, ch, 1), lambda bb, h, c: (h, 0, 0)),
                  pl.BlockSpec((None, 1, 1), lambda bb, h, c: (h, 0, 0))],
        out_specs=[pl.BlockSpec((ch, dk), lambda bb, h, c: (bb * n_chunks + c, h)),
                   pl.BlockSpec((None, None, dk, dk), lambda bb, h, c: (bb, h, 0, 0))],
        out_shape=[SDS((b * t, nh * dk), BF16 if ch % 16 == 0 else F32), SDS((b, nh, dk, dk), F32)],
        scratch_shapes=[pltpu.VMEM((dk, dk), F32)],
        compiler_params=_cparams(("parallel", "parallel", "arbitrary")),
        name="retention",
    )(hab, hab, hab, hab, gn_w.reshape(1, nh * dk), s0, dmask, cross, kdec, cdec)


def _logf_body(x_ref, b_ref, o_ref):
    x = x_ref[...] + b_ref[...]
    o_ref[...] = -(jnp.maximum(-x, 0.0) + jnp.log1p(jnp.exp(-jnp.abs(x))))


def fox_logf(hc, b_pad, col0):
    n = hc.shape[0]
    tm = _pick(n, (768, 512, 256, 128, 64, 32, 16, 8))
    return pl.pallas_call(
        _logf_body,
        grid=(n // tm,),
        in_specs=[pl.BlockSpec((tm, LANES), lambda i: (i, col0 // LANES)),
                  pl.BlockSpec((1, LANES), lambda i: (0, 0))],
        out_specs=pl.BlockSpec((tm, LANES), lambda i: (i, 0)),
        out_shape=SDS((n, LANES), F32),
        compiler_params=_cparams(("parallel",)),
        name="fox_logf",
    )(hc, b_pad)


def _cumsum_rows_body(x_ref, o_ref, carry_ref):
    @pl.when(pl.program_id(1) == 0)
    def _():
        carry_ref[...] = jnp.zeros_like(carry_ref)

    tc = x_ref.shape[0]
    tri = jnp.where(lax.broadcasted_iota(I32, (tc, tc), 1) <= lax.broadcasted_iota(I32, (tc, tc), 0),
                    1.0, 0.0).astype(BF16)
    hi, mid, lo = _split3(x_ref[...])
    f = (_dot(tri, hi) + _dot(tri, mid)) + _dot(tri, lo) + carry_ref[...]
    o_ref[...] = f
    carry_ref[...] = f[tc - 1:tc, :]


def cumsum_rows(x, *, b, t):
    tc = 128 if t % 128 == 0 else t
    nc = t // tc
    return pl.pallas_call(
        _cumsum_rows_body,
        grid=(b, nc),
        in_specs=[pl.BlockSpec((tc, LANES), lambda bb, c: (bb * nc + c, 0))],
        out_specs=pl.BlockSpec((tc, LANES), lambda bb, c: (bb * nc + c, 0)),
        out_shape=SDS((b * t, LANES), F32),
        scratch_shapes=[pltpu.VMEM((1, LANES), F32)],
        compiler_params=_cparams(("parallel", "arbitrary")),
        name="fox_cumsum",
    )(x)


def _fox_p_body(q_ref, k_ref, v_ref, fq_ref, fk_ref, o_ref, m_ref, l_ref, acc_ref, *, tq, tk, scale, r):
    qi = pl.program_id(2)
    qs = (_stack_heads(q_ref[...], r, HEAD_DIM) * scale).astype(BF16)
    fq = jnp.concatenate([fq_ref[:, i:i + 1] for i in range(r)], axis=0)
    t = _rep_rows(qi * tq + lax.broadcasted_iota(I32, (tq, 1), 0), r)

    def tile(j, masked):
        start = pl.multiple_of(j * tk, tk)
        k = k_ref[pl.ds(start, tk), :].astype(BF16)
        v = v_ref[pl.ds(start, tk), :].astype(BF16)
        fkj = fk_ref[j]
        fk = jnp.concatenate([jnp.broadcast_to(fkj[i:i + 1, :], (tq, tk)) for i in range(r)], axis=0)
        s = _dot_nt(qs, k) + fq - fk
        mask = (start + lax.broadcasted_iota(I32, (1, tk), 1) <= t) if masked else None
        _flash_update(s, mask, [v], m_ref, l_ref, acc_ref)

    def full_tile(j, carry):
        tile(j, False)
        return carry

    _flash_init(m_ref, l_ref, acc_ref)
    n_full = (qi * tq) // tk
    lax.fori_loop(0, n_full, full_tile, 0)
    for jj in range(tq // tk):
        tile(n_full + jj, True)
    o_ref[...] = _unstack_heads(_flash_final(l_ref, acc_ref), r, tq).astype(BF16)


def fox_prompt(hc, f, cols, *, b, t, g, r):
    hd = HEAD_DIM
    tq = _pick(t, (512, 256, 128, t))
    tk = _pick(tq, (256, 128, tq))
    nq = t // tq
    nk = t // tk
    qw = r * hd
    fq = f.transpose(0, 2, 1, 3)
    fk = f.reshape(b, nk, tk, g, r).transpose(0, 3, 1, 4, 2)
    body = functools.partial(_fox_p_body, tq=tq, tk=tk, scale=hd ** -0.5, r=r)
    return pl.pallas_call(
        body,
        grid=(b, g, nq),
        in_specs=[pl.BlockSpec((tq, qw), lambda bb, gg, qi: (bb * nq + qi, gg)),
                  pl.BlockSpec((t, hd), lambda bb, gg, qi: (bb, cols["k"] // hd + gg)),
                  pl.BlockSpec((t, hd), lambda bb, gg, qi: (bb, cols["v"] // hd + gg)),
                  pl.BlockSpec((None, None, tq, r), lambda bb, gg, qi: (bb, gg, qi, 0)),
                  pl.BlockSpec((None, None, nk, r, tk), lambda bb, gg, qi: (bb, gg, 0, 0, 0))],
        out_specs=pl.BlockSpec((tq, qw), lambda bb, gg, qi: (bb * nq + qi, gg)),
        out_shape=SDS((b * t, g * qw), BF16),
        scratch_shapes=[pltpu.VMEM((r * tq, 1), F32), pltpu.VMEM((r * tq, 1), F32),
                        pltpu.VMEM((r * tq, hd), F32)],
        compiler_params=_cparams(("parallel", "parallel", "arbitrary")),
        name="fox_prompt",
    )(hc, hc, hc, fq, fk)


def _fox_f_body(pt_ref, new_ref, *rest, pp, n_chunks):
    pages = rest[:pp]
    fk_ref, fn_ref, carry_ref = rest[pp:]
    del pt_ref
    c = pl.program_id(1)
    page = pages[0].shape[2]

    @pl.when(c == 0)
    def _():
        carry_ref[...] = jnp.zeros_like(carry_ref)

    ut = jnp.where(lax.broadcasted_iota(I32, (page, page), 0) <= lax.broadcasted_iota(I32, (page, page), 1),
                   1.0, 0.0).astype(BF16)

    def csum(x, carry):
        hi, mid, lo = _split3(x)
        return (_dot(hi, ut) + _dot(mid, ut)) + _dot(lo, ut) + carry

    carry = carry_ref[...]
    for i in range(pp):
        f = csum(pages[i][0], carry)
        fk_ref[:, i * page:(i + 1) * page] = f
        carry = f[:, page - 1:page]
    carry_ref[...] = carry

    @pl.when(c == n_chunks - 1)
    def _():
        fn_ref[...] = csum(new_ref[...], carry)


def fox_f_sample(logf_pool_t, new_t, page_table):
    bs, n_pages = page_table.shape
    _, nh, page = logf_pool_t.shape
    pp = _pick(n_pages, (16, 8, 4, 2, 1))
    n_chunks = n_pages // pp
    body = functools.partial(_fox_f_body, pp=pp, n_chunks=n_chunks)
    in_specs = [pl.BlockSpec((None, nh, page), lambda b, c, pt: (b, 0, 0))]
    in_specs += [pl.BlockSpec((1, nh, page), functools.partial(lambda b, c, pt, i: (pt[b, c * pp + i], 0, 0), i=i))
                 for i in range(pp)]
    grid_spec = pltpu.PrefetchScalarGridSpec(
        num_scalar_prefetch=1, grid=(bs, n_chunks), in_specs=in_specs,
        out_specs=[pl.BlockSpec((None, nh, pp * page), lambda b, c, pt: (b, 0, c)),
                   pl.BlockSpec((None, nh, page), lambda b, c, pt: (b, 0, 0))],
        scratch_shapes=[pltpu.VMEM((nh, 1), F32)])
    return pl.pallas_call(
        body, grid_spec=grid_spec,
        out_shape=[SDS((bs, nh, n_pages * page), F32), SDS((bs, nh, page), F32)],
        compiler_params=_cparams(("parallel", "arbitrary")), name="fox_f_sample",
    )(page_table, new_t, *([logf_pool_t] * pp))


def _fox_s_body(pt_ref, q_ref, kn_ref, vn_ref, fq_ref, fk_ref, fn_ref, *rest, pp, n_chunks, ts, page, offset,
                scale, g, r):
    pages = rest[:pp]
    o_ref = rest[pp]
    qs_ref, m_ref, l_ref, acc_ref = rest[pp + 1:]
    del pt_ref
    c = pl.program_id(1)
    hd = HEAD_DIM

    @pl.when(c == 0)
    def _():
        for gg in range(g):
            qs_ref[gg] = (_stack_heads(q_ref[:, gg * r * hd:(gg + 1) * r * hd], r, hd) * scale).astype(BF16)
            _flash_init(m_ref.at[gg], l_ref.at[gg], acc_ref.at[gg])

    def fk_rows(f, gg):
        return jnp.concatenate([jnp.broadcast_to(f[gg * r + i:gg * r + i + 1, :], (ts, f.shape[1]))
                                for i in range(r)], axis=0)

    fk = fk_ref[...]
    for gg in range(g):
        ks = [pages[i][0, pl.ds(gg, page, stride=2 * g), :].astype(BF16) for i in range(pp)]
        vs = [pages[i][0, pl.ds(g + gg, page, stride=2 * g), :].astype(BF16) for i in range(pp)]
        s = jnp.concatenate([_dot_nt(qs_ref[gg], k) for k in ks], axis=1) + fq_ref[gg] - fk_rows(fk, gg)
        _flash_update(s, None, vs, m_ref.at[gg], l_ref.at[gg], acc_ref.at[gg])

    @pl.when(c == n_chunks - 1)
    def _():
        t = _rep_rows(offset + lax.broadcasted_iota(I32, (ts, 1), 0), r)
        lane = lax.broadcasted_iota(I32, (1, page), 1)
        outs = []
        for gg in range(g):
            k = _pad_rows(kn_ref[:, gg * hd:(gg + 1) * hd], page).astype(BF16)
            v = _pad_rows(vn_ref[:, gg * hd:(gg + 1) * hd], page).astype(BF16)
            s = _dot_nt(qs_ref[gg], k) + fq_ref[gg] - fk_rows(fn_ref[...], gg)
            _flash_update(s, (offset + lane <= t) & (lane < ts), [v], m_ref.at[gg], l_ref.at[gg], acc_ref.at[gg])
            outs.append(_unstack_heads(_flash_final(l_ref.at[gg], acc_ref.at[gg]), r, ts))
        o_ref[...] = jnp.concatenate(outs, axis=1)


def fox_sample(hc, fq, fk, fn, pool, page_table, cols, *, n_p, bs, ts, page, g, r):
    hd = HEAD_DIM
    n_pages = page_table.shape[1]
    nh = g * r
    offset = n_pages * page
    pp = _pick(n_pages, (8, 4, 2, 1))
    n_chunks = n_pages // pp
    rb = n_p // ts
    body = functools.partial(_fox_s_body, pp=pp, n_chunks=n_chunks, ts=ts, page=page, offset=offset,
                             scale=hd ** -0.5, g=g, r=r)

    def row_spec(width, col0):
        return pl.BlockSpec((ts, width), lambda b, c, pt: (rb + b, col0 // width))

    def page_spec(i):
        return pl.BlockSpec((1, page * 2 * g, hd), lambda b, c, pt: (pt[b, c * pp + i], 0, 0))

    in_specs = [row_spec(nh * hd, cols["q"]), row_spec(g * hd, cols["k"]), row_spec(g * hd, cols["v"]),
                pl.BlockSpec((None, g, r * ts, 1), lambda b, c, pt: (b, 0, 0, 0)),
                pl.BlockSpec((None, nh, pp * page), lambda b, c, pt: (b, 0, c)),
                pl.BlockSpec((None, nh, page), lambda b, c, pt: (b, 0, 0))]
    in_specs += [page_spec(i) for i in range(pp)]
    grid_spec = pltpu.PrefetchScalarGridSpec(
        num_scalar_prefetch=1, grid=(bs, n_chunks), in_specs=in_specs,
        out_specs=pl.BlockSpec((ts, nh * hd), lambda b, c, pt: (b, 0)),
        scratch_shapes=[pltpu.VMEM((g, r * ts, hd), BF16), pltpu.VMEM((g, r * ts, 1), F32),
                        pltpu.VMEM((g, r * ts, 1), F32), pltpu.VMEM((g, r * ts, hd), F32)])
    return pl.pallas_call(
        body, grid_spec=grid_spec, out_shape=SDS((bs * ts, nh * hd), F32),
        compiler_params=_cparams(("parallel", "arbitrary")), name="fox_sample",
    )(page_table, hc, hc, hc, fq, fk, fn, *([pool] * pp))


def _ab_layout(nh_a, g, r, nh_r):
    hd = HEAD_DIM
    qa, kv = nh_a * hd, g * hd
    sizes = [("q", qa), ("kc", kv), ("vc", kv), ("ks", kv), ("vs", kv), ("kw", kv), ("vw", kv),
             ("gt_src", nh_a * 3), ("rq", nh_r * hd), ("rk", nh_r * hd), ("rv", nh_r * hd), ("rg", nh_r * hd)]
    src = {}
    pos = 0
    for name, w in sizes:
        src[name] = pos
        pos += w
    order = ["q", "kc", "vc", "ks", "vs", "kw", "vw", "rq", "rk", "rv", "rg"]
    widths = dict(sizes)
    idx = []
    cols = {}
    for name in order:
        cols[name] = len(idx)
        idx += list(range(src[name], src[name] + widths[name]))
    cols["gt"] = len(idx)
    for gg in range(g):
        blk = [-1] * LANES
        for c in range(3):
            for i in range(r):
                blk[c * r + i] = src["gt_src"] + (gg * r + i) * 3 + c
        idx += blk
    return np.array(idx, np.int32), cols


def _even_layer(x, n_p, b, t, bs, ts, p, cache_kv, cache_win, state, page_table):
    hd = HEAD_DIM
    d = x.shape[1]
    nh_a = d // (2 * hd)
    g = nh_a // 4
    r = nh_a // g
    nh_r = d // (2 * hd)
    idx, cols = _ab_layout(nh_a, g, r, nh_r)
    tn = 768
    ncol = -(-len(idx) // tn) * tn
    idx = np.concatenate([idx, np.full(ncol - len(idx), -1, np.int32)])
    w_in = jnp.where(idx[None, :] >= 0, jnp.take(p["w_in"], np.maximum(idx, 0), axis=1), 0.0).astype(BF16)
    hab, _ = mm_norm(x, p["norm"], w_in, tn)

    kv4 = 4 * g * hd
    c_rows = cols["kc"]
    c_win = cols["kw"]
    new_rows_p = hab[:n_p, c_rows:c_rows + kv4].reshape(b, t, 4, g, hd)
    new_rows_s = hab[n_p:, c_rows:c_rows + kv4].reshape(bs, ts, 4, g, hd)
    new_win_p = hab[:n_p, c_win:c_win + 2 * g * hd].reshape(b, t, 2, g, hd)
    new_win_s = hab[n_p:, c_win:c_win + 2 * g * hd].reshape(bs, ts, 2, g, hd)
    win_state_p = new_win_p[:, -min(WINDOW, t):]
    wb = cache_win.shape[1]
    win_state_s = jnp.concatenate([cache_win, new_win_s], axis=1)[:, -wb:]

    slopes = jnp.exp2(-8.0 * (jnp.arange(nh_a, dtype=F32) + 1.0) / nh_a).reshape(g, r)
    w1k, w2k = p["w1_k"].astype(BF16), p["w2_k"].astype(BF16)
    w1v, w2v = p["w1_v"].astype(BF16), p["w2_v"].astype(BF16)

    def cmp_pair(blocks):
        bb, nb = blocks.shape[:2]
        flat = blocks.transpose(3, 0, 1, 4, 2, 5).reshape(2, bb * nb * g, CMP_BLOCK * hd)
        kc = compress(flat[0], p["pe_k"].reshape(-1), w1k, w2k).reshape(bb, nb, g, hd).transpose(0, 2, 1, 3)
        vc = compress(flat[1], p["pe_v"].reshape(-1), w1v, w2v).reshape(bb, nb, g, hd).transpose(0, 2, 1, 3)
        return kc, vc

    nb_p = t // CMP_BLOCK
    kc_p, vc_p = cmp_pair(new_rows_p[:, :nb_p * CMP_BLOCK, 0:2].reshape(b, nb_p, CMP_BLOCK, 2, g, hd))
    o_a_p = nsa_prompt(hab, kc_p, vc_p, slopes, cols, b=b, t=t, g=g, r=r)

    page = cache_kv.shape[1]
    n_pages = page_table.shape[1]
    nb_s = (n_pages * page + ts) // CMP_BLOCK
    assert nb_s * CMP_BLOCK == n_pages * page and page % CMP_BLOCK == 0
    pool = cache_kv.reshape(cache_kv.shape[0], page * 4 * g, hd)
    kc_s, vc_s = compress_sample(pool, page_table, p, page=page, g=g)
    o_a_s = nsa_sample(hab, kc_s, vc_s, pool, cache_win.reshape(bs, wb * 2 * g, hd), page_table, slopes, cols,
                       n_p=n_p, bs=bs, ts=ts, page=page, g=g, r=r)

    o_b_p, st_p = retention(hab, p["gn"], jnp.zeros((b, nh_r, hd, hd), F32), cols, row0=0, b=b, t=t, nh=nh_r)
    o_b_s, st_s = retention(hab, p["gn"], state, cols, row0=n_p, b=bs, t=ts, nh=nh_r)

    o = jnp.concatenate([jnp.concatenate([o_a_p, o_b_p.astype(BF16)], axis=1),
                         jnp.concatenate([o_a_s.astype(BF16), o_b_s.astype(BF16)], axis=1)], axis=0)
    x = mm_res(o, p["w_out"].astype(BF16), x)
    return x, (new_rows_p, win_state_p, st_p, new_rows_s, win_state_s, st_s)


def _odd_layer(x, n_p, b, t, bs, ts, p, cache_kv, cache_logf, page_table):
    hd = HEAD_DIM
    d = x.shape[1]
    nh = d // hd
    g = nh // 4
    r = nh // g
    cols = {"q": 0, "k": nh * hd, "v": (nh + g) * hd, "f": (nh + 2 * g) * hd}
    ncol_src = p["w_in"].shape[1]
    tn = 640
    ncol = -(-(cols["f"] + LANES) // tn) * tn
    w_in = jnp.pad(p["w_in"], ((0, 0), (0, ncol - ncol_src))).astype(BF16)
    hc, _ = mm_norm(x, p["norm"], w_in, tn)

    new_rows_p = hc[:n_p, cols["k"]:cols["f"]].reshape(b, t, 2, g, hd)
    new_rows_s = hc[n_p:, cols["k"]:cols["f"]].reshape(bs, ts, 2, g, hd)
    b_pad = jnp.pad(p["b_f"].astype(F32), (0, LANES - nh)).reshape(1, LANES)
    logf = fox_logf(hc, b_pad, cols["f"])
    new_logf_p = logf[:n_p, :nh].reshape(b, t, nh)
    new_logf_s = logf[n_p:, :nh].reshape(bs, ts, nh)

    f_p = cumsum_rows(logf, b=b, t=t)[:, :nh].reshape(b, t, g, r)
    o_p = fox_prompt(hc, f_p, cols, b=b, t=t, g=g, r=r)

    page = cache_kv.shape[1]
    new_t = jnp.pad(new_logf_s.transpose(0, 2, 1), ((0, 0), (0, 0), (0, page - ts)))
    fk, fn = fox_f_sample(cache_logf.transpose(0, 2, 1), new_t, page_table)
    fq = fn[:, :, :ts].reshape(bs, g, r * ts, 1)
    o_s = fox_sample(hc, fq, fk, fn, cache_kv.reshape(cache_kv.shape[0], page * 2 * g, hd), page_table, cols,
                     n_p=n_p, bs=bs, ts=ts, page=page, g=g, r=r)

    x = mm_res(jnp.concatenate([o_p, o_s.astype(BF16)], axis=0), p["w_out"].astype(BF16), x)
    return x, (new_rows_p, new_logf_p, new_rows_s, new_logf_s)


def _peer_layer(x, nw, wq, k1, k2, u, v):
    q, xn = mm_norm(x, nw, wq.astype(BF16), _pick(wq.shape[1], (1024, 512, 256, 128)))
    s1t, s2t, st = peer_score(q, k1, k2)
    return peer_dense(xn, u.astype(BF16), v.T.astype(BF16), s1t, s2t, st, x)


def kernel(x_prompt, x_sample, cache_nsa_kv, cache_nsa_win, state_ret, cache_fox_kv, cache_fox_logf,
           page_table, norm_mix, norm_ffn, norm_final, w_in_ab, w_out_ab, cmp_pe_k, cmp_w1_k, cmp_w2_k,
           cmp_pe_v, cmp_w1_v, cmp_w2_v, ret_gn, w_in_c, b_forget, w_out_c, peer_wq, peer_k1, peer_k2,
           peer_u, peer_v):
    b, t, d = x_prompt.shape
    bs, ts, _ = x_sample.shape
    n_p, n_s = b * t, bs * ts
    depth = norm_mix.shape[0]
    x = jnp.concatenate([x_prompt.reshape(n_p, d), x_sample.reshape(n_s, d)], axis=0)
    even, odd = [], []
    for l in range(depth):
        if l % 2 == 0:
            e = l // 2
            p = dict(norm=norm_mix[l], w_in=w_in_ab[e], w_out=w_out_ab[e], pe_k=cmp_pe_k[e], w1_k=cmp_w1_k[e],
                     w2_k=cmp_w2_k[e], pe_v=cmp_pe_v[e], w1_v=cmp_w1_v[e], w2_v=cmp_w2_v[e], gn=ret_gn[e])
            x, outs = _even_layer(x, n_p, b, t, bs, ts, p, cache_nsa_kv[e], cache_nsa_win[e], state_ret[e],
                                  page_table)
            even.append(outs)
        else:
            o = l // 2
            p = dict(norm=norm_mix[l], w_in=w_in_c[o], b_f=b_forget[o], w_out=w_out_c[o])
            x, outs = _odd_layer(x, n_p, b, t, bs, ts, p, cache_fox_kv[o], cache_fox_logf[o], page_table)
            odd.append(outs)
        x = _peer_layer(x, norm_ffn[l], peer_wq[l], peer_k1[l], peer_k2[l], peer_u[l], peer_v[l])
    y = rms_final(x, norm_final)
    y_prompt = y[:n_p].reshape(b, t, d)
    y_sample = y[n_p:].reshape(bs, ts, d)

    def stack(group, i):
        return jnp.stack([o[i] for o in group])

    return (y_prompt, y_sample, stack(even, 0), stack(even, 1), stack(even, 2), stack(odd, 0), stack(odd, 1),
            stack(even, 3), stack(even, 4), stack(even, 5), stack(odd, 2), stack(odd, 3))
```

```python
import functools
import math

import numpy as np
import jax
import jax.numpy as jnp
from jax import lax
from jax.experimental import pallas as pl
from jax.experimental.pallas import tpu as pltpu

F32 = jnp.float32
BF16 = jnp.bfloat16
I32 = jnp.int32

HEAD_DIM = 128
CMP_BLOCK = 64
SEL_BLOCK = 64
SEL_TOPK = 16
WINDOW = 512
CMP_HIDDEN = 256
RET_CHUNK = 128
Q_BLOCK = 128
PEER_HEADS = 8
PEER_DK = 256
PEER_TOPK = 16
EPS = 1e-6
GN_EPS = 1e-5
NEG = -1e30

LANES = 128
SUBLANES = 8
VMEM_LIMIT = 56 * 1024 * 1024

SDS = jax.ShapeDtypeStruct


def _cparams(sem):
    return pltpu.CompilerParams(dimension_semantics=sem, vmem_limit_bytes=VMEM_LIMIT)


def _dot(a, b):
    return jnp.dot(a, b, preferred_element_type=F32)


def _dot_nt(a, b):
    return lax.dot_general(a, b, (((1,), (1,)), ((), ())), preferred_element_type=F32)


def _dot_tn(a, b):
    return lax.dot_general(a, b, (((0,), (0,)), ((), ())), preferred_element_type=F32)


def _pick(n, cands):
    for c in cands:
        if c <= n and n % c == 0:
            return c
    raise ValueError(f"no tile for {n} in {cands}")


def _gelu(x):
    c = math.sqrt(2.0 / math.pi)
    return x * (0.5 * (1.0 + jnp.tanh(c * (x + 0.044715 * (x * x * x)))))


def _split3(x):
    hi = x.astype(BF16)
    r1 = x - hi.astype(F32)
    mid = r1.astype(BF16)
    lo = (r1 - mid.astype(F32)).astype(BF16)
    return hi, mid, lo


def _stack_heads(x, nh, hd):
    return jnp.concatenate([x[:, r * hd:(r + 1) * hd] for r in range(nh)], axis=0)


def _unstack_heads(x, nh, t):
    return jnp.concatenate([x[r * t:(r + 1) * t, :] for r in range(nh)], axis=1)


def _rep_rows(x, k):
    return jnp.concatenate([x] * k, axis=0)


def _mm_norm_body(x_ref, nw_ref, w_ref, o_ref, xn_ref):
    @pl.when(pl.program_id(1) == 0)
    def _():
        x = x_ref[...]
        ms = jnp.mean(x * x, axis=-1, keepdims=True)
        xn_ref[...] = (x * lax.rsqrt(ms + EPS) * nw_ref[...]).astype(BF16)

    o_ref[...] = _dot(xn_ref[...], w_ref[...])


def mm_norm(x, nw, w_bf, tn):
    n, d = x.shape
    nn = w_bf.shape[1]
    tm = _pick(n, (768, 512, 256, 128, 64, 32, 16))
    return pl.pallas_call(
        _mm_norm_body,
        grid=(n // tm, nn // tn),
        in_specs=[pl.BlockSpec((tm, d), lambda i, j: (i, 0)),
                  pl.BlockSpec((1, d), lambda i, j: (0, 0)),
                  pl.BlockSpec((d, tn), lambda i, j: (0, j))],
        out_specs=[pl.BlockSpec((tm, tn), lambda i, j: (i, j)),
                   pl.BlockSpec((tm, d), lambda i, j: (i, 0))],
        out_shape=[SDS((n, nn), F32), SDS((n, d), BF16)],
        compiler_params=_cparams(("parallel", "arbitrary")),
        name="mm_norm",
    )(x, nw.reshape(1, d), w_bf)


def _mm_res_body(a_ref, w_ref, r_ref, o_ref):
    o_ref[...] = r_ref[...] + _dot(a_ref[...], w_ref[...])


def mm_res(a_bf, w_bf, res):
    n, k = a_bf.shape
    nn = w_bf.shape[1]
    tm = _pick(n, (768, 512, 256, 128, 64, 32, 16))
    tn = _pick(nn, (1024, 512, 256, 128))
    return pl.pallas_call(
        _mm_res_body,
        grid=(n // tm, nn // tn),
        in_specs=[pl.BlockSpec((tm, k), lambda i, j: (i, 0)),
                  pl.BlockSpec((k, tn), lambda i, j: (0, j)),
                  pl.BlockSpec((tm, tn), lambda i, j: (i, j))],
        out_specs=pl.BlockSpec((tm, tn), lambda i, j: (i, j)),
        out_shape=SDS((n, nn), F32),
        compiler_params=_cparams(("parallel", "arbitrary")),
        name="mm_res",
    )(a_bf, w_bf, res)


def _rms_body(x_ref, nw_ref, o_ref):
    x = x_ref[...]
    ms = jnp.mean(x * x, axis=-1, keepdims=True)
    o_ref[...] = x * lax.rsqrt(ms + EPS) * nw_ref[...]


def rms_final(x, nw):
    n, d = x.shape
    tm = _pick(n, (768, 512, 256, 128, 64, 32, 16, 8))
    return pl.pallas_call(
        _rms_body,
        grid=(n // tm,),
        in_specs=[pl.BlockSpec((tm, d), lambda i: (i, 0)), pl.BlockSpec((1, d), lambda i: (0, 0))],
        out_specs=pl.BlockSpec((tm, d), lambda i: (i, 0)),
        out_shape=SDS((n, d), F32),
        compiler_params=_cparams(("parallel",)),
        name="rms_final",
    )(x, nw.reshape(1, d))


def _batcher_pairs(n):
    pairs = []
    p = 1
    while p < n:
        k = p
        while k >= 1:
            for j in range(k % p, n - k, 2 * k):
                for i in range(min(k, n - j - k)):
                    if (i + j) // (2 * p) == (i + j + k) // (2 * p):
                        pairs.append((i + j, i + j + k))
            k //= 2
        p *= 2
    return pairs


_SORT16 = _batcher_pairs(16)


def _sort16_desc(xs):
    xs = list(xs)
    for i, j in _SORT16:
        hi = jnp.maximum(xs[i], xs[j])
        lo = jnp.minimum(xs[i], xs[j])
        xs[i], xs[j] = hi, lo
    return xs


def _bitonic16_desc(c):
    c = list(c)
    for stride in (8, 4, 2, 1):
        for i in range(16):
            if i & stride == 0:
                hi = jnp.maximum(c[i], c[i + stride])
                lo = jnp.minimum(c[i], c[i + stride])
                c[i], c[i + stride] = hi, lo
    return c


def _merge16_desc(a, b):
    return _bitonic16_desc([jnp.maximum(a[i], b[15 - i]) for i in range(16)])


def _top16_sorted(s):
    cols = _sort16_desc([s[v * SUBLANES:(v + 1) * SUBLANES, :] for v in range(16)])
    for shift in (4, 2, 1):
        other = [pltpu.roll(x, shift, 0) for x in cols]
        cols = _merge16_desc(cols, other)
    return cols


def _peer_score_body(q_ref, k1_ref, k2_ref, s1_ref, s2_ref, st_ref):
    tm = q_ref.shape[0]
    half = PEER_DK // 2
    k1 = k1_ref[...].astype(BF16)
    k2 = k2_ref[...].astype(BF16)
    sub = lax.broadcasted_iota(I32, (SUBLANES, tm), 0)
    a_top = None
    b_top = None
    for h in range(PEER_HEADS):
        qh = q_ref[:, h * PEER_DK:(h + 1) * PEER_DK]
        s1 = _dot_nt(k1, qh[:, :half].astype(BF16))
        s2 = _dot_nt(k2, qh[:, half:].astype(BF16))
        s1_ref[h] = s1
        s2_ref[h] = s2
        a_h = _top16_sorted(s1)
        b_h = _top16_sorted(s2)
        if h == 0:
            a_top, b_top = a_h, b_h
        else:
            a_top = [jnp.where(sub == h, x, y) for x, y in zip(a_h, a_top)]
            b_top = [jnp.where(sub == h, x, y) for x, y in zip(b_h, b_top)]
    ninf = jnp.full((SUBLANES, tm), -jnp.inf, F32)
    row0 = [a_top[0] + b_top[b] for b in range(16)]
    col0 = [a_top[a] + b_top[0] for a in range(1, 16)] + [ninf]
    mid = ([a_top[1] + b_top[b] for b in range(1, 8)] + [a_top[a] + b_top[1] for a in range(2, 8)]
           + [a_top[2] + b_top[b] for b in range(2, 5)])
    mid = _sort16_desc(mid)
    v0 = a_top[3] + b_top[2]
    v1 = a_top[4] + b_top[2]
    v2 = a_top[3] + b_top[3]
    tail = [v0, jnp.maximum(v1, v2), jnp.minimum(v1, v2)] + [ninf] * 13
    top = _merge16_desc(_merge16_desc(_merge16_desc(row0, col0), mid), tail)
    z = jnp.zeros((SUBLANES, tm), F32)
    for i in range(16):
        z = z + jnp.exp(top[i] - top[0])
    st_ref[0] = top[15]
    st_ref[1] = a_top[0]
    st_ref[2] = b_top[0]
    st_ref[3] = z


def peer_score(q, k1, k2):
    n, d = q.shape
    nk = k1.shape[0]
    assert nk == 128 and d == PEER_HEADS * PEER_DK
    tm = _pick(n, (256, 128))
    return pl.pallas_call(
        _peer_score_body,
        grid=(n // tm,),
        in_specs=[pl.BlockSpec((tm, d), lambda i: (i, 0)),
                  pl.BlockSpec(k1.shape, lambda i: (0, 0)),
                  pl.BlockSpec(k2.shape, lambda i: (0, 0))],
        out_specs=[pl.BlockSpec((PEER_HEADS, nk, tm), lambda i: (0, 0, i)),
                   pl.BlockSpec((PEER_HEADS, nk, tm), lambda i: (0, 0, i)),
                   pl.BlockSpec((4, PEER_HEADS, tm), lambda i: (0, 0, i))],
        out_shape=[SDS((PEER_HEADS, nk, n), F32), SDS((PEER_HEADS, nk, n), F32),
                   SDS((4, PEER_HEADS, n), F32)],
        compiler_params=_cparams(("parallel",)),
        name="peer_score",
    )(q, k1, k2)


def _peer_dense_body(xn_ref, u_ref, vt_ref, s1_ref, s2_ref, st_ref, res_ref, o_ref,
                     acc_ref, e2_ref, h_ref, act_ref, *, nc, n_steps):
    c = pl.program_id(1)
    nk = s2_ref.shape[1]

    @pl.when(c == 0)
    def _():
        acc_ref[...] = jnp.zeros_like(acc_ref)
        for h in range(PEER_HEADS):
            e2_ref[h] = jnp.exp(s2_ref[h] - st_ref[2, h:h + 1, :])

    h_ref[...] = _dot_nt(u_ref[...], xn_ref[...])
    for s in range(nc):
        e1 = c * nc + s
        w = jnp.zeros((nk, xn_ref.shape[0]), F32)
        for h in range(PEER_HEADS):
            s1row = s1_ref[h, pl.ds(e1, 1), :]
            p1 = jnp.exp(s1row - st_ref[1, h:h + 1, :]) / st_ref[3, h:h + 1, :]
            a = s1row + s2_ref[h]
            w = w + jnp.where(a >= st_ref[0, h:h + 1, :], p1 * e2_ref[h], 0.0)
        act_ref[s * nk:(s + 1) * nk, :] = (_gelu(h_ref[s * nk:(s + 1) * nk, :]) * w).astype(BF16)
    acc_ref[...] += _dot(vt_ref[...], act_ref[...])

    @pl.when(c == n_steps - 1)
    def _():
        o_ref[...] = res_ref[...] + acc_ref[...].T


def peer_dense(xn_bf, u_bf, vt_bf, s1t, s2t, st, res):
    n, d = xn_bf.shape
    ne = u_bf.shape[0]
    nk = s1t.shape[1]
    tm = _pick(n, (768, 512, 256, 128))
    nc = 4
    te = nc * nk
    n_steps = ne // te
    body = functools.partial(_peer_dense_body, nc=nc, n_steps=n_steps)
    once = pl.Buffered(1)
    return pl.pallas_call(
        body,
        grid=(n // tm, n_steps),
        in_specs=[pl.BlockSpec((tm, d), lambda i, c: (i, 0), pipeline_mode=once),
                  pl.BlockSpec((te, d), lambda i, c: (c, 0)),
                  pl.BlockSpec((d, te), lambda i, c: (0, c)),
                  pl.BlockSpec((PEER_HEADS, nk, tm), lambda i, c: (0, 0, i), pipeline_mode=once),
                  pl.BlockSpec((PEER_HEADS, nk, tm), lambda i, c: (0, 0, i), pipeline_mode=once),
                  pl.BlockSpec((4, PEER_HEADS, tm), lambda i, c: (0, 0, i), pipeline_mode=once),
                  pl.BlockSpec((tm, d), lambda i, c: (i, 0), pipeline_mode=once)],
        out_specs=pl.BlockSpec((tm, d), lambda i, c: (i, 0)),
        out_shape=SDS((n, d), F32),
        scratch_shapes=[pltpu.VMEM((d, tm), F32), pltpu.VMEM((PEER_HEADS, nk, tm), F32),
                        pltpu.VMEM((te, tm), F32), pltpu.VMEM((te, tm), BF16)],
        compiler_params=_cparams(("parallel", "arbitrary")),
        name="peer_dense",
    )(xn_bf, u_bf, vt_bf, s1t, s2t, st, res)


def _flash_init(m_ref, l_ref, acc_ref):
    m_ref[...] = jnp.full(m_ref.shape, NEG, F32)
    l_ref[...] = jnp.zeros(l_ref.shape, F32)
    acc_ref[...] = jnp.zeros(acc_ref.shape, F32)


def _flash_update(s, mask, v_tiles, m_ref, l_ref, acc_ref):
    if mask is not None:
        s = jnp.where(mask, s, NEG)
    m_prev = m_ref[...]
    m_new = jnp.maximum(m_prev, jnp.max(s, axis=-1, keepdims=True))
    alpha = jnp.exp(m_prev - m_new)
    p = jnp.exp(s - m_new)
    if mask is not None:
        p = jnp.where(mask, p, 0.0)
    l_ref[...] = alpha * l_ref[...] + jnp.sum(p, axis=-1, keepdims=True)
    pb = p.astype(BF16)
    pv = None
    off = 0
    for v in v_tiles:
        part = _dot(pb[:, off:off + v.shape[0]], v)
        pv = part if pv is None else pv + part
        off += v.shape[0]
    acc_ref[...] = alpha * acc_ref[...] + pv
    m_ref[...] = m_new


def _flash_final(l_ref, acc_ref):
    return acc_ref[...] / jnp.maximum(l_ref[...], 1e-30)


def _softmax_masked(s, mask):
    s = jnp.where(mask, s, NEG)
    m = jnp.max(s, axis=-1, keepdims=True)
    e = jnp.where(mask, jnp.exp(s - m), 0.0)
    return e / jnp.maximum(jnp.sum(e, axis=-1, keepdims=True), 1e-30)


def _topk_mask(score, k):
    n = score.shape[-1]
    lane = lax.broadcasted_iota(I32, score.shape, 1).astype(F32)
    sel = jnp.zeros(score.shape, F32)
    for _ in range(k):
        m = jnp.max(score, axis=-1, keepdims=True)
        idx = jnp.min(jnp.where(score == m, lane, float(n)), axis=-1, keepdims=True)
        hit = lane == idx
        sel = jnp.where(hit, jnp.where(m >= 0.0, 1.0, 0.0), sel)
        score = jnp.where(hit, -jnp.inf, score)
    return sel


def _compress_body(x_ref, pe_ref, w1_ref, w2_ref, o_ref):
    xb = (x_ref[...] + pe_ref[...]).astype(BF16)
    h = _gelu(_dot(xb, w1_ref[...]))
    o_ref[...] = _dot(h.astype(BF16), w2_ref[...])


def compress(x, pe, w1_bf, w2_bf):
    rows, k = x.shape
    tr = _pick(rows, (256, 128, 64, 32, 16, 8))
    hid = w1_bf.shape[1]
    hd = w2_bf.shape[1]
    return pl.pallas_call(
        _compress_body,
        grid=(rows // tr,),
        in_specs=[pl.BlockSpec((tr, k), lambda i: (i, 0)),
                  pl.BlockSpec((1, k), lambda i: (0, 0)),
                  pl.BlockSpec((k, hid), lambda i: (0, 0)),
                  pl.BlockSpec((hid, hd), lambda i: (0, 0))],
        out_specs=pl.BlockSpec((tr, hd), lambda i: (i, 0)),
        out_shape=SDS((rows, hd), F32),
        compiler_params=_cparams(("parallel",)),
        name="nsa_compress",
    )(x, pe.reshape(1, k), w1_bf, w2_bf)


def _nsa_cmp_and_select(qs, slope, t1, kc_ref, vc_ref, *, r, nb, n_sel, ns_pad, k_top):
    tq = t1.shape[0]
    t = _rep_rows(t1, r)
    blk_end = lax.broadcasted_iota(I32, (1, nb), 1) * CMP_BLOCK + (CMP_BLOCK - 1)
    d_c = t - blk_end
    s = _dot_nt(qs, kc_ref[...].astype(BF16)) - slope * d_c.astype(F32)
    p_c = _softmax_masked(s, d_c >= 0)
    o_c = _dot(p_c.astype(BF16), vc_ref[...].astype(BF16))
    imp = p_c[0:tq]
    for i in range(1, r):
        imp = imp + p_c[i * tq:(i + 1) * tq]
    if ns_pad > nb:
        imp = jnp.concatenate([imp, jnp.zeros((tq, ns_pad - nb), F32)], axis=1)
    jsel = lax.broadcasted_iota(I32, (1, ns_pad), 1)
    cur = jnp.right_shift(t1, 6)
    forced = (jsel == 0) | (jsel == cur) | (jsel == cur - 1)
    score = jnp.where(forced, r + 1.0, jnp.where(jsel <= cur, imp, -1.0))
    score = jnp.where(jsel < n_sel, score, -2.0)
    return o_c, _topk_mask(score, k_top)


def _block_mask(sel_bf, first_pos, tk):
    ns_pad = sel_bf.shape[1]
    blk = lax.broadcasted_iota(I32, (ns_pad, tk), 0)
    key_blk = jnp.right_shift(first_pos + lax.broadcasted_iota(I32, (ns_pad, tk), 1), 6)
    expand = jnp.where(blk == key_blk, 1.0, 0.0).astype(BF16)
    return _dot(sel_bf, expand)


def _gate_cols(gsig, c, r):
    return jnp.concatenate([gsig[:, c * r + i:c * r + i + 1] for i in range(r)], axis=0)


def _nsa_p_body(q_ref, gt_ref, kc_ref, vc_ref, ks_ref, vs_ref, kw_ref, vw_ref, sl_ref, o_ref,
                m_ref, l_ref, acc_ref, *, tq, tk, wk, nb, n_sel, k_top, scale, r):
    qi = pl.program_id(2)
    a = qi * tq
    qs = (_stack_heads(q_ref[...], r, HEAD_DIM) * scale).astype(BF16)
    slope = sl_ref[...]
    t1 = a + lax.broadcasted_iota(I32, (tq, 1), 0)
    t = _rep_rows(t1, r)
    o_c, sel = _nsa_cmp_and_select(qs, slope, t1, kc_ref, vc_ref, r=r, nb=nb, n_sel=n_sel,
                                   ns_pad=n_sel, k_top=k_top)
    sel_bf = sel.astype(BF16)
    n_it = (a + tq - 1) // tk + 1

    def sel_step(j, carry):
        start = pl.multiple_of(j * tk, tk)
        k = ks_ref[pl.ds(start, tk), :].astype(BF16)
        v = vs_ref[pl.ds(start, tk), :].astype(BF16)
        d = t - (start + lax.broadcasted_iota(I32, (1, tk), 1))
        s = _dot_nt(qs, k) - slope * d.astype(F32)
        bm = _rep_rows(_block_mask(sel_bf, start, tk), r)
        _flash_update(s, (bm > 0.5) & (d >= 0), [v], m_ref, l_ref, acc_ref)
        return carry

    _flash_init(m_ref, l_ref, acc_ref)
    lax.fori_loop(0, n_it, sel_step, 0)
    o_s = _flash_final(l_ref, acc_ref)

    start = pl.multiple_of(jnp.maximum(a + tq - wk, 0), SUBLANES)
    k = kw_ref[pl.ds(start, wk), :].astype(BF16)
    v = vw_ref[pl.ds(start, wk), :].astype(BF16)
    d = t - (start + lax.broadcasted_iota(I32, (1, wk), 1))
    s = _dot_nt(qs, k) - slope * d.astype(F32)
    _flash_init(m_ref, l_ref, acc_ref)
    _flash_update(s, (d >= 0) & (d <= WINDOW), [v], m_ref, l_ref, acc_ref)
    o_w = _flash_final(l_ref, acc_ref)

    gsig = 1.0 / (1.0 + jnp.exp(-gt_ref[...]))
    o = _gate_cols(gsig, 0, r) * o_c + _gate_cols(gsig, 1, r) * o_s + _gate_cols(gsig, 2, r) * o_w
    o_ref[...] = _unstack_heads(o, r, tq).astype(BF16)


def nsa_prompt(hab, kc, vc, slopes, cols, *, b, t, g, r):
    hd = HEAD_DIM
    nb = kc.shape[2]
    n_sel = -(-t // SEL_BLOCK)
    assert n_sel == nb and t % SEL_BLOCK == 0
    tq = Q_BLOCK if t % Q_BLOCK == 0 else t
    tk = _pick(t, (512, 256, 128, t))
    wk = min(WINDOW + tq, t)
    nq = t // tq
    qw = r * hd
    body = functools.partial(_nsa_p_body, tq=tq, tk=tk, wk=wk, nb=nb, n_sel=n_sel, k_top=min(SEL_TOPK, n_sel),
                             scale=hd ** -0.5, r=r)

    def seq_spec(col0):
        return pl.BlockSpec((t, hd), lambda bb, gg, qi: (bb, col0 // hd + gg))

    return pl.pallas_call(
        body,
        grid=(b, g, nq),
        in_specs=[pl.BlockSpec((tq, qw), lambda bb, gg, qi: (bb * nq + qi, gg)),
                  pl.BlockSpec((tq, LANES), lambda bb, gg, qi: (bb * nq + qi, cols["gt"] // LANES + gg)),
                  pl.BlockSpec((None, None, nb, hd), lambda bb, gg, qi: (bb, gg, 0, 0)),
                  pl.BlockSpec((None, None, nb, hd), lambda bb, gg, qi: (bb, gg, 0, 0)),
                  seq_spec(cols["ks"]), seq_spec(cols["vs"]), seq_spec(cols["kw"]), seq_spec(cols["vw"]),
                  pl.BlockSpec((None, r * tq, 1), lambda bb, gg, qi: (gg, 0, 0))],
        out_specs=pl.BlockSpec((tq, qw), lambda bb, gg, qi: (bb * nq + qi, gg)),
        out_shape=SDS((b * t, g * qw), BF16),
        scratch_shapes=[pltpu.VMEM((r * tq, 1), F32), pltpu.VMEM((r * tq, 1), F32),
                        pltpu.VMEM((r * tq, hd), F32)],
        compiler_params=_cparams(("parallel", "parallel", "arbitrary")),
        name="nsa_prompt",
    )(hab, hab, kc, vc, hab, hab, hab, hab, jnp.repeat(slopes, tq, axis=1).reshape(g, r * tq, 1))


def _pad_rows(x, rows):
    return jnp.concatenate([x, jnp.zeros((rows - x.shape[0], x.shape[1]), x.dtype)], axis=0)


def _block_expand(ns_pad, first_pos, width):
    blk = lax.broadcasted_iota(I32, (ns_pad, width), 0)
    key_blk = jnp.right_shift(first_pos + lax.broadcasted_iota(I32, (ns_pad, width), 1), 6)
    return jnp.where(blk == key_blk, 1.0, 0.0).astype(BF16)


def _nsa_s_body(pt_ref, q_ref, gt_ref, ksn_ref, vsn_ref, kwn_ref, vwn_ref, kc_ref, vc_ref, win_ref,
                sl_ref, *rest, pp, n_chunks, ts, page, offset, wb, nb, n_sel, ns_pad, k_top, scale, g, r):
    pages = rest[:pp]
    o_ref = rest[pp]
    qs_ref, oc_ref, sel_ref, m_ref, l_ref, acc_ref = rest[pp + 1:]
    del pt_ref
    c = pl.program_id(1)
    hd = HEAD_DIM
    t1 = offset + lax.broadcasted_iota(I32, (ts, 1), 0)
    t = _rep_rows(t1, r)
    lane = lax.broadcasted_iota(I32, (1, page), 1)

    @pl.when(c == 0)
    def _():
        for gg in range(g):
            qs = (_stack_heads(q_ref[:, gg * r * hd:(gg + 1) * r * hd], r, hd) * scale).astype(BF16)
            qs_ref[gg] = qs
            o_c, sel = _nsa_cmp_and_select(qs, sl_ref[gg], t1, kc_ref.at[gg], vc_ref.at[gg], r=r, nb=nb,
                                           n_sel=n_sel, ns_pad=ns_pad, k_top=k_top)
            oc_ref[gg] = o_c
            sel_ref[gg] = _rep_rows(sel, r).astype(BF16)
            _flash_init(m_ref.at[gg], l_ref.at[gg], acc_ref.at[gg])

    width = pp * page
    first = c * width
    d = t - (first + lax.broadcasted_iota(I32, (1, width), 1))
    expand = _block_expand(ns_pad, first, width)
    for gg in range(g):
        ks = [pages[i][0, pl.ds(2 * g + gg, page, stride=4 * g), :].astype(BF16) for i in range(pp)]
        vs = [pages[i][0, pl.ds(3 * g + gg, page, stride=4 * g), :].astype(BF16) for i in range(pp)]
        s = jnp.concatenate([_dot_nt(qs_ref[gg], k) for k in ks], axis=1) - sl_ref[gg] * d.astype(F32)
        bm = _dot(sel_ref[gg], expand)
        _flash_update(s, (bm > 0.5) & (d >= 0), vs, m_ref.at[gg], l_ref.at[gg], acc_ref.at[gg])

    @pl.when(c == n_chunks - 1)
    def _():
        gsig = 1.0 / (1.0 + jnp.exp(-gt_ref[...]))
        dn = t - (offset + lane)
        expand_n = _block_expand(ns_pad, offset, page)
        outs = []
        for gg in range(g):
            k = _pad_rows(ksn_ref[:, gg * hd:(gg + 1) * hd], page).astype(BF16)
            v = _pad_rows(vsn_ref[:, gg * hd:(gg + 1) * hd], page).astype(BF16)
            s = _dot_nt(qs_ref[gg], k) - sl_ref[gg] * dn.astype(F32)
            bm = _dot(sel_ref[gg], expand_n)
            _flash_update(s, (bm > 0.5) & (dn >= 0) & (lane < ts), [v], m_ref.at[gg], l_ref.at[gg],
                          acc_ref.at[gg])
            o_s = _flash_final(l_ref.at[gg], acc_ref.at[gg])
            _flash_init(m_ref.at[gg], l_ref.at[gg], acc_ref.at[gg])
            kw = win_ref[pl.ds(gg, wb, stride=2 * g), :].astype(BF16)
            vw = win_ref[pl.ds(g + gg, wb, stride=2 * g), :].astype(BF16)
            kn = _pad_rows(kwn_ref[:, gg * hd:(gg + 1) * hd], page).astype(BF16)
            vn = _pad_rows(vwn_ref[:, gg * hd:(gg + 1) * hd], page).astype(BF16)
            dw = t - (offset - wb + lax.broadcasted_iota(I32, (1, wb), 1))
            s = jnp.concatenate([_dot_nt(qs_ref[gg], kw) - sl_ref[gg] * dw.astype(F32),
                                 _dot_nt(qs_ref[gg], kn) - sl_ref[gg] * dn.astype(F32)], axis=1)
            mask = jnp.concatenate([(dw >= 0) & (dw <= WINDOW), (dn >= 0) & (dn <= WINDOW) & (lane < ts)], axis=1)
            _flash_update(s, mask, [vw, vn], m_ref.at[gg], l_ref.at[gg], acc_ref.at[gg])
            o_w = _flash_final(l_ref.at[gg], acc_ref.at[gg])
            gs = gsig[:, gg * LANES:(gg + 1) * LANES]
            o = _gate_cols(gs, 0, r) * oc_ref[gg] + _gate_cols(gs, 1, r) * o_s + _gate_cols(gs, 2, r) * o_w
            outs.append(_unstack_heads(o, r, ts))
        o_ref[...] = jnp.concatenate(outs, axis=1)


def nsa_sample(hab, kc, vc, pool, win, page_table, slopes, cols, *, n_p, bs, ts, page, g, r):
    hd = HEAD_DIM
    n_pages = page_table.shape[1]
    offset = n_pages * page
    wb = win.shape[1] // (2 * g)
    assert offset % SEL_BLOCK == 0 and offset - wb >= 0 and wb == WINDOW
    nb = kc.shape[2]
    n_sel = -(-(offset + ts) // SEL_BLOCK)
    ns_pad = -(-n_sel // LANES) * LANES
    pp = _pick(n_pages, (8, 4, 2, 1))
    n_chunks = n_pages // pp
    rb = n_p // ts
    body = functools.partial(_nsa_s_body, pp=pp, n_chunks=n_chunks, ts=ts, page=page, offset=offset, wb=wb,
                             nb=nb, n_sel=n_sel, ns_pad=ns_pad, k_top=min(SEL_TOPK, n_sel),
                             scale=hd ** -0.5, g=g, r=r)

    def row_spec(width, col0):
        return pl.BlockSpec((ts, width), lambda b, c, pt: (rb + b, col0 // width))

    def page_spec(i):
        return pl.BlockSpec((1, page * 4 * g, hd), lambda b, c, pt: (pt[b, c * pp + i], 0, 0))

    in_specs = [row_spec(g * r * hd, cols["q"]), row_spec(g * LANES, cols["gt"]),
                row_spec(g * hd, cols["ks"]), row_spec(g * hd, cols["vs"]),
                row_spec(g * hd, cols["kw"]), row_spec(g * hd, cols["vw"]),
                pl.BlockSpec((None, g, nb, hd), lambda b, c, pt: (b, 0, 0, 0)),
                pl.BlockSpec((None, g, nb, hd), lambda b, c, pt: (b, 0, 0, 0)),
                pl.BlockSpec((None, wb * 2 * g, hd), lambda b, c, pt: (b, 0, 0)),
                pl.BlockSpec((g, r * ts, 1), lambda b, c, pt: (0, 0, 0))]
    in_specs += [page_spec(i) for i in range(pp)]
    grid_spec = pltpu.PrefetchScalarGridSpec(
        num_scalar_prefetch=1, grid=(bs, n_chunks), in_specs=in_specs,
        out_specs=pl.BlockSpec((ts, g * r * hd), lambda b, c, pt: (b, 0)),
        scratch_shapes=[pltpu.VMEM((g, r * ts, hd), BF16), pltpu.VMEM((g, r * ts, hd), F32),
                        pltpu.VMEM((g, r * ts, ns_pad), BF16), pltpu.VMEM((g, r * ts, 1), F32),
                        pltpu.VMEM((g, r * ts, 1), F32), pltpu.VMEM((g, r * ts, hd), F32)])
    return pl.pallas_call(
        body, grid_spec=grid_spec, out_shape=SDS((bs * ts, g * r * hd), F32),
        compiler_params=_cparams(("parallel", "arbitrary")), name="nsa_sample",
    )(page_table, hab, hab, hab, hab, hab, hab, kc, vc, win,
      jnp.repeat(slopes, ts, axis=1).reshape(g, r * ts, 1), *([pool] * pp))


def _cmp_s_body(pt_ref, pek_ref, w1k_ref, w2k_ref, pev_ref, w1v_ref, w2v_ref, *rest, pp, page, g):
    pages = rest[:pp]
    kc_ref, vc_ref, stage_ref, x_ref = rest[pp:]
    del pt_ref
    hd = HEAD_DIM
    rows = pp * (page // CMP_BLOCK)
    for cg in range(2 * g):
        for i in range(pp):
            stage_ref[cg, i * page:(i + 1) * page, :] = pages[i][0, pl.ds(cg, page, stride=4 * g), :]
    for comp, (pe_ref, w1_ref, w2_ref, out_ref) in enumerate(
            ((pek_ref, w1k_ref, w2k_ref, kc_ref), (pev_ref, w1v_ref, w2v_ref, vc_ref))):
        for l in range(CMP_BLOCK):
            x = jnp.concatenate([stage_ref[comp * g + gg, pl.ds(l, rows, stride=CMP_BLOCK), :]
                                 for gg in range(g)], axis=0)
            x_ref[:, l * hd:(l + 1) * hd] = (x + pe_ref[l:l + 1, :]).astype(BF16)
        out = _dot(_gelu(_dot(x_ref[...], w1_ref[...])).astype(BF16), w2_ref[...])
        for gg in range(g):
            out_ref[gg] = out[gg * rows:(gg + 1) * rows]


def compress_sample(pool, page_table, p, *, page, g):
    hd = HEAD_DIM
    bs, n_pages = page_table.shape
    pp = _pick(n_pages, (16, 8, 4))
    rows = pp * (page // CMP_BLOCK)
    nb = n_pages * (page // CMP_BLOCK)
    hid = p["w1_k"].shape[1]
    body = functools.partial(_cmp_s_body, pp=pp, page=page, g=g)
    once = pl.Buffered(1)

    def wspecs():
        return [pl.BlockSpec((CMP_BLOCK, hd), lambda b, c, pt: (0, 0)),
                pl.BlockSpec((CMP_BLOCK * hd, hid), lambda b, c, pt: (0, 0), pipeline_mode=once),
                pl.BlockSpec((hid, hd), lambda b, c, pt: (0, 0))]

    def page_spec(i):
        return pl.BlockSpec((1, page * 4 * g, hd), lambda b, c, pt: (pt[b, c * pp + i], 0, 0))

    grid_spec = pltpu.PrefetchScalarGridSpec(
        num_scalar_prefetch=1, grid=(bs, n_pages // pp),
        in_specs=wspecs() + wspecs() + [page_spec(i) for i in range(pp)],
        out_specs=[pl.BlockSpec((None, g, rows, hd), lambda b, c, pt: (b, 0, c, 0)),
                   pl.BlockSpec((None, g, rows, hd), lambda b, c, pt: (b, 0, c, 0))],
        scratch_shapes=[pltpu.VMEM((2 * g, pp * page, hd), F32), pltpu.VMEM((g * rows, CMP_BLOCK * hd), BF16)])
    return pl.pallas_call(
        body, grid_spec=grid_spec, out_shape=[SDS((bs, g, nb, hd), F32), SDS((bs, g, nb, hd), F32)],
        compiler_params=_cparams(("parallel", "arbitrary")), name="nsa_compress_sample",
    )(page_table, p["pe_k"], p["w1_k"].astype(BF16), p["w2_k"].astype(BF16),
      p["pe_v"], p["w1_v"].astype(BF16), p["w2_v"].astype(BF16), *([pool] * pp))


def _ret_body(q_ref, k_ref, v_ref, g_ref, gn_ref, s0_ref, dm_ref, cr_ref, kd_ref, cd_ref, o_ref, s_ref,
              st_ref, *, scale, n_chunks, nh):
    c = pl.program_id(1)
    dk = HEAD_DIM

    @pl.when(c == 0)
    def _():
        st_ref[...] = s0_ref[...]

    for h in range(nh):
        cs = slice(h * dk, (h + 1) * dk)
        qb = q_ref[:, cs].astype(BF16)
        ks = k_ref[:, cs] * scale
        kb = ks.astype(BF16)
        vb = v_ref[:, cs].astype(BF16)
        st = st_ref[h]
        att = _dot_nt(qb, kb) * dm_ref[h]
        o = _dot(att.astype(BF16), vb) + _dot(qb, st.astype(BF16)) * cr_ref[h]
        st_ref[h] = st * cd_ref[h] + _dot_tn((ks * kd_ref[h]).astype(BF16), vb)
        mu = jnp.mean(o, axis=-1, keepdims=True)
        var = jnp.mean(jnp.square(o - mu), axis=-1, keepdims=True)
        on = (o - mu) * lax.rsqrt(var + GN_EPS) * gn_ref[:, cs]
        gate = g_ref[:, cs]
        o_ref[:, cs] = ((gate * (1.0 / (1.0 + jnp.exp(-gate)))) * on).astype(o_ref.dtype)

    @pl.when(c == n_chunks - 1)
    def _():
        s_ref[...] = st_ref[...]


def retention(hab, gn_w, s0, cols, *, row0, b, t, nh):
    dk = HEAD_DIM
    w = nh * dk
    ch = RET_CHUNK if t % RET_CHUNK == 0 else t
    n_chunks = t // ch
    lg = jnp.log1p(-jnp.exp2(-5.0 - jnp.arange(nh, dtype=F32)))
    i = jnp.arange(ch, dtype=F32)
    diff = i[:, None] - i[None, :]
    dmask = jnp.where(diff >= 0, jnp.exp(jnp.maximum(diff, 0.0)[None] * lg[:, None, None]), 0.0)
    cross = jnp.exp((i + 1.0)[None, :] * lg[:, None]).reshape(nh, ch, 1)
    kdec = jnp.exp((ch - 1.0 - i)[None, :] * lg[:, None]).reshape(nh, ch, 1)
    cdec = jnp.exp(ch * lg).reshape(nh, 1, 1)
    rb = row0 // ch
    body = functools.partial(_ret_body, scale=dk ** -0.5, n_chunks=n_chunks, nh=nh)

    def col_spec(col0):
        assert col0 % w == 0
        return pl.BlockSpec((ch, w), lambda bb, c: (rb + bb * n_chunks + c, col0 // w))

    def whole(shape):
        return pl.BlockSpec(shape, lambda bb, c: (0,) * len(shape))

    return pl.pallas_call(
        body,
        grid=(b, n_chunks),
        in_specs=[col_spec(cols["rq"]), col_spec(cols["rk"]), col_spec(cols["rv"]), col_spec(cols["rg"]),
                  whole((1, w)),
                  pl.BlockSpec((None, nh, dk, dk), lambda bb, c: (bb, 0, 0, 0)),
                  whole((nh, ch, ch)), whole((nh, ch, 1)), whole((nh, ch, 1)), whole((nh, 1, 1))],
        out_specs=[pl.BlockSpec((ch, w), lambda bb, c: (bb * n_chunks + c, 0)),
                   pl.BlockSpec((None, nh, dk, dk), lambda bb, c: (bb, 0, 0, 0))],
        out_shape=[SDS((b * t, w), BF16 if ch % 16 == 0 else F32), SDS((b, nh, dk, dk), F32)],
        scratch_shapes=[pltpu.VMEM((nh, dk, dk), F32)],
        compiler_params=_cparams(("parallel", "arbitrary")),
        name="retention",
    )(hab, hab, hab, hab, gn_w.reshape(1, w), s0, dmask, cross, kdec, cdec)


def _logf_body(x_ref, b_ref, o_ref):
    x = x_ref[...] + b_ref[...]
    o_ref[...] = -(jnp.maximum(-x, 0.0) + jnp.log1p(jnp.exp(-jnp.abs(x))))


def fox_logf(hc, b_pad, col0):
    n = hc.shape[0]
    tm = _pick(n, (768, 512, 256, 128, 64, 32, 16, 8))
    return pl.pallas_call(
        _logf_body,
        grid=(n // tm,),
        in_specs=[pl.BlockSpec((tm, LANES), lambda i: (i, col0 // LANES)),
                  pl.BlockSpec((1, LANES), lambda i: (0, 0))],
        out_specs=pl.BlockSpec((tm, LANES), lambda i: (i, 0)),
        out_shape=SDS((n, LANES), F32),
        compiler_params=_cparams(("parallel",)),
        name="fox_logf",
    )(hc, b_pad)


def _cumsum_rows_body(x_ref, o_ref, hi_ref, mid_ref, lo_ref, carry_ref):
    @pl.when(pl.program_id(1) == 0)
    def _():
        carry_ref[...] = jnp.zeros_like(carry_ref)

    tc = x_ref.shape[0]
    tri = jnp.where(lax.broadcasted_iota(I32, (tc, tc), 1) <= lax.broadcasted_iota(I32, (tc, tc), 0),
                    1.0, 0.0).astype(BF16)
    hi, mid, lo = _split3(x_ref[...])
    f = (_dot(tri, hi) + _dot(tri, mid)) + _dot(tri, lo) + carry_ref[...]
    o_ref[...] = f
    hi_ref[...], mid_ref[...], lo_ref[...] = _split3(f)
    carry_ref[...] = f[tc - 1:tc, :]


def cumsum_rows(x, *, b, t):
    tc = 128 if t % 128 == 0 else t
    nc = t // tc
    spec = pl.BlockSpec((tc, LANES), lambda bb, c: (bb * nc + c, 0))
    return pl.pallas_call(
        _cumsum_rows_body,
        grid=(b, nc),
        in_specs=[spec],
        out_specs=[spec, spec, spec, spec],
        out_shape=[SDS((b * t, LANES), F32)] + [SDS((b * t, LANES), BF16)] * 3,
        scratch_shapes=[pltpu.VMEM((1, LANES), F32)],
        compiler_params=_cparams(("parallel", "arbitrary")),
        name="fox_cumsum",
    )(x)


def _fox_p_body(q_ref, k_ref, v_ref, fq_ref, fa_ref, bs_ref, o_ref, m_ref, l_ref, acc_ref, *, tq, tk, scale, r):
    qi = pl.program_id(2)
    hd = HEAD_DIM
    rows = r * tq
    qs = (_stack_heads(q_ref[...], r, hd) * scale).astype(BF16)
    fq = jnp.concatenate([fq_ref[:, i:i + 1] for i in range(r)], axis=0)
    hi, mid, lo = _split3(fq)
    lane = lax.broadcasted_iota(I32, (rows, LANES), 1)
    qb = (bs_ref[...] + jnp.where(lane == 3 * r, hi.astype(F32), 0.0) + jnp.where(lane == 3 * r + 1, mid.astype(F32), 0.0)
          + jnp.where(lane == 3 * r + 2, lo.astype(F32), 0.0)).astype(BF16)
    qa = jnp.concatenate([qs, qb], axis=1)
    t_row = qi * tq + jnp.concatenate([lax.broadcasted_iota(I32, (1, tq), 1)] * r, axis=1)

    def tile(j, masked):
        start = pl.multiple_of(j * tk, tk)
        ka = jnp.concatenate([k_ref[pl.ds(start, tk), :].astype(BF16), fa_ref[pl.ds(start, tk), :]], axis=1)
        v = v_ref[pl.ds(start, tk), :].astype(BF16)
        s = _dot_nt(ka, qa)
        if masked:
            mask = start + lax.broadcasted_iota(I32, (tk, 1), 0) <= t_row
            s = jnp.where(mask, s, NEG)
        m_prev = m_ref[...]
        m_new = jnp.maximum(m_prev, jnp.max(s, axis=0, keepdims=True))
        alpha = jnp.exp(m_prev - m_new)
        p = jnp.exp(s - m_new)
        if masked:
            p = jnp.where(mask, p, 0.0)
        l_ref[...] = alpha * l_ref[...] + jnp.sum(p, axis=0, keepdims=True)
        acc_ref[...] = alpha * acc_ref[...] + _dot_tn(v, p.astype(BF16))
        m_ref[...] = m_new

    def full_tile(j, carry):
        tile(j, False)
        return carry

    _flash_init(m_ref, l_ref, acc_ref)
    n_full = (qi * tq) // tk
    lax.fori_loop(0, n_full, full_tile, 0)
    for jj in range(tq // tk):
        tile(n_full + jj, True)
    o = acc_ref[...] / jnp.maximum(l_ref[...], 1e-30)
    o_ref[...] = jnp.concatenate([o[:, i * tq:(i + 1) * tq].T for i in range(r)], axis=1).astype(BF16)


def fox_prompt(hc, f, f_split, cols, *, b, t, g, r):
    hd = HEAD_DIM
    tq = _pick(t, (512, 256, 128, t))
    tk = _pick(tq, (256, 128, tq))
    nq = t // tq
    qw = r * hd
    fq = f.transpose(0, 2, 1, 3)
    fa = jnp.concatenate([piece.transpose(0, 2, 1, 3) for piece in f_split]
                         + [jnp.ones((b, g, t, 3), BF16), jnp.zeros((b, g, t, LANES - 3 * r - 3), BF16)],
                         axis=-1)
    rows = np.arange(r * tq)[:, None] // tq
    lanes = np.arange(LANES)[None, :]
    bsel = jnp.asarray(np.where((lanes < 3 * r) & (lanes % r == rows), -1.0, 0.0), F32)
    body = functools.partial(_fox_p_body, tq=tq, tk=tk, scale=hd ** -0.5, r=r)
    return pl.pallas_call(
        body,
        grid=(b, g, nq),
        in_specs=[pl.BlockSpec((tq, qw), lambda bb, gg, qi: (bb * nq + qi, gg)),
                  pl.BlockSpec((t, hd), lambda bb, gg, qi: (bb, cols["k"] // hd + gg)),
                  pl.BlockSpec((t, hd), lambda bb, gg, qi: (bb, cols["v"] // hd + gg)),
                  pl.BlockSpec((None, None, tq, r), lambda bb, gg, qi: (bb, gg, qi, 0)),
                  pl.BlockSpec((None, None, t, LANES), lambda bb, gg, qi: (bb, gg, 0, 0)),
                  pl.BlockSpec((r * tq, LANES), lambda bb, gg, qi: (0, 0))],
        out_specs=pl.BlockSpec((tq, qw), lambda bb, gg, qi: (bb * nq + qi, gg)),
        out_shape=SDS((b * t, g * qw), BF16),
        scratch_shapes=[pltpu.VMEM((1, r * tq), F32), pltpu.VMEM((1, r * tq), F32),
                        pltpu.VMEM((hd, r * tq), F32)],
        compiler_params=_cparams(("parallel", "parallel", "arbitrary")),
        name="fox_prompt",
    )(hc, hc, hc, fq, fa, bsel)


def _fox_f_body(pt_ref, new_ref, *rest, pp, n_chunks):
    pages = rest[:pp]
    fk_ref, fn_ref, carry_ref = rest[pp:]
    del pt_ref
    c = pl.program_id(1)
    page = pages[0].shape[2]

    @pl.when(c == 0)
    def _():
        carry_ref[...] = jnp.zeros_like(carry_ref)

    ut = jnp.where(lax.broadcasted_iota(I32, (page, page), 0) <= lax.broadcasted_iota(I32, (page, page), 1),
                   1.0, 0.0).astype(BF16)

    def csum(x, carry):
        hi, mid, lo = _split3(x)
        return (_dot(hi, ut) + _dot(mid, ut)) + _dot(lo, ut) + carry

    carry = carry_ref[...]
    for i in range(pp):
        f = csum(pages[i][0], carry)
        fk_ref[:, i * page:(i + 1) * page] = f
        carry = f[:, page - 1:page]
    carry_ref[...] = carry

    @pl.when(c == n_chunks - 1)
    def _():
        fn_ref[...] = csum(new_ref[...], carry)


def fox_f_sample(logf_pool_t, new_t, page_table):
    bs, n_pages = page_table.shape
    _, nh, page = logf_pool_t.shape
    pp = _pick(n_pages, (16, 8, 4, 2, 1))
    n_chunks = n_pages // pp
    body = functools.partial(_fox_f_body, pp=pp, n_chunks=n_chunks)
    in_specs = [pl.BlockSpec((None, nh, page), lambda b, c, pt: (b, 0, 0))]
    in_specs += [pl.BlockSpec((1, nh, page), functools.partial(lambda b, c, pt, i: (pt[b, c * pp + i], 0, 0), i=i))
                 for i in range(pp)]
    grid_spec = pltpu.PrefetchScalarGridSpec(
        num_scalar_prefetch=1, grid=(bs, n_chunks), in_specs=in_specs,
        out_specs=[pl.BlockSpec((None, nh, pp * page), lambda b, c, pt: (b, 0, c)),
                   pl.BlockSpec((None, nh, page), lambda b, c, pt: (b, 0, 0))],
        scratch_shapes=[pltpu.VMEM((nh, 1), F32)])
    return pl.pallas_call(
        body, grid_spec=grid_spec,
        out_shape=[SDS((bs, nh, n_pages * page), F32), SDS((bs, nh, page), F32)],
        compiler_params=_cparams(("parallel", "arbitrary")), name="fox_f_sample",
    )(page_table, new_t, *([logf_pool_t] * pp))


def _fox_s_body(pt_ref, q_ref, kn_ref, vn_ref, fq_ref, fk_ref, fn_ref, *rest, pp, n_chunks, ts, page, offset,
                scale, g, r):
    pages = rest[:pp]
    o_ref = rest[pp]
    qs_ref, m_ref, l_ref, acc_ref = rest[pp + 1:]
    del pt_ref
    c = pl.program_id(1)
    hd = HEAD_DIM

    @pl.when(c == 0)
    def _():
        for gg in range(g):
            qs_ref[gg] = (_stack_heads(q_ref[:, gg * r * hd:(gg + 1) * r * hd], r, hd) * scale).astype(BF16)
            _flash_init(m_ref.at[gg], l_ref.at[gg], acc_ref.at[gg])

    def fk_rows(f, gg):
        return jnp.concatenate([jnp.broadcast_to(f[gg * r + i:gg * r + i + 1, :], (ts, f.shape[1]))
                                for i in range(r)], axis=0)

    fk = fk_ref[...]
    for gg in range(g):
        ks = [pages[i][0, pl.ds(gg, page, stride=2 * g), :].astype(BF16) for i in range(pp)]
        vs = [pages[i][0, pl.ds(g + gg, page, stride=2 * g), :].astype(BF16) for i in range(pp)]
        s = jnp.concatenate([_dot_nt(qs_ref[gg], k) for k in ks], axis=1) + fq_ref[gg] - fk_rows(fk, gg)
        _flash_update(s, None, vs, m_ref.at[gg], l_ref.at[gg], acc_ref.at[gg])

    @pl.when(c == n_chunks - 1)
    def _():
        t = _rep_rows(offset + lax.broadcasted_iota(I32, (ts, 1), 0), r)
        lane = lax.broadcasted_iota(I32, (1, page), 1)
        outs = []
        for gg in range(g):
            k = _pad_rows(kn_ref[:, gg * hd:(gg + 1) * hd], page).astype(BF16)
            v = _pad_rows(vn_ref[:, gg * hd:(gg + 1) * hd], page).astype(BF16)
            s = _dot_nt(qs_ref[gg], k) + fq_ref[gg] - fk_rows(fn_ref[...], gg)
            _flash_update(s, (offset + lane <= t) & (lane < ts), [v], m_ref.at[gg], l_ref.at[gg], acc_ref.at[gg])
            outs.append(_unstack_heads(_flash_final(l_ref.at[gg], acc_ref.at[gg]), r, ts))
        o_ref[...] = jnp.concatenate(outs, axis=1)


def fox_sample(hc, fq, fk, fn, pool, page_table, cols, *, n_p, bs, ts, page, g, r):
    hd = HEAD_DIM
    n_pages = page_table.shape[1]
    nh = g * r
    offset = n_pages * page
    pp = _pick(n_pages, (8, 4, 2, 1))
    n_chunks = n_pages // pp
    rb = n_p // ts
    body = functools.partial(_fox_s_body, pp=pp, n_chunks=n_chunks, ts=ts, page=page, offset=offset,
                             scale=hd ** -0.5, g=g, r=r)

    def row_spec(width, col0):
        return pl.BlockSpec((ts, width), lambda b, c, pt: (rb + b, col0 // width))

    def page_spec(i):
        return pl.BlockSpec((1, page * 2 * g, hd), lambda b, c, pt: (pt[b, c * pp + i], 0, 0))

    in_specs = [row_spec(nh * hd, cols["q"]), row_spec(g * hd, cols["k"]), row_spec(g * hd, cols["v"]),
                pl.BlockSpec((None, g, r * ts, 1), lambda b, c, pt: (b, 0, 0, 0)),
                pl.BlockSpec((None, nh, pp * page), lambda b, c, pt: (b, 0, c)),
                pl.BlockSpec((None, nh, page), lambda b, c, pt: (b, 0, 0))]
    in_specs += [page_spec(i) for i in range(pp)]
    grid_spec = pltpu.PrefetchScalarGridSpec(
        num_scalar_prefetch=1, grid=(bs, n_chunks), in_specs=in_specs,
        out_specs=pl.BlockSpec((ts, nh * hd), lambda b, c, pt: (b, 0)),
        scratch_shapes=[pltpu.VMEM((g, r * ts, hd), BF16), pltpu.VMEM((g, r * ts, 1), F32),
                        pltpu.VMEM((g, r * ts, 1), F32), pltpu.VMEM((g, r * ts, hd), F32)])
    return pl.pallas_call(
        body, grid_spec=grid_spec, out_shape=SDS((bs * ts, nh * hd), F32),
        compiler_params=_cparams(("parallel", "arbitrary")), name="fox_sample",
    )(page_table, hc, hc, hc, fq, fk, fn, *([pool] * pp))


def _ab_layout(nh_a, g, r, nh_r):
    hd = HEAD_DIM
    qa, kv = nh_a * hd, g * hd
    sizes = [("q", qa), ("kc", kv), ("vc", kv), ("ks", kv), ("vs", kv), ("kw", kv), ("vw", kv),
             ("gt_src", nh_a * 3), ("rq", nh_r * hd), ("rk", nh_r * hd), ("rv", nh_r * hd), ("rg", nh_r * hd)]
    src = {}
    pos = 0
    for name, w in sizes:
        src[name] = pos
        pos += w
    order = ["q", "rq", "rk", "rv", "rg", "kc", "vc", "ks", "vs", "kw", "vw"]
    widths = dict(sizes)
    idx = []
    cols = {}
    for name in order:
        cols[name] = len(idx)
        idx += list(range(src[name], src[name] + widths[name]))
    cols["gt"] = len(idx)
    for gg in range(g):
        blk = [-1] * LANES
        for c in range(3):
            for i in range(r):
                blk[c * r + i] = src["gt_src"] + (gg * r + i) * 3 + c
        idx += blk
    return np.array(idx, np.int32), cols


def _even_layer(x, n_p, b, t, bs, ts, p, cache_kv, cache_win, state, page_table):
    hd = HEAD_DIM
    d = x.shape[1]
    nh_a = d // (2 * hd)
    g = nh_a // 4
    r = nh_a // g
    nh_r = d // (2 * hd)
    idx, cols = _ab_layout(nh_a, g, r, nh_r)
    tn = 768
    ncol = -(-len(idx) // tn) * tn
    idx = np.concatenate([idx, np.full(ncol - len(idx), -1, np.int32)])
    w_in = jnp.where(idx[None, :] >= 0, jnp.take(p["w_in"], np.maximum(idx, 0), axis=1), 0.0).astype(BF16)
    hab, _ = mm_norm(x, p["norm"], w_in, tn)

    kv4 = 4 * g * hd
    c_rows = cols["kc"]
    c_win = cols["kw"]
    new_rows_p = hab[:n_p, c_rows:c_rows + kv4].reshape(b, t, 4, g, hd)
    new_rows_s = hab[n_p:, c_rows:c_rows + kv4].reshape(bs, ts, 4, g, hd)
    new_win_p = hab[:n_p, c_win:c_win + 2 * g * hd].reshape(b, t, 2, g, hd)
    new_win_s = hab[n_p:, c_win:c_win + 2 * g * hd].reshape(bs, ts, 2, g, hd)
    win_state_p = new_win_p[:, -min(WINDOW, t):]
    wb = cache_win.shape[1]
    win_state_s = jnp.concatenate([cache_win, new_win_s], axis=1)[:, -wb:]

    slopes = jnp.exp2(-8.0 * (jnp.arange(nh_a, dtype=F32) + 1.0) / nh_a).reshape(g, r)
    w1k, w2k = p["w1_k"].astype(BF16), p["w2_k"].astype(BF16)
    w1v, w2v = p["w1_v"].astype(BF16), p["w2_v"].astype(BF16)

    def cmp_pair(blocks):
        bb, nb = blocks.shape[:2]
        flat = blocks.transpose(3, 0, 1, 4, 2, 5).reshape(2, bb * nb * g, CMP_BLOCK * hd)
        kc = compress(flat[0], p["pe_k"].reshape(-1), w1k, w2k).reshape(bb, nb, g, hd).transpose(0, 2, 1, 3)
        vc = compress(flat[1], p["pe_v"].reshape(-1), w1v, w2v).reshape(bb, nb, g, hd).transpose(0, 2, 1, 3)
        return kc, vc

    nb_p = t // CMP_BLOCK
    kc_p, vc_p = cmp_pair(new_rows_p[:, :nb_p * CMP_BLOCK, 0:2].reshape(b, nb_p, CMP_BLOCK, 2, g, hd))
    o_a_p = nsa_prompt(hab, kc_p, vc_p, slopes, cols, b=b, t=t, g=g, r=r)

    page = cache_kv.shape[1]
    n_pages = page_table.shape[1]
    nb_s = (n_pages * page + ts) // CMP_BLOCK
    assert nb_s * CMP_BLOCK == n_pages * page and page % CMP_BLOCK == 0
    pool = cache_kv.reshape(cache_kv.shape[0], page * 4 * g, hd)
    kc_s, vc_s = compress_sample(pool, page_table, p, page=page, g=g)
    o_a_s = nsa_sample(hab, kc_s, vc_s, pool, cache_win.reshape(bs, wb * 2 * g, hd), page_table, slopes, cols,
                       n_p=n_p, bs=bs, ts=ts, page=page, g=g, r=r)

    o_b_p, st_p = retention(hab, p["gn"], jnp.zeros((b, nh_r, hd, hd), F32), cols, row0=0, b=b, t=t, nh=nh_r)
    o_b_s, st_s = retention(hab, p["gn"], state, cols, row0=n_p, b=bs, t=ts, nh=nh_r)

    o = jnp.concatenate([jnp.concatenate([o_a_p, o_b_p.astype(BF16)], axis=1),
                         jnp.concatenate([o_a_s.astype(BF16), o_b_s.astype(BF16)], axis=1)], axis=0)
    x = mm_res(o, p["w_out"].astype(BF16), x)
    return x, (new_rows_p, win_state_p, st_p, new_rows_s, win_state_s, st_s)


def _odd_layer(x, n_p, b, t, bs, ts, p, cache_kv, cache_logf, page_table):
    hd = HEAD_DIM
    d = x.shape[1]
    nh = d // hd
    g = nh // 4
    r = nh // g
    cols = {"q": 0, "k": nh * hd, "v": (nh + g) * hd, "f": (nh + 2 * g) * hd}
    ncol_src = p["w_in"].shape[1]
    tn = 640
    ncol = -(-(cols["f"] + LANES) // tn) * tn
    w_in = jnp.pad(p["w_in"], ((0, 0), (0, ncol - ncol_src))).astype(BF16)
    hc, _ = mm_norm(x, p["norm"], w_in, tn)

    new_rows_p = hc[:n_p, cols["k"]:cols["f"]].reshape(b, t, 2, g, hd)
    new_rows_s = hc[n_p:, cols["k"]:cols["f"]].reshape(bs, ts, 2, g, hd)
    b_pad = jnp.pad(p["b_f"].astype(F32), (0, LANES - nh)).reshape(1, LANES)
    logf = fox_logf(hc, b_pad, cols["f"])
    new_logf_p = logf[:n_p, :nh].reshape(b, t, nh)
    new_logf_s = logf[n_p:, :nh].reshape(bs, ts, nh)

    f_p, *f_split = [a[:, :nh].reshape(b, t, g, r) for a in cumsum_rows(logf, b=b, t=t)]
    o_p = fox_prompt(hc, f_p, f_split, cols, b=b, t=t, g=g, r=r)

    page = cache_kv.shape[1]
    new_t = jnp.pad(new_logf_s.transpose(0, 2, 1), ((0, 0), (0, 0), (0, page - ts)))
    fk, fn = fox_f_sample(cache_logf.transpose(0, 2, 1), new_t, page_table)
    fq = fn[:, :, :ts].reshape(bs, g, r * ts, 1)
    o_s = fox_sample(hc, fq, fk, fn, cache_kv.reshape(cache_kv.shape[0], page * 2 * g, hd), page_table, cols,
                     n_p=n_p, bs=bs, ts=ts, page=page, g=g, r=r)

    x = mm_res(jnp.concatenate([o_p, o_s.astype(BF16)], axis=0), p["w_out"].astype(BF16), x)
    return x, (new_rows_p, new_logf_p, new_rows_s, new_logf_s)


def _peer_layer(x, nw, wq, k1, k2, u, v):
    q, xn = mm_norm(x, nw, wq.astype(BF16), _pick(wq.shape[1], (1024, 512, 256, 128)))
    s1t, s2t, st = peer_score(q, k1, k2)
    return peer_dense(xn, u.astype(BF16), v.T.astype(BF16), s1t, s2t, st, x)


def kernel(x_prompt, x_sample, cache_nsa_kv, cache_nsa_win, state_ret, cache_fox_kv, cache_fox_logf,
           page_table, norm_mix, norm_ffn, norm_final, w_in_ab, w_out_ab, cmp_pe_k, cmp_w1_k, cmp_w2_k,
           cmp_pe_v, cmp_w1_v, cmp_w2_v, ret_gn, w_in_c, b_forget, w_out_c, peer_wq, peer_k1, peer_k2,
           peer_u, peer_v):
    b, t, d = x_prompt.shape
    bs, ts, _ = x_sample.shape
    n_p, n_s = b * t, bs * ts
    depth = norm_mix.shape[0]
    x = jnp.concatenate([x_prompt.reshape(n_p, d), x_sample.reshape(n_s, d)], axis=0)
    even, odd = [], []
    for l in range(depth):
        if l % 2 == 0:
            e = l // 2
            p = dict(norm=norm_mix[l], w_in=w_in_ab[e], w_out=w_out_ab[e], pe_k=cmp_pe_k[e], w1_k=cmp_w1_k[e],
                     w2_k=cmp_w2_k[e], pe_v=cmp_pe_v[e], w1_v=cmp_w1_v[e], w2_v=cmp_w2_v[e], gn=ret_gn[e])
            x, outs = _even_layer(x, n_p, b, t, bs, ts, p, cache_nsa_kv[e], cache_nsa_win[e], state_ret[e],
                                  page_table)
            even.append(outs)
        else:
            o = l // 2
            p = dict(norm=norm_mix[l], w_in=w_in_c[o], b_f=b_forget[o], w_out=w_out_c[o])
            x, outs = _odd_layer(x, n_p, b, t, bs, ts, p, cache_fox_kv[o], cache_fox_logf[o], page_table)
            odd.append(outs)
        x = _peer_layer(x, norm_ffn[l], peer_wq[l], peer_k1[l], peer_k2[l], peer_u[l], peer_v[l])
    y = rms_final(x, norm_final)
    y_prompt = y[:n_p].reshape(b, t, d)
    y_sample = y[n_p:].reshape(bs, ts, d)

    def stack(group, i):
        return jnp.stack([o[i] for o in group])

    return (y_prompt, y_sample, stack(even, 0), stack(even, 1), stack(even, 2), stack(odd, 0), stack(odd, 1),
            stack(even, 3), stack(even, 4), stack(even, 5), stack(odd, 2), stack(odd, 3))
```

```python
import functools
import math

import numpy as np
import jax
import jax.numpy as jnp
from jax import lax
from jax.experimental import pallas as pl
from jax.experimental.pallas import tpu as pltpu

F32 = jnp.float32
BF16 = jnp.bfloat16
I32 = jnp.int32

HEAD_DIM = 128
CMP_BLOCK = 64
SEL_BLOCK = 64
SEL_TOPK = 16
WINDOW = 512
CMP_HIDDEN = 256
RET_CHUNK = 128
Q_BLOCK = 128
PEER_HEADS = 8
PEER_DK = 256
PEER_TOPK = 16
EPS = 1e-6
GN_EPS = 1e-5
NEG = -1e30

LANES = 128
SUBLANES = 8
VMEM_LIMIT = 56 * 1024 * 1024

SDS = jax.ShapeDtypeStruct


def _cparams(sem):
    return pltpu.CompilerParams(dimension_semantics=sem, vmem_limit_bytes=VMEM_LIMIT)


def _dot(a, b):
    return jnp.dot(a, b, preferred_element_type=F32)


def _dot_nt(a, b):
    return lax.dot_general(a, b, (((1,), (1,)), ((), ())), preferred_element_type=F32)


def _dot_tn(a, b):
    return lax.dot_general(a, b, (((0,), (0,)), ((), ())), preferred_element_type=F32)


def _pick(n, cands):
    for c in cands:
        if c <= n and n % c == 0:
            return c
    raise ValueError(f"no tile for {n} in {cands}")


def _gelu(x):
    c = math.sqrt(2.0 / math.pi)
    return x * (0.5 * (1.0 + jnp.tanh(c * (x + 0.044715 * (x * x * x)))))


def _split3(x):
    hi = x.astype(BF16)
    r1 = x - hi.astype(F32)
    mid = r1.astype(BF16)
    lo = (r1 - mid.astype(F32)).astype(BF16)
    return hi, mid, lo


def _stack_heads(x, nh, hd):
    return jnp.concatenate([x[:, r * hd:(r + 1) * hd] for r in range(nh)], axis=0)


def _unstack_heads(x, nh, t):
    return jnp.concatenate([x[r * t:(r + 1) * t, :] for r in range(nh)], axis=1)


def _rep_rows(x, k):
    return jnp.concatenate([x] * k, axis=0)


def _mm_norm_body(x_ref, nw_ref, w_ref, o_ref, xn_ref):
    @pl.when(pl.program_id(1) == 0)
    def _():
        x = x_ref[...]
        ms = jnp.mean(x * x, axis=-1, keepdims=True)
        xn_ref[...] = (x * lax.rsqrt(ms + EPS) * nw_ref[...]).astype(BF16)

    o_ref[...] = _dot(xn_ref[...], w_ref[...])


def mm_norm(x, nw, w_bf, tn):
    n, d = x.shape
    nn = w_bf.shape[1]
    tm = _pick(n, (768, 512, 256, 128, 64, 32, 16))
    return pl.pallas_call(
        _mm_norm_body,
        grid=(n // tm, nn // tn),
        in_specs=[pl.BlockSpec((tm, d), lambda i, j: (i, 0)),
                  pl.BlockSpec((1, d), lambda i, j: (0, 0)),
                  pl.BlockSpec((d, tn), lambda i, j: (0, j))],
        out_specs=[pl.BlockSpec((tm, tn), lambda i, j: (i, j)),
                   pl.BlockSpec((tm, d), lambda i, j: (i, 0))],
        out_shape=[SDS((n, nn), F32), SDS((n, d), BF16)],
        compiler_params=_cparams(("parallel", "arbitrary")),
        name="mm_norm",
    )(x, nw.reshape(1, d), w_bf)


def _mm_res_body(a_ref, w_ref, r_ref, o_ref):
    o_ref[...] = r_ref[...] + _dot(a_ref[...], w_ref[...])


def mm_res(a_bf, w_bf, res):
    n, k = a_bf.shape
    nn = w_bf.shape[1]
    tm = _pick(n, (768, 512, 256, 128, 64, 32, 16))
    tn = _pick(nn, (1024, 512, 256, 128))
    return pl.pallas_call(
        _mm_res_body,
        grid=(n // tm, nn // tn),
        in_specs=[pl.BlockSpec((tm, k), lambda i, j: (i, 0)),
                  pl.BlockSpec((k, tn), lambda i, j: (0, j)),
                  pl.BlockSpec((tm, tn), lambda i, j: (i, j))],
        out_specs=pl.BlockSpec((tm, tn), lambda i, j: (i, j)),
        out_shape=SDS((n, nn), F32),
        compiler_params=_cparams(("parallel", "arbitrary")),
        name="mm_res",
    )(a_bf, w_bf, res)


def _rms_body(x_ref, nw_ref, o_ref):
    x = x_ref[...]
    ms = jnp.mean(x * x, axis=-1, keepdims=True)
    o_ref[...] = x * lax.rsqrt(ms + EPS) * nw_ref[...]


def rms_final(x, nw):
    n, d = x.shape
    tm = _pick(n, (768, 512, 256, 128, 64, 32, 16, 8))
    return pl.pallas_call(
        _rms_body,
        grid=(n // tm,),
        in_specs=[pl.BlockSpec((tm, d), lambda i: (i, 0)), pl.BlockSpec((1, d), lambda i: (0, 0))],
        out_specs=pl.BlockSpec((tm, d), lambda i: (i, 0)),
        out_shape=SDS((n, d), F32),
        compiler_params=_cparams(("parallel",)),
        name="rms_final",
    )(x, nw.reshape(1, d))


def _batcher_pairs(n):
    pairs = []
    p = 1
    while p < n:
        k = p
        while k >= 1:
            for j in range(k % p, n - k, 2 * k):
                for i in range(min(k, n - j - k)):
                    if (i + j) // (2 * p) == (i + j + k) // (2 * p):
                        pairs.append((i + j, i + j + k))
            k //= 2
        p *= 2
    return pairs


_SORT16 = _batcher_pairs(16)


def _sort16_desc(xs):
    xs = list(xs)
    for i, j in _SORT16:
        hi = jnp.maximum(xs[i], xs[j])
        lo = jnp.minimum(xs[i], xs[j])
        xs[i], xs[j] = hi, lo
    return xs


def _bitonic16_desc(c):
    c = list(c)
    for stride in (8, 4, 2, 1):
        for i in range(16):
            if i & stride == 0:
                hi = jnp.maximum(c[i], c[i + stride])
                lo = jnp.minimum(c[i], c[i + stride])
                c[i], c[i + stride] = hi, lo
    return c


def _merge16_desc(a, b):
    return _bitonic16_desc([jnp.maximum(a[i], b[15 - i]) for i in range(16)])


def _top16_sorted(s):
    cols = _sort16_desc([s[v * SUBLANES:(v + 1) * SUBLANES, :] for v in range(16)])
    for shift in (4, 2, 1):
        other = [pltpu.roll(x, shift, 0) for x in cols]
        cols = _merge16_desc(cols, other)
    return cols


def _peer_score_body(q_ref, k1_ref, k2_ref, s1_ref, s2_ref, st_ref):
    tm = q_ref.shape[0]
    half = PEER_DK // 2
    k1 = k1_ref[...].astype(BF16)
    k2 = k2_ref[...].astype(BF16)
    sub = lax.broadcasted_iota(I32, (SUBLANES, tm), 0)
    a_top = None
    b_top = None
    for h in range(PEER_HEADS):
        qh = q_ref[:, h * PEER_DK:(h + 1) * PEER_DK]
        s1 = _dot_nt(k1, qh[:, :half].astype(BF16))
        s2 = _dot_nt(k2, qh[:, half:].astype(BF16))
        s1_ref[h] = s1
        s2_ref[h] = s2
        a_h = _top16_sorted(s1)
        b_h = _top16_sorted(s2)
        if h == 0:
            a_top, b_top = a_h, b_h
        else:
            a_top = [jnp.where(sub == h, x, y) for x, y in zip(a_h, a_top)]
            b_top = [jnp.where(sub == h, x, y) for x, y in zip(b_h, b_top)]
    ninf = jnp.full((SUBLANES, tm), -jnp.inf, F32)
    row0 = [a_top[0] + b_top[b] for b in range(16)]
    col0 = [a_top[a] + b_top[0] for a in range(1, 16)] + [ninf]
    mid = ([a_top[1] + b_top[b] for b in range(1, 8)] + [a_top[a] + b_top[1] for a in range(2, 8)]
           + [a_top[2] + b_top[b] for b in range(2, 5)])
    mid = _sort16_desc(mid)
    v0 = a_top[3] + b_top[2]
    v1 = a_top[4] + b_top[2]
    v2 = a_top[3] + b_top[3]
    tail = [v0, jnp.maximum(v1, v2), jnp.minimum(v1, v2)] + [ninf] * 13
    top = _merge16_desc(_merge16_desc(_merge16_desc(row0, col0), mid), tail)
    z = jnp.zeros((SUBLANES, tm), F32)
    for i in range(16):
        z = z + jnp.exp(top[i] - top[0])
    st_ref[0] = top[15]
    st_ref[1] = a_top[0]
    st_ref[2] = b_top[0]
    st_ref[3] = z


def peer_score(q, k1, k2):
    n, d = q.shape
    nk = k1.shape[0]
    assert nk == 128 and d == PEER_HEADS * PEER_DK
    tm = _pick(n, (256, 128))
    return pl.pallas_call(
        _peer_score_body,
        grid=(n // tm,),
        in_specs=[pl.BlockSpec((tm, d), lambda i: (i, 0)),
                  pl.BlockSpec(k1.shape, lambda i: (0, 0)),
                  pl.BlockSpec(k2.shape, lambda i: (0, 0))],
        out_specs=[pl.BlockSpec((PEER_HEADS, nk, tm), lambda i: (0, 0, i)),
                   pl.BlockSpec((PEER_HEADS, nk, tm), lambda i: (0, 0, i)),
                   pl.BlockSpec((4, PEER_HEADS, tm), lambda i: (0, 0, i))],
        out_shape=[SDS((PEER_HEADS, nk, n), F32), SDS((PEER_HEADS, nk, n), F32),
                   SDS((4, PEER_HEADS, n), F32)],
        compiler_params=_cparams(("parallel",)),
        name="peer_score",
    )(q, k1, k2)


def _peer_dense_body(xn_ref, u_ref, vt_ref, s1_ref, s2_ref, st_ref, res_ref, o_ref,
                     acc_ref, e2_ref, h_ref, act_ref, *, nc, n_steps):
    c = pl.program_id(1)
    nk = s2_ref.shape[1]

    @pl.when(c == 0)
    def _():
        acc_ref[...] = jnp.zeros_like(acc_ref)
        for h in range(PEER_HEADS):
            e2_ref[h] = jnp.exp(s2_ref[h] - st_ref[2, h:h + 1, :])

    h_ref[...] = _dot_nt(u_ref[...], xn_ref[...])
    for s in range(nc):
        e1 = c * nc + s
        w = jnp.zeros((nk, xn_ref.shape[0]), F32)
        for h in range(PEER_HEADS):
            s1row = s1_ref[h, pl.ds(e1, 1), :]
            p1 = jnp.exp(s1row - st_ref[1, h:h + 1, :]) / st_ref[3, h:h + 1, :]
            a = s1row + s2_ref[h]
            w = w + jnp.where(a >= st_ref[0, h:h + 1, :], p1 * e2_ref[h], 0.0)
        act_ref[s * nk:(s + 1) * nk, :] = (_gelu(h_ref[s * nk:(s + 1) * nk, :]) * w).astype(BF16)
    acc_ref[...] += _dot(vt_ref[...], act_ref[...])

    @pl.when(c == n_steps - 1)
    def _():
        o_ref[...] = res_ref[...] + acc_ref[...].T


def peer_dense(xn_bf, u_bf, vt_bf, s1t, s2t, st, res):
    n, d = xn_bf.shape
    ne = u_bf.shape[0]
    nk = s1t.shape[1]
    tm = _pick(n, (768, 512, 256, 128))
    nc = 4
    te = nc * nk
    n_steps = ne // te
    body = functools.partial(_peer_dense_body, nc=nc, n_steps=n_steps)
    once = pl.Buffered(1)
    return pl.pallas_call(
        body,
        grid=(n // tm, n_steps),
        in_specs=[pl.BlockSpec((tm, d), lambda i, c: (i, 0), pipeline_mode=once),
                  pl.BlockSpec((te, d), lambda i, c: (c, 0)),
                  pl.BlockSpec((d, te), lambda i, c: (0, c)),
                  pl.BlockSpec((PEER_HEADS, nk, tm), lambda i, c: (0, 0, i), pipeline_mode=once),
                  pl.BlockSpec((PEER_HEADS, nk, tm), lambda i, c: (0, 0, i), pipeline_mode=once),
                  pl.BlockSpec((4, PEER_HEADS, tm), lambda i, c: (0, 0, i), pipeline_mode=once),
                  pl.BlockSpec((tm, d), lambda i, c: (i, 0), pipeline_mode=once)],
        out_specs=pl.BlockSpec((tm, d), lambda i, c: (i, 0)),
        out_shape=SDS((n, d), F32),
        scratch_shapes=[pltpu.VMEM((d, tm), F32), pltpu.VMEM((PEER_HEADS, nk, tm), F32),
                        pltpu.VMEM((te, tm), F32), pltpu.VMEM((te, tm), BF16)],
        compiler_params=_cparams(("parallel", "arbitrary")),
        name="peer_dense",
    )(xn_bf, u_bf, vt_bf, s1t, s2t, st, res)


def _flash_init(m_ref, l_ref, acc_ref):
    m_ref[...] = jnp.full(m_ref.shape, NEG, F32)
    l_ref[...] = jnp.zeros(l_ref.shape, F32)
    acc_ref[...] = jnp.zeros(acc_ref.shape, F32)


def _flash_update(s, mask, v_tiles, m_ref, l_ref, acc_ref):
    if mask is not None:
        s = jnp.where(mask, s, NEG)
    m_prev = m_ref[...]
    m_new = jnp.maximum(m_prev, jnp.max(s, axis=-1, keepdims=True))
    alpha = jnp.exp(m_prev - m_new)
    p = jnp.exp(s - m_new)
    if mask is not None:
        p = jnp.where(mask, p, 0.0)
    l_ref[...] = alpha * l_ref[...] + jnp.sum(p, axis=-1, keepdims=True)
    pb = p.astype(BF16)
    pv = None
    off = 0
    for v in v_tiles:
        part = _dot(pb[:, off:off + v.shape[0]], v)
        pv = part if pv is None else pv + part
        off += v.shape[0]
    acc_ref[...] = alpha * acc_ref[...] + pv
    m_ref[...] = m_new


def _flash_final(l_ref, acc_ref):
    return acc_ref[...] / jnp.maximum(l_ref[...], 1e-30)


def _softmax_masked(s, mask):
    s = jnp.where(mask, s, NEG)
    m = jnp.max(s, axis=-1, keepdims=True)
    e = jnp.where(mask, jnp.exp(s - m), 0.0)
    return e / jnp.maximum(jnp.sum(e, axis=-1, keepdims=True), 1e-30)


def _topk_mask(score, k):
    n = score.shape[-1]
    lane = lax.broadcasted_iota(I32, score.shape, 1).astype(F32)
    sel = jnp.zeros(score.shape, F32)
    for _ in range(k):
        m = jnp.max(score, axis=-1, keepdims=True)
        idx = jnp.min(jnp.where(score == m, lane, float(n)), axis=-1, keepdims=True)
        hit = lane == idx
        sel = jnp.where(hit, jnp.where(m >= 0.0, 1.0, 0.0), sel)
        score = jnp.where(hit, -jnp.inf, score)
    return sel


def _compress_body(x_ref, pe_ref, w1_ref, w2_ref, o_ref):
    xb = (x_ref[...] + pe_ref[...]).astype(BF16)
    h = _gelu(_dot(xb, w1_ref[...]))
    o_ref[...] = _dot(h.astype(BF16), w2_ref[...])


def compress(x, pe, w1_bf, w2_bf):
    rows, k = x.shape
    tr = _pick(rows, (256, 128, 64, 32, 16, 8))
    hid = w1_bf.shape[1]
    hd = w2_bf.shape[1]
    return pl.pallas_call(
        _compress_body,
        grid=(rows // tr,),
        in_specs=[pl.BlockSpec((tr, k), lambda i: (i, 0)),
                  pl.BlockSpec((1, k), lambda i: (0, 0)),
                  pl.BlockSpec((k, hid), lambda i: (0, 0)),
                  pl.BlockSpec((hid, hd), lambda i: (0, 0))],
        out_specs=pl.BlockSpec((tr, hd), lambda i: (i, 0)),
        out_shape=SDS((rows, hd), F32),
        compiler_params=_cparams(("parallel",)),
        name="nsa_compress",
    )(x, pe.reshape(1, k), w1_bf, w2_bf)


def _nsa_cmp_and_select(qs, slope, t1, kc_ref, vc_ref, *, r, nb, n_sel, ns_pad, k_top):
    tq = t1.shape[0]
    t = _rep_rows(t1, r)
    blk_end = lax.broadcasted_iota(I32, (1, nb), 1) * CMP_BLOCK + (CMP_BLOCK - 1)
    d_c = t - blk_end
    s = _dot_nt(qs, kc_ref[...].astype(BF16)) - slope * d_c.astype(F32)
    p_c = _softmax_masked(s, d_c >= 0)
    o_c = _dot(p_c.astype(BF16), vc_ref[...].astype(BF16))
    imp = p_c[0:tq]
    for i in range(1, r):
        imp = imp + p_c[i * tq:(i + 1) * tq]
    if ns_pad > nb:
        imp = jnp.concatenate([imp, jnp.zeros((tq, ns_pad - nb), F32)], axis=1)
    jsel = lax.broadcasted_iota(I32, (1, ns_pad), 1)
    cur = jnp.right_shift(t1, 6)
    forced = (jsel == 0) | (jsel == cur) | (jsel == cur - 1)
    score = jnp.where(forced, r + 1.0, jnp.where(jsel <= cur, imp, -1.0))
    score = jnp.where(jsel < n_sel, score, -2.0)
    return o_c, _topk_mask(score, k_top)


def _gate_cols(gsig, c, r):
    return jnp.concatenate([gsig[:, c * r + i:c * r + i + 1] for i in range(r)], axis=0)


def _flash_update_t(s, mask, v, m_ref, l_ref, acc_ref):
    if mask is not None:
        s = jnp.where(mask, s, NEG)
    m_prev = m_ref[...]
    m_new = jnp.maximum(m_prev, jnp.max(s, axis=0, keepdims=True))
    alpha = jnp.exp(m_prev - m_new)
    p = jnp.exp(s - m_new)
    if mask is not None:
        p = jnp.where(mask, p, 0.0)
    l_ref[...] = alpha * l_ref[...] + jnp.sum(p, axis=0, keepdims=True)
    acc_ref[...] = alpha * acc_ref[...] + _dot_tn(v, p.astype(BF16))
    m_ref[...] = m_new


def _nsa_p_body(q_ref, gt_ref, kc_ref, vc_ref, ks_ref, vs_ref, kw_ref, vw_ref, sl_ref, kp_ref, o_ref,
                m_ref, l_ref, acc_ref, *, tq, tk, wk, nb, n_sel, k_top, scale, r):
    qi = pl.program_id(2)
    a = qi * tq
    rows = r * tq
    qs = (_stack_heads(q_ref[...], r, HEAD_DIM) * scale).astype(BF16)
    slope = sl_ref[...]
    t1 = a + lax.broadcasted_iota(I32, (tq, 1), 0)
    t = _rep_rows(t1, r)
    o_c, sel = _nsa_cmp_and_select(qs, slope, t1, kc_ref, vc_ref, r=r, nb=nb, n_sel=n_sel,
                                   ns_pad=n_sel, k_top=k_top)
    sel_bf = sel.astype(BF16)
    lane = lax.broadcasted_iota(I32, (rows, LANES), 1)
    t_hi = jnp.right_shift(t, 6).astype(F32)
    t_lo = jnp.bitwise_and(t, SEL_BLOCK - 1).astype(F32)
    qb = jnp.where(lane == 0, slope * 64.0,
                   jnp.where(lane == 1, slope,
                             jnp.where(lane == 2, -(slope * 64.0) * t_hi,
                                       jnp.where(lane == 3, -slope * t_lo, 0.0)))).astype(BF16)
    qa = jnp.concatenate([qs, qb], axis=1)
    t_row = a + jnp.concatenate([lax.broadcasted_iota(I32, (1, tq), 1)] * r, axis=1)

    def attend(k_ref_, v_ref_, start, width, window):
        ka = jnp.concatenate([k_ref_[pl.ds(start, width), :].astype(BF16), kp_ref[pl.ds(start, width), :]], axis=1)
        v = v_ref_[pl.ds(start, width), :].astype(BF16)
        s = _dot_nt(ka, qa)
        pos = start + lax.broadcasted_iota(I32, (width, 1), 0)
        mask = pos <= t_row
        if window:
            mask = mask & (pos >= t_row - WINDOW)
        else:
            blk = lax.broadcasted_iota(I32, (width, n_sel), 1)
            key_blk = jnp.right_shift(start + lax.broadcasted_iota(I32, (width, n_sel), 0), 6)
            bm = _dot_nt(jnp.where(blk == key_blk, 1.0, 0.0).astype(BF16), sel_bf)
            mask = mask & (jnp.concatenate([bm] * r, axis=1) > 0.5)
        _flash_update_t(s, mask, v, m_ref, l_ref, acc_ref)

    def sel_step(j, carry):
        attend(ks_ref, vs_ref, pl.multiple_of(j * tk, tk), tk, False)
        return carry

    _flash_init(m_ref, l_ref, acc_ref)
    lax.fori_loop(0, (a + tq - 1) // tk + 1, sel_step, 0)
    o_s = (acc_ref[...] / jnp.maximum(l_ref[...], 1e-30)).T

    _flash_init(m_ref, l_ref, acc_ref)
    attend(kw_ref, vw_ref, pl.multiple_of(jnp.maximum(a + tq - wk, 0), SUBLANES), wk, True)
    o_w = (acc_ref[...] / jnp.maximum(l_ref[...], 1e-30)).T

    gsig = 1.0 / (1.0 + jnp.exp(-gt_ref[...]))
    o = _gate_cols(gsig, 0, r) * o_c + _gate_cols(gsig, 1, r) * o_s + _gate_cols(gsig, 2, r) * o_w
    o_ref[...] = _unstack_heads(o, r, tq).astype(BF16)


def nsa_prompt(hab, kc, vc, slopes, cols, *, b, t, g, r):
    hd = HEAD_DIM
    nb = kc.shape[2]
    n_sel = -(-t // SEL_BLOCK)
    assert n_sel == nb and t % SEL_BLOCK == 0
    tq = Q_BLOCK if t % Q_BLOCK == 0 else t
    tk = _pick(t, (512, 256, 128, t))
    wk = min(WINDOW + tq, t)
    nq = t // tq
    qw = r * hd
    body = functools.partial(_nsa_p_body, tq=tq, tk=tk, wk=wk, nb=nb, n_sel=n_sel, k_top=min(SEL_TOPK, n_sel),
                             scale=hd ** -0.5, r=r)
    assert t <= 256 * SEL_BLOCK
    pos = np.arange(t)
    kp = np.zeros((t, LANES), np.float32)
    kp[:, 0], kp[:, 1], kp[:, 2], kp[:, 3] = pos >> 6, pos & (SEL_BLOCK - 1), 1.0, 1.0
    kpos = jnp.asarray(kp, BF16)

    def seq_spec(col0):
        return pl.BlockSpec((t, hd), lambda bb, gg, qi: (bb, col0 // hd + gg))

    return pl.pallas_call(
        body,
        grid=(b, g, nq),
        in_specs=[pl.BlockSpec((tq, qw), lambda bb, gg, qi: (bb * nq + qi, gg)),
                  pl.BlockSpec((tq, LANES), lambda bb, gg, qi: (bb * nq + qi, cols["gt"] // LANES + gg)),
                  pl.BlockSpec((None, None, nb, hd), lambda bb, gg, qi: (bb, gg, 0, 0)),
                  pl.BlockSpec((None, None, nb, hd), lambda bb, gg, qi: (bb, gg, 0, 0)),
                  seq_spec(cols["ks"]), seq_spec(cols["vs"]), seq_spec(cols["kw"]), seq_spec(cols["vw"]),
                  pl.BlockSpec((None, r * tq, 1), lambda bb, gg, qi: (gg, 0, 0)),
                  pl.BlockSpec((t, LANES), lambda bb, gg, qi: (0, 0))],
        out_specs=pl.BlockSpec((tq, qw), lambda bb, gg, qi: (bb * nq + qi, gg)),
        out_shape=SDS((b * t, g * qw), BF16),
        scratch_shapes=[pltpu.VMEM((1, r * tq), F32), pltpu.VMEM((1, r * tq), F32),
                        pltpu.VMEM((hd, r * tq), F32)],
        compiler_params=_cparams(("parallel", "parallel", "arbitrary")),
        name="nsa_prompt",
    )(hab, hab, kc, vc, hab, hab, hab, hab, jnp.repeat(slopes, tq, axis=1).reshape(g, r * tq, 1), kpos)


def _pad_rows(x, rows):
    return jnp.concatenate([x, jnp.zeros((rows - x.shape[0], x.shape[1]), x.dtype)], axis=0)


def _block_expand(ns_pad, first_pos, width):
    blk = lax.broadcasted_iota(I32, (ns_pad, width), 0)
    key_blk = jnp.right_shift(first_pos + lax.broadcasted_iota(I32, (ns_pad, width), 1), 6)
    return jnp.where(blk == key_blk, 1.0, 0.0).astype(BF16)


def _nsa_s_body(pt_ref, q_ref, gt_ref, ksn_ref, vsn_ref, kwn_ref, vwn_ref, kc_ref, vc_ref, win_ref,
                sl_ref, *rest, pp, n_chunks, ts, page, offset, wb, nb, n_sel, ns_pad, k_top, scale, g, r):
    pages = rest[:pp]
    o_ref = rest[pp]
    qs_ref, oc_ref, sel_ref, m_ref, l_ref, acc_ref = rest[pp + 1:]
    del pt_ref
    c = pl.program_id(1)
    hd = HEAD_DIM
    t1 = offset + lax.broadcasted_iota(I32, (ts, 1), 0)
    t = _rep_rows(t1, r)
    lane = lax.broadcasted_iota(I32, (1, page), 1)

    @pl.when(c == 0)
    def _():
        for gg in range(g):
            qs = (_stack_heads(q_ref[:, gg * r * hd:(gg + 1) * r * hd], r, hd) * scale).astype(BF16)
            qs_ref[gg] = qs
            o_c, sel = _nsa_cmp_and_select(qs, sl_ref[gg], t1, kc_ref.at[gg], vc_ref.at[gg], r=r, nb=nb,
                                           n_sel=n_sel, ns_pad=ns_pad, k_top=k_top)
            oc_ref[gg] = o_c
            sel_ref[gg] = _rep_rows(sel, r).astype(BF16)
            _flash_init(m_ref.at[gg], l_ref.at[gg], acc_ref.at[gg])

    width = pp * page
    first = c * width
    d = t - (first + lax.broadcasted_iota(I32, (1, width), 1))
    expand = _block_expand(ns_pad, first, width)
    for gg in range(g):
        ks = [pages[i][0, pl.ds(2 * g + gg, page, stride=4 * g), :].astype(BF16) for i in range(pp)]
        vs = [pages[i][0, pl.ds(3 * g + gg, page, stride=4 * g), :].astype(BF16) for i in range(pp)]
        s = jnp.concatenate([_dot_nt(qs_ref[gg], k) for k in ks], axis=1) - sl_ref[gg] * d.astype(F32)
        bm = _dot(sel_ref[gg], expand)
        _flash_update(s, (bm > 0.5) & (d >= 0), vs, m_ref.at[gg], l_ref.at[gg], acc_ref.at[gg])

    @pl.when(c == n_chunks - 1)
    def _():
        gsig = 1.0 / (1.0 + jnp.exp(-gt_ref[...]))
        dn = t - (offset + lane)
        expand_n = _block_expand(ns_pad, offset, page)
        outs = []
        for gg in range(g):
            k = _pad_rows(ksn_ref[:, gg * hd:(gg + 1) * hd], page).astype(BF16)
            v = _pad_rows(vsn_ref[:, gg * hd:(gg + 1) * hd], page).astype(BF16)
            s = _dot_nt(qs_ref[gg], k) - sl_ref[gg] * dn.astype(F32)
            bm = _dot(sel_ref[gg], expand_n)
            _flash_update(s, (bm > 0.5) & (dn >= 0) & (lane < ts), [v], m_ref.at[gg], l_ref.at[gg],
                          acc_ref.at[gg])
            o_s = _flash_final(l_ref.at[gg], acc_ref.at[gg])
            _flash_init(m_ref.at[gg], l_ref.at[gg], acc_ref.at[gg])
            kw = win_ref[pl.ds(gg, wb, stride=2 * g), :].astype(BF16)
            vw = win_ref[pl.ds(g + gg, wb, stride=2 * g), :].astype(BF16)
            kn = _pad_rows(kwn_ref[:, gg * hd:(gg + 1) * hd], page).astype(BF16)
            vn = _pad_rows(vwn_ref[:, gg * hd:(gg + 1) * hd], page).astype(BF16)
            dw = t - (offset - wb + lax.broadcasted_iota(I32, (1, wb), 1))
            s = jnp.concatenate([_dot_nt(qs_ref[gg], kw) - sl_ref[gg] * dw.astype(F32),
                                 _dot_nt(qs_ref[gg], kn) - sl_ref[gg] * dn.astype(F32)], axis=1)
            mask = jnp.concatenate([(dw >= 0) & (dw <= WINDOW), (dn >= 0) & (dn <= WINDOW) & (lane < ts)], axis=1)
            _flash_update(s, mask, [vw, vn], m_ref.at[gg], l_ref.at[gg], acc_ref.at[gg])
            o_w = _flash_final(l_ref.at[gg], acc_ref.at[gg])
            gs = gsig[:, gg * LANES:(gg + 1) * LANES]
            o = _gate_cols(gs, 0, r) * oc_ref[gg] + _gate_cols(gs, 1, r) * o_s + _gate_cols(gs, 2, r) * o_w
            outs.append(_unstack_heads(o, r, ts))
        o_ref[...] = jnp.concatenate(outs, axis=1)


def nsa_sample(hab, kc, vc, pool, win, page_table, slopes, cols, *, n_p, bs, ts, page, g, r):
    hd = HEAD_DIM
    n_pages = page_table.shape[1]
    offset = n_pages * page
    wb = win.shape[1] // (2 * g)
    assert offset % SEL_BLOCK == 0 and offset - wb >= 0 and wb == WINDOW
    nb = kc.shape[2]
    n_sel = -(-(offset + ts) // SEL_BLOCK)
    ns_pad = -(-n_sel // LANES) * LANES
    pp = _pick(n_pages, (16, 8, 4, 2, 1))
    n_chunks = n_pages // pp
    rb = n_p // ts
    body = functools.partial(_nsa_s_body, pp=pp, n_chunks=n_chunks, ts=ts, page=page, offset=offset, wb=wb,
                             nb=nb, n_sel=n_sel, ns_pad=ns_pad, k_top=min(SEL_TOPK, n_sel),
                             scale=hd ** -0.5, g=g, r=r)

    def row_spec(width, col0):
        return pl.BlockSpec((ts, width), lambda b, c, pt: (rb + b, col0 // width))

    def page_spec(i):
        return pl.BlockSpec((1, page * 4 * g, hd), lambda b, c, pt: (pt[b, c * pp + i], 0, 0))

    in_specs = [row_spec(g * r * hd, cols["q"]), row_spec(g * LANES, cols["gt"]),
                row_spec(g * hd, cols["ks"]), row_spec(g * hd, cols["vs"]),
                row_spec(g * hd, cols["kw"]), row_spec(g * hd, cols["vw"]),
                pl.BlockSpec((None, g, nb, hd), lambda b, c, pt: (b, 0, 0, 0)),
                pl.BlockSpec((None, g, nb, hd), lambda b, c, pt: (b, 0, 0, 0)),
                pl.BlockSpec((None, wb * 2 * g, hd), lambda b, c, pt: (b, 0, 0)),
                pl.BlockSpec((g, r * ts, 1), lambda b, c, pt: (0, 0, 0))]
    in_specs += [page_spec(i) for i in range(pp)]
    grid_spec = pltpu.PrefetchScalarGridSpec(
        num_scalar_prefetch=1, grid=(bs, n_chunks), in_specs=in_specs,
        out_specs=pl.BlockSpec((ts, g * r * hd), lambda b, c, pt: (b, 0)),
        scratch_shapes=[pltpu.VMEM((g, r * ts, hd), BF16), pltpu.VMEM((g, r * ts, hd), F32),
                        pltpu.VMEM((g, r * ts, ns_pad), BF16), pltpu.VMEM((g, r * ts, 1), F32),
                        pltpu.VMEM((g, r * ts, 1), F32), pltpu.VMEM((g, r * ts, hd), F32)])
    return pl.pallas_call(
        body, grid_spec=grid_spec, out_shape=SDS((bs * ts, g * r * hd), F32),
        compiler_params=_cparams(("parallel", "arbitrary")), name="nsa_sample",
    )(page_table, hab, hab, hab, hab, hab, hab, kc, vc, win,
      jnp.repeat(slopes, ts, axis=1).reshape(g, r * ts, 1), *([pool] * pp))


def _cmp_s_body(pt_ref, pek_ref, w1k_ref, w2k_ref, pev_ref, w1v_ref, w2v_ref, *rest, pp, page, g):
    pages = rest[:pp]
    kc_ref, vc_ref, stage_ref, x_ref = rest[pp:]
    del pt_ref
    hd = HEAD_DIM
    rows = pp * (page // CMP_BLOCK)
    for cg in range(2 * g):
        for i in range(pp):
            stage_ref[cg, i * page:(i + 1) * page, :] = pages[i][0, pl.ds(cg, page, stride=4 * g), :]
    for comp, (pe_ref, w1_ref, w2_ref, out_ref) in enumerate(
            ((pek_ref, w1k_ref, w2k_ref, kc_ref), (pev_ref, w1v_ref, w2v_ref, vc_ref))):
        for l in range(CMP_BLOCK):
            x = jnp.concatenate([stage_ref[comp * g + gg, pl.ds(l, rows, stride=CMP_BLOCK), :]
                                 for gg in range(g)], axis=0)
            x_ref[:, l * hd:(l + 1) * hd] = (x + pe_ref[l:l + 1, :]).astype(BF16)
        out = _dot(_gelu(_dot(x_ref[...], w1_ref[...])).astype(BF16), w2_ref[...])
        for gg in range(g):
            out_ref[gg] = out[gg * rows:(gg + 1) * rows]


def compress_sample(pool, page_table, p, *, page, g):
    hd = HEAD_DIM
    bs, n_pages = page_table.shape
    pp = _pick(n_pages, (16, 8, 4))
    rows = pp * (page // CMP_BLOCK)
    nb = n_pages * (page // CMP_BLOCK)
    hid = p["w1_k"].shape[1]
    body = functools.partial(_cmp_s_body, pp=pp, page=page, g=g)
    once = pl.Buffered(1)

    def wspecs():
        return [pl.BlockSpec((CMP_BLOCK, hd), lambda b, c, pt: (0, 0)),
                pl.BlockSpec((CMP_BLOCK * hd, hid), lambda b, c, pt: (0, 0), pipeline_mode=once),
                pl.BlockSpec((hid, hd), lambda b, c, pt: (0, 0))]

    def page_spec(i):
        return pl.BlockSpec((1, page * 4 * g, hd), lambda b, c, pt: (pt[b, c * pp + i], 0, 0))

    grid_spec = pltpu.PrefetchScalarGridSpec(
        num_scalar_prefetch=1, grid=(bs, n_pages // pp),
        in_specs=wspecs() + wspecs() + [page_spec(i) for i in range(pp)],
        out_specs=[pl.BlockSpec((None, g, rows, hd), lambda b, c, pt: (b, 0, c, 0)),
                   pl.BlockSpec((None, g, rows, hd), lambda b, c, pt: (b, 0, c, 0))],
        scratch_shapes=[pltpu.VMEM((2 * g, pp * page, hd), F32), pltpu.VMEM((g * rows, CMP_BLOCK * hd), BF16)])
    return pl.pallas_call(
        body, grid_spec=grid_spec, out_shape=[SDS((bs, g, nb, hd), F32), SDS((bs, g, nb, hd), F32)],
        compiler_params=_cparams(("parallel", "arbitrary")), name="nsa_compress_sample",
    )(page_table, p["pe_k"], p["w1_k"].astype(BF16), p["w2_k"].astype(BF16),
      p["pe_v"], p["w1_v"].astype(BF16), p["w2_v"].astype(BF16), *([pool] * pp))


def _ret_body(q_ref, k_ref, v_ref, g_ref, gn_ref, s0_ref, dm_ref, cr_ref, kd_ref, cd_ref, o_ref, s_ref,
              st_ref, *, scale, n_chunks, nh):
    c = pl.program_id(1)
    dk = HEAD_DIM

    @pl.when(c == 0)
    def _():
        st_ref[...] = s0_ref[...]

    for h in range(nh):
        cs = slice(h * dk, (h + 1) * dk)
        qb = q_ref[:, cs].astype(BF16)
        ks = k_ref[:, cs] * scale
        kb = ks.astype(BF16)
        vb = v_ref[:, cs].astype(BF16)
        st = st_ref[h]
        att = _dot_nt(qb, kb) * dm_ref[h]
        o = _dot(att.astype(BF16), vb) + _dot(qb, st.astype(BF16)) * cr_ref[h]
        st_ref[h] = st * cd_ref[h] + _dot_tn((ks * kd_ref[h]).astype(BF16), vb)
        mu = jnp.mean(o, axis=-1, keepdims=True)
        var = jnp.mean(jnp.square(o - mu), axis=-1, keepdims=True)
        on = (o - mu) * lax.rsqrt(var + GN_EPS) * gn_ref[:, cs]
        gate = g_ref[:, cs]
        o_ref[:, cs] = ((gate * (1.0 / (1.0 + jnp.exp(-gate)))) * on).astype(o_ref.dtype)

    @pl.when(c == n_chunks - 1)
    def _():
        s_ref[...] = st_ref[...]


def retention(hab, gn_w, s0, cols, *, row0, b, t, nh):
    dk = HEAD_DIM
    w = nh * dk
    ch = RET_CHUNK if t % RET_CHUNK == 0 else t
    n_chunks = t // ch
    lg = jnp.log1p(-jnp.exp2(-5.0 - jnp.arange(nh, dtype=F32)))
    i = jnp.arange(ch, dtype=F32)
    diff = i[:, None] - i[None, :]
    dmask = jnp.where(diff >= 0, jnp.exp(jnp.maximum(diff, 0.0)[None] * lg[:, None, None]), 0.0)
    cross = jnp.exp((i + 1.0)[None, :] * lg[:, None]).reshape(nh, ch, 1)
    kdec = jnp.exp((ch - 1.0 - i)[None, :] * lg[:, None]).reshape(nh, ch, 1)
    cdec = jnp.exp(ch * lg).reshape(nh, 1, 1)
    rb = row0 // ch
    body = functools.partial(_ret_body, scale=dk ** -0.5, n_chunks=n_chunks, nh=nh)

    def col_spec(col0):
        assert col0 % w == 0
        return pl.BlockSpec((ch, w), lambda bb, c: (rb + bb * n_chunks + c, col0 // w))

    def whole(shape):
        return pl.BlockSpec(shape, lambda bb, c: (0,) * len(shape))

    return pl.pallas_call(
        body,
        grid=(b, n_chunks),
        in_specs=[col_spec(cols["rq"]), col_spec(cols["rk"]), col_spec(cols["rv"]), col_spec(cols["rg"]),
                  whole((1, w)),
                  pl.BlockSpec((None, nh, dk, dk), lambda bb, c: (bb, 0, 0, 0)),
                  whole((nh, ch, ch)), whole((nh, ch, 1)), whole((nh, ch, 1)), whole((nh, 1, 1))],
        out_specs=[pl.BlockSpec((ch, w), lambda bb, c: (bb * n_chunks + c, 0)),
                   pl.BlockSpec((None, nh, dk, dk), lambda bb, c: (bb, 0, 0, 0))],
        out_shape=[SDS((b * t, w), BF16 if ch % 16 == 0 else F32), SDS((b, nh, dk, dk), F32)],
        scratch_shapes=[pltpu.VMEM((nh, dk, dk), F32)],
        compiler_params=_cparams(("parallel", "arbitrary")),
        name="retention",
    )(hab, hab, hab, hab, gn_w.reshape(1, w), s0, dmask, cross, kdec, cdec)


def _logf_body(x_ref, b_ref, o_ref):
    x = x_ref[...] + b_ref[...]
    o_ref[...] = -(jnp.maximum(-x, 0.0) + jnp.log1p(jnp.exp(-jnp.abs(x))))


def fox_logf(hc, b_pad, col0):
    n = hc.shape[0]
    tm = _pick(n, (768, 512, 256, 128, 64, 32, 16, 8))
    return pl.pallas_call(
        _logf_body,
        grid=(n // tm,),
        in_specs=[pl.BlockSpec((tm, LANES), lambda i: (i, col0 // LANES)),
                  pl.BlockSpec((1, LANES), lambda i: (0, 0))],
        out_specs=pl.BlockSpec((tm, LANES), lambda i: (i, 0)),
        out_shape=SDS((n, LANES), F32),
        compiler_params=_cparams(("parallel",)),
        name="fox_logf",
    )(hc, b_pad)


def _cumsum_rows_body(x_ref, o_ref, hi_ref, mid_ref, lo_ref, carry_ref):
    @pl.when(pl.program_id(1) == 0)
    def _():
        carry_ref[...] = jnp.zeros_like(carry_ref)

    tc = x_ref.shape[0]
    tri = jnp.where(lax.broadcasted_iota(I32, (tc, tc), 1) <= lax.broadcasted_iota(I32, (tc, tc), 0),
                    1.0, 0.0).astype(BF16)
    hi, mid, lo = _split3(x_ref[...])
    f = (_dot(tri, hi) + _dot(tri, mid)) + _dot(tri, lo) + carry_ref[...]
    o_ref[...] = f
    hi_ref[...], mid_ref[...], lo_ref[...] = _split3(f)
    carry_ref[...] = f[tc - 1:tc, :]


def cumsum_rows(x, *, b, t):
    tc = 128 if t % 128 == 0 else t
    nc = t // tc
    spec = pl.BlockSpec((tc, LANES), lambda bb, c: (bb * nc + c, 0))
    return pl.pallas_call(
        _cumsum_rows_body,
        grid=(b, nc),
        in_specs=[spec],
        out_specs=[spec, spec, spec, spec],
        out_shape=[SDS((b * t, LANES), F32)] + [SDS((b * t, LANES), BF16)] * 3,
        scratch_shapes=[pltpu.VMEM((1, LANES), F32)],
        compiler_params=_cparams(("parallel", "arbitrary")),
        name="fox_cumsum",
    )(x)


def _fox_p_body(q_ref, k_ref, v_ref, fq_ref, fa_ref, bs_ref, o_ref, m_ref, l_ref, acc_ref, *, tq, tk, scale, r):
    qi = pl.program_id(2)
    hd = HEAD_DIM
    rows = r * tq
    qs = (_stack_heads(q_ref[...], r, hd) * scale).astype(BF16)
    fq = jnp.concatenate([fq_ref[:, i:i + 1] for i in range(r)], axis=0)
    hi, mid, lo = _split3(fq)
    lane = lax.broadcasted_iota(I32, (rows, LANES), 1)
    qb = (bs_ref[...] + jnp.where(lane == 3 * r, hi.astype(F32), 0.0) + jnp.where(lane == 3 * r + 1, mid.astype(F32), 0.0)
          + jnp.where(lane == 3 * r + 2, lo.astype(F32), 0.0)).astype(BF16)
    qa = jnp.concatenate([qs, qb], axis=1)
    t_row = qi * tq + jnp.concatenate([lax.broadcasted_iota(I32, (1, tq), 1)] * r, axis=1)

    def tile(j, masked):
        start = pl.multiple_of(j * tk, tk)
        ka = jnp.concatenate([k_ref[pl.ds(start, tk), :].astype(BF16), fa_ref[pl.ds(start, tk), :]], axis=1)
        v = v_ref[pl.ds(start, tk), :].astype(BF16)
        s = _dot_nt(ka, qa)
        if masked:
            mask = start + lax.broadcasted_iota(I32, (tk, 1), 0) <= t_row
            s = jnp.where(mask, s, NEG)
        m_prev = m_ref[...]
        m_new = jnp.maximum(m_prev, jnp.max(s, axis=0, keepdims=True))
        alpha = jnp.exp(m_prev - m_new)
        p = jnp.exp(s - m_new)
        if masked:
            p = jnp.where(mask, p, 0.0)
        l_ref[...] = alpha * l_ref[...] + jnp.sum(p, axis=0, keepdims=True)
        acc_ref[...] = alpha * acc_ref[...] + _dot_tn(v, p.astype(BF16))
        m_ref[...] = m_new

    def full_tile(j, carry):
        tile(j, False)
        return carry

    _flash_init(m_ref, l_ref, acc_ref)
    n_full = (qi * tq) // tk
    lax.fori_loop(0, n_full, full_tile, 0)
    for jj in range(tq // tk):
        tile(n_full + jj, True)
    o = acc_ref[...] / jnp.maximum(l_ref[...], 1e-30)
    o_ref[...] = jnp.concatenate([o[:, i * tq:(i + 1) * tq].T for i in range(r)], axis=1).astype(BF16)


def fox_prompt(hc, f, f_split, cols, *, b, t, g, r):
    hd = HEAD_DIM
    tq = _pick(t, (512, 256, 128, t))
    tk = _pick(tq, (512, 256, 128, tq))
    nq = t // tq
    qw = r * hd
    fq = f.transpose(0, 2, 1, 3)
    fa = jnp.concatenate([piece.transpose(0, 2, 1, 3) for piece in f_split]
                         + [jnp.ones((b, g, t, 3), BF16), jnp.zeros((b, g, t, LANES - 3 * r - 3), BF16)],
                         axis=-1)
    rows = np.arange(r * tq)[:, None] // tq
    lanes = np.arange(LANES)[None, :]
    bsel = jnp.asarray(np.where((lanes < 3 * r) & (lanes % r == rows), -1.0, 0.0), F32)
    body = functools.partial(_fox_p_body, tq=tq, tk=tk, scale=hd ** -0.5, r=r)
    return pl.pallas_call(
        body,
        grid=(b, g, nq),
        in_specs=[pl.BlockSpec((tq, qw), lambda bb, gg, qi: (bb * nq + qi, gg)),
                  pl.BlockSpec((t, hd), lambda bb, gg, qi: (bb, cols["k"] // hd + gg)),
                  pl.BlockSpec((t, hd), lambda bb, gg, qi: (bb, cols["v"] // hd + gg)),
                  pl.BlockSpec((None, None, tq, r), lambda bb, gg, qi: (bb, gg, qi, 0)),
                  pl.BlockSpec((None, None, t, LANES), lambda bb, gg, qi: (bb, gg, 0, 0)),
                  pl.BlockSpec((r * tq, LANES), lambda bb, gg, qi: (0, 0))],
        out_specs=pl.BlockSpec((tq, qw), lambda bb, gg, qi: (bb * nq + qi, gg)),
        out_shape=SDS((b * t, g * qw), BF16),
        scratch_shapes=[pltpu.VMEM((1, r * tq), F32), pltpu.VMEM((1, r * tq), F32),
                        pltpu.VMEM((hd, r * tq), F32)],
        compiler_params=_cparams(("parallel", "parallel", "arbitrary")),
        name="fox_prompt",
    )(hc, hc, hc, fq, fa, bsel)


def _fox_f_body(pt_ref, new_ref, *rest, pp, n_chunks):
    pages = rest[:pp]
    fk_ref, fn_ref, carry_ref = rest[pp:]
    del pt_ref
    c = pl.program_id(1)
    page = pages[0].shape[2]

    @pl.when(c == 0)
    def _():
        carry_ref[...] = jnp.zeros_like(carry_ref)

    ut = jnp.where(lax.broadcasted_iota(I32, (page, page), 0) <= lax.broadcasted_iota(I32, (page, page), 1),
                   1.0, 0.0).astype(BF16)

    def csum(x, carry):
        hi, mid, lo = _split3(x)
        return (_dot(hi, ut) + _dot(mid, ut)) + _dot(lo, ut) + carry

    carry = carry_ref[...]
    for i in range(pp):
        f = csum(pages[i][0], carry)
        fk_ref[:, i * page:(i + 1) * page] = f
        carry = f[:, page - 1:page]
    carry_ref[...] = carry

    @pl.when(c == n_chunks - 1)
    def _():
        fn_ref[...] = csum(new_ref[...], carry)


def fox_f_sample(logf_pool_t, new_t, page_table):
    bs, n_pages = page_table.shape
    _, nh, page = logf_pool_t.shape
    pp = _pick(n_pages, (16, 8, 4, 2, 1))
    n_chunks = n_pages // pp
    body = functools.partial(_fox_f_body, pp=pp, n_chunks=n_chunks)
    in_specs = [pl.BlockSpec((None, nh, page), lambda b, c, pt: (b, 0, 0))]
    in_specs += [pl.BlockSpec((1, nh, page), functools.partial(lambda b, c, pt, i: (pt[b, c * pp + i], 0, 0), i=i))
                 for i in range(pp)]
    grid_spec = pltpu.PrefetchScalarGridSpec(
        num_scalar_prefetch=1, grid=(bs, n_chunks), in_specs=in_specs,
        out_specs=[pl.BlockSpec((None, nh, pp * page), lambda b, c, pt: (b, 0, c)),
                   pl.BlockSpec((None, nh, page), lambda b, c, pt: (b, 0, 0))],
        scratch_shapes=[pltpu.VMEM((nh, 1), F32)])
    return pl.pallas_call(
        body, grid_spec=grid_spec,
        out_shape=[SDS((bs, nh, n_pages * page), F32), SDS((bs, nh, page), F32)],
        compiler_params=_cparams(("parallel", "arbitrary")), name="fox_f_sample",
    )(page_table, new_t, *([logf_pool_t] * pp))


def _fox_s_body(pt_ref, q_ref, kn_ref, vn_ref, fq_ref, fk_ref, fn_ref, *rest, pp, n_chunks, ts, page, offset,
                scale, g, r):
    pages = rest[:pp]
    o_ref = rest[pp]
    qs_ref, m_ref, l_ref, acc_ref = rest[pp + 1:]
    del pt_ref
    c = pl.program_id(1)
    hd = HEAD_DIM

    @pl.when(c == 0)
    def _():
        for gg in range(g):
            qs_ref[gg] = (_stack_heads(q_ref[:, gg * r * hd:(gg + 1) * r * hd], r, hd) * scale).astype(BF16)
            _flash_init(m_ref.at[gg], l_ref.at[gg], acc_ref.at[gg])

    def fk_rows(f, gg):
        return jnp.concatenate([jnp.broadcast_to(f[gg * r + i:gg * r + i + 1, :], (ts, f.shape[1]))
                                for i in range(r)], axis=0)

    fk = fk_ref[...]
    for gg in range(g):
        ks = [pages[i][0, pl.ds(gg, page, stride=2 * g), :].astype(BF16) for i in range(pp)]
        vs = [pages[i][0, pl.ds(g + gg, page, stride=2 * g), :].astype(BF16) for i in range(pp)]
        s = jnp.concatenate([_dot_nt(qs_ref[gg], k) for k in ks], axis=1) + fq_ref[gg] - fk_rows(fk, gg)
        _flash_update(s, None, vs, m_ref.at[gg], l_ref.at[gg], acc_ref.at[gg])

    @pl.when(c == n_chunks - 1)
    def _():
        t = _rep_rows(offset + lax.broadcasted_iota(I32, (ts, 1), 0), r)
        lane = lax.broadcasted_iota(I32, (1, page), 1)
        outs = []
        for gg in range(g):
            k = _pad_rows(kn_ref[:, gg * hd:(gg + 1) * hd], page).astype(BF16)
            v = _pad_rows(vn_ref[:, gg * hd:(gg + 1) * hd], page).astype(BF16)
            s = _dot_nt(qs_ref[gg], k) + fq_ref[gg] - fk_rows(fn_ref[...], gg)
            _flash_update(s, (offset + lane <= t) & (lane < ts), [v], m_ref.at[gg], l_ref.at[gg], acc_ref.at[gg])
            outs.append(_unstack_heads(_flash_final(l_ref.at[gg], acc_ref.at[gg]), r, ts))
        o_ref[...] = jnp.concatenate(outs, axis=1)


def fox_sample(hc, fq, fk, fn, pool, page_table, cols, *, n_p, bs, ts, page, g, r):
    hd = HEAD_DIM
    n_pages = page_table.shape[1]
    nh = g * r
    offset = n_pages * page
    pp = _pick(n_pages, (16, 8, 4, 2, 1))
    n_chunks = n_pages // pp
    rb = n_p // ts
    body = functools.partial(_fox_s_body, pp=pp, n_chunks=n_chunks, ts=ts, page=page, offset=offset,
                             scale=hd ** -0.5, g=g, r=r)

    def row_spec(width, col0):
        return pl.BlockSpec((ts, width), lambda b, c, pt: (rb + b, col0 // width))

    def page_spec(i):
        return pl.BlockSpec((1, page * 2 * g, hd), lambda b, c, pt: (pt[b, c * pp + i], 0, 0))

    in_specs = [row_spec(nh * hd, cols["q"]), row_spec(g * hd, cols["k"]), row_spec(g * hd, cols["v"]),
                pl.BlockSpec((None, g, r * ts, 1), lambda b, c, pt: (b, 0, 0, 0)),
                pl.BlockSpec((None, nh, pp * page), lambda b, c, pt: (b, 0, c)),
                pl.BlockSpec((None, nh, page), lambda b, c, pt: (b, 0, 0))]
    in_specs += [page_spec(i) for i in range(pp)]
    grid_spec = pltpu.PrefetchScalarGridSpec(
        num_scalar_prefetch=1, grid=(bs, n_chunks), in_specs=in_specs,
        out_specs=pl.BlockSpec((ts, nh * hd), lambda b, c, pt: (b, 0)),
        scratch_shapes=[pltpu.VMEM((g, r * ts, hd), BF16), pltpu.VMEM((g, r * ts, 1), F32),
                        pltpu.VMEM((g, r * ts, 1), F32), pltpu.VMEM((g, r * ts, hd), F32)])
    return pl.pallas_call(
        body, grid_spec=grid_spec, out_shape=SDS((bs * ts, nh * hd), F32),
        compiler_params=_cparams(("parallel", "arbitrary")), name="fox_sample",
    )(page_table, hc, hc, hc, fq, fk, fn, *([pool] * pp))


def _ab_layout(nh_a, g, r, nh_r):
    hd = HEAD_DIM
    qa, kv = nh_a * hd, g * hd
    sizes = [("q", qa), ("kc", kv), ("vc", kv), ("ks", kv), ("vs", kv), ("kw", kv), ("vw", kv),
             ("gt_src", nh_a * 3), ("rq", nh_r * hd), ("rk", nh_r * hd), ("rv", nh_r * hd), ("rg", nh_r * hd)]
    src = {}
    pos = 0
    for name, w in sizes:
        src[name] = pos
        pos += w
    order = ["q", "rq", "rk", "rv", "rg", "kc", "vc", "ks", "vs", "kw", "vw"]
    widths = dict(sizes)
    idx = []
    cols = {}
    for name in order:
        cols[name] = len(idx)
        idx += list(range(src[name], src[name] + widths[name]))
    cols["gt"] = len(idx)
    for gg in range(g):
        blk = [-1] * LANES
        for c in range(3):
            for i in range(r):
                blk[c * r + i] = src["gt_src"] + (gg * r + i) * 3 + c
        idx += blk
    return np.array(idx, np.int32), cols


def _even_layer(x, n_p, b, t, bs, ts, p, cache_kv, cache_win, state, page_table):
    hd = HEAD_DIM
    d = x.shape[1]
    nh_a = d // (2 * hd)
    g = nh_a // 4
    r = nh_a // g
    nh_r = d // (2 * hd)
    idx, cols = _ab_layout(nh_a, g, r, nh_r)
    tn = 768
    ncol = -(-len(idx) // tn) * tn
    idx = np.concatenate([idx, np.full(ncol - len(idx), -1, np.int32)])
    w_in = jnp.where(idx[None, :] >= 0, jnp.take(p["w_in"], np.maximum(idx, 0), axis=1), 0.0).astype(BF16)
    hab, _ = mm_norm(x, p["norm"], w_in, tn)

    kv4 = 4 * g * hd
    c_rows = cols["kc"]
    c_win = cols["kw"]
    new_rows_p = hab[:n_p, c_rows:c_rows + kv4].reshape(b, t, 4, g, hd)
    new_rows_s = hab[n_p:, c_rows:c_rows + kv4].reshape(bs, ts, 4, g, hd)
    new_win_p = hab[:n_p, c_win:c_win + 2 * g * hd].reshape(b, t, 2, g, hd)
    new_win_s = hab[n_p:, c_win:c_win + 2 * g * hd].reshape(bs, ts, 2, g, hd)
    win_state_p = new_win_p[:, -min(WINDOW, t):]
    wb = cache_win.shape[1]
    win_state_s = jnp.concatenate([cache_win, new_win_s], axis=1)[:, -wb:]

    slopes = jnp.exp2(-8.0 * (jnp.arange(nh_a, dtype=F32) + 1.0) / nh_a).reshape(g, r)
    w1k, w2k = p["w1_k"].astype(BF16), p["w2_k"].astype(BF16)
    w1v, w2v = p["w1_v"].astype(BF16), p["w2_v"].astype(BF16)

    def cmp_pair(blocks):
        bb, nb = blocks.shape[:2]
        flat = blocks.transpose(3, 0, 1, 4, 2, 5).reshape(2, bb * nb * g, CMP_BLOCK * hd)
        kc = compress(flat[0], p["pe_k"].reshape(-1), w1k, w2k).reshape(bb, nb, g, hd).transpose(0, 2, 1, 3)
        vc = compress(flat[1], p["pe_v"].reshape(-1), w1v, w2v).reshape(bb, nb, g, hd).transpose(0, 2, 1, 3)
        return kc, vc

    nb_p = t // CMP_BLOCK
    kc_p, vc_p = cmp_pair(new_rows_p[:, :nb_p * CMP_BLOCK, 0:2].reshape(b, nb_p, CMP_BLOCK, 2, g, hd))
    o_a_p = nsa_prompt(hab, kc_p, vc_p, slopes, cols, b=b, t=t, g=g, r=r)

    page = cache_kv.shape[1]
    n_pages = page_table.shape[1]
    nb_s = (n_pages * page + ts) // CMP_BLOCK
    assert nb_s * CMP_BLOCK == n_pages * page and page % CMP_BLOCK == 0
    pool = cache_kv.reshape(cache_kv.shape[0], page * 4 * g, hd)
    kc_s, vc_s = compress_sample(pool, page_table, p, page=page, g=g)
    o_a_s = nsa_sample(hab, kc_s, vc_s, pool, cache_win.reshape(bs, wb * 2 * g, hd), page_table, slopes, cols,
                       n_p=n_p, bs=bs, ts=ts, page=page, g=g, r=r)

    o_b_p, st_p = retention(hab, p["gn"], jnp.zeros((b, nh_r, hd, hd), F32), cols, row0=0, b=b, t=t, nh=nh_r)
    o_b_s, st_s = retention(hab, p["gn"], state, cols, row0=n_p, b=bs, t=ts, nh=nh_r)

    o = jnp.concatenate([jnp.concatenate([o_a_p, o_b_p.astype(BF16)], axis=1),
                         jnp.concatenate([o_a_s.astype(BF16), o_b_s.astype(BF16)], axis=1)], axis=0)
    x = mm_res(o, p["w_out"].astype(BF16), x)
    return x, (new_rows_p, win_state_p, st_p, new_rows_s, win_state_s, st_s)


def _odd_layer(x, n_p, b, t, bs, ts, p, cache_kv, cache_logf, page_table):
    hd = HEAD_DIM
    d = x.shape[1]
    nh = d // hd
    g = nh // 4
    r = nh // g
    cols = {"q": 0, "k": nh * hd, "v": (nh + g) * hd, "f": (nh + 2 * g) * hd}
    ncol_src = p["w_in"].shape[1]
    tn = 640
    ncol = -(-(cols["f"] + LANES) // tn) * tn
    w_in = jnp.pad(p["w_in"], ((0, 0), (0, ncol - ncol_src))).astype(BF16)
    hc, _ = mm_norm(x, p["norm"], w_in, tn)

    new_rows_p = hc[:n_p, cols["k"]:cols["f"]].reshape(b, t, 2, g, hd)
    new_rows_s = hc[n_p:, cols["k"]:cols["f"]].reshape(bs, ts, 2, g, hd)
    b_pad = jnp.pad(p["b_f"].astype(F32), (0, LANES - nh)).reshape(1, LANES)
    logf = fox_logf(hc, b_pad, cols["f"])
    new_logf_p = logf[:n_p, :nh].reshape(b, t, nh)
    new_logf_s = logf[n_p:, :nh].reshape(bs, ts, nh)

    f_p, *f_split = [a[:, :nh].reshape(b, t, g, r) for a in cumsum_rows(logf, b=b, t=t)]
    o_p = fox_prompt(hc, f_p, f_split, cols, b=b, t=t, g=g, r=r)

    page = cache_kv.shape[1]
    new_t = jnp.pad(new_logf_s.transpose(0, 2, 1), ((0, 0), (0, 0), (0, page - ts)))
    fk, fn = fox_f_sample(cache_logf.transpose(0, 2, 1), new_t, page_table)
    fq = fn[:, :, :ts].reshape(bs, g, r * ts, 1)
    o_s = fox_sample(hc, fq, fk, fn, cache_kv.reshape(cache_kv.shape[0], page * 2 * g, hd), page_table, cols,
                     n_p=n_p, bs=bs, ts=ts, page=page, g=g, r=r)

    x = mm_res(jnp.concatenate([o_p, o_s.astype(BF16)], axis=0), p["w_out"].astype(BF16), x)
    return x, (new_rows_p, new_logf_p, new_rows_s, new_logf_s)


def _peer_layer(x, nw, wq, k1, k2, u, v):
    q, xn = mm_norm(x, nw, wq.astype(BF16), _pick(wq.shape[1], (1024, 512, 256, 128)))
    s1t, s2t, st = peer_score(q, k1, k2)
    return peer_dense(xn, u.astype(BF16), v.T.astype(BF16), s1t, s2t, st, x)


def kernel(x_prompt, x_sample, cache_nsa_kv, cache_nsa_win, state_ret, cache_fox_kv, cache_fox_logf,
           page_table, norm_mix, norm_ffn, norm_final, w_in_ab, w_out_ab, cmp_pe_k, cmp_w1_k, cmp_w2_k,
           cmp_pe_v, cmp_w1_v, cmp_w2_v, ret_gn, w_in_c, b_forget, w_out_c, peer_wq, peer_k1, peer_k2,
           peer_u, peer_v):
    b, t, d = x_prompt.shape
    bs, ts, _ = x_sample.shape
    n_p, n_s = b * t, bs * ts
    depth = norm_mix.shape[0]
    x = jnp.concatenate([x_prompt.reshape(n_p, d), x_sample.reshape(n_s, d)], axis=0)
    even, odd = [], []
    for l in range(depth):
        if l % 2 == 0:
            e = l // 2
            p = dict(norm=norm_mix[l], w_in=w_in_ab[e], w_out=w_out_ab[e], pe_k=cmp_pe_k[e], w1_k=cmp_w1_k[e],
                     w2_k=cmp_w2_k[e], pe_v=cmp_pe_v[e], w1_v=cmp_w1_v[e], w2_v=cmp_w2_v[e], gn=ret_gn[e])
            x, outs = _even_layer(x, n_p, b, t, bs, ts, p, cache_nsa_kv[e], cache_nsa_win[e], state_ret[e],
                                  page_table)
            even.append(outs)
        else:
            o = l // 2
            p = dict(norm=norm_mix[l], w_in=w_in_c[o], b_f=b_forget[o], w_out=w_out_c[o])
            x, outs = _odd_layer(x, n_p, b, t, bs, ts, p, cache_fox_kv[o], cache_fox_logf[o], page_table)
            odd.append(outs)
        x = _peer_layer(x, norm_ffn[l], peer_wq[l], peer_k1[l], peer_k2[l], peer_u[l], peer_v[l])
    y = rms_final(x, norm_final)
    y_prompt = y[:n_p].reshape(b, t, d)
    y_sample = y[n_p:].reshape(bs, ts, d)

    def stack(group, i):
        return jnp.stack([o[i] for o in group])

    return (y_prompt, y_sample, stack(even, 0), stack(even, 1), stack(even, 2), stack(odd, 0), stack(odd, 1),
            stack(even, 3), stack(even, 4), stack(even, 5), stack(odd, 2), stack(odd, 3))
```

```python
import functools
import math

import numpy as np
import jax
import jax.numpy as jnp
from jax import lax
from jax.experimental import pallas as pl
from jax.experimental.pallas import tpu as pltpu

F32 = jnp.float32
BF16 = jnp.bfloat16
I32 = jnp.int32

HEAD_DIM = 128
CMP_BLOCK = 64
SEL_BLOCK = 64
SEL_TOPK = 16
WINDOW = 512
CMP_HIDDEN = 256
RET_CHUNK = 128
Q_BLOCK = 128
PEER_HEADS = 8
PEER_DK = 256
PEER_TOPK = 16
EPS = 1e-6
GN_EPS = 1e-5
NEG = -1e30

LANES = 128
SUBLANES = 8
VMEM_LIMIT = 56 * 1024 * 1024

SDS = jax.ShapeDtypeStruct


def _cparams(sem):
    return pltpu.CompilerParams(dimension_semantics=sem, vmem_limit_bytes=VMEM_LIMIT)


def _dot(a, b):
    return jnp.dot(a, b, preferred_element_type=F32)


def _dot_nt(a, b):
    return lax.dot_general(a, b, (((1,), (1,)), ((), ())), preferred_element_type=F32)


def _dot_tn(a, b):
    return lax.dot_general(a, b, (((0,), (0,)), ((), ())), preferred_element_type=F32)


def _pick(n, cands):
    for c in cands:
        if c <= n and n % c == 0:
            return c
    raise ValueError(f"no tile for {n} in {cands}")


def _gelu(x):
    c = math.sqrt(2.0 / math.pi)
    return x * (0.5 * (1.0 + jnp.tanh(c * (x + 0.044715 * (x * x * x)))))


def _split3(x):
    hi = x.astype(BF16)
    r1 = x - hi.astype(F32)
    mid = r1.astype(BF16)
    lo = (r1 - mid.astype(F32)).astype(BF16)
    return hi, mid, lo


def _stack_heads(x, nh, hd):
    return jnp.concatenate([x[:, r * hd:(r + 1) * hd] for r in range(nh)], axis=0)


def _unstack_heads(x, nh, t):
    return jnp.concatenate([x[r * t:(r + 1) * t, :] for r in range(nh)], axis=1)


def _rep_rows(x, k):
    return jnp.concatenate([x] * k, axis=0)


def _mm_norm_body(x_ref, nw_ref, w_ref, o_ref, xn_ref):
    @pl.when(pl.program_id(1) == 0)
    def _():
        x = x_ref[...]
        ms = jnp.mean(x * x, axis=-1, keepdims=True)
        xn_ref[...] = (x * lax.rsqrt(ms + EPS) * nw_ref[...]).astype(BF16)

    o_ref[...] = _dot(xn_ref[...], w_ref[...])


def mm_norm(x, nw, w_bf, tn):
    n, d = x.shape
    nn = w_bf.shape[1]
    tm = _pick(n, (768, 512, 256, 128, 64, 32, 16))
    return pl.pallas_call(
        _mm_norm_body,
        grid=(n // tm, nn // tn),
        in_specs=[pl.BlockSpec((tm, d), lambda i, j: (i, 0)),
                  pl.BlockSpec((1, d), lambda i, j: (0, 0)),
                  pl.BlockSpec((d, tn), lambda i, j: (0, j))],
        out_specs=[pl.BlockSpec((tm, tn), lambda i, j: (i, j)),
                   pl.BlockSpec((tm, d), lambda i, j: (i, 0))],
        out_shape=[SDS((n, nn), F32), SDS((n, d), BF16)],
        compiler_params=_cparams(("parallel", "arbitrary")),
        name="mm_norm",
    )(x, nw.reshape(1, d), w_bf)


def _mm_res_body(a_ref, w_ref, r_ref, o_ref):
    o_ref[...] = r_ref[...] + _dot(a_ref[...], w_ref[...])


def mm_res(a_bf, w_bf, res):
    n, k = a_bf.shape
    nn = w_bf.shape[1]
    tm = _pick(n, (768, 512, 256, 128, 64, 32, 16))
    tn = _pick(nn, (1024, 512, 256, 128))
    return pl.pallas_call(
        _mm_res_body,
        grid=(n // tm, nn // tn),
        in_specs=[pl.BlockSpec((tm, k), lambda i, j: (i, 0)),
                  pl.BlockSpec((k, tn), lambda i, j: (0, j)),
                  pl.BlockSpec((tm, tn), lambda i, j: (i, j))],
        out_specs=pl.BlockSpec((tm, tn), lambda i, j: (i, j)),
        out_shape=SDS((n, nn), F32),
        compiler_params=_cparams(("parallel", "arbitrary")),
        name="mm_res",
    )(a_bf, w_bf, res)


def _rms_body(x_ref, nw_ref, o_ref):
    x = x_ref[...]
    ms = jnp.mean(x * x, axis=-1, keepdims=True)
    o_ref[...] = x * lax.rsqrt(ms + EPS) * nw_ref[...]


def rms_final(x, nw):
    n, d = x.shape
    tm = _pick(n, (768, 512, 256, 128, 64, 32, 16, 8))
    return pl.pallas_call(
        _rms_body,
        grid=(n // tm,),
        in_specs=[pl.BlockSpec((tm, d), lambda i: (i, 0)), pl.BlockSpec((1, d), lambda i: (0, 0))],
        out_specs=pl.BlockSpec((tm, d), lambda i: (i, 0)),
        out_shape=SDS((n, d), F32),
        compiler_params=_cparams(("parallel",)),
        name="rms_final",
    )(x, nw.reshape(1, d))


def _batcher_pairs(n):
    pairs = []
    p = 1
    while p < n:
        k = p
        while k >= 1:
            for j in range(k % p, n - k, 2 * k):
                for i in range(min(k, n - j - k)):
                    if (i + j) // (2 * p) == (i + j + k) // (2 * p):
                        pairs.append((i + j, i + j + k))
            k //= 2
        p *= 2
    return pairs


_SORT16 = _batcher_pairs(16)


def _sort16_desc(xs):
    xs = list(xs)
    for i, j in _SORT16:
        hi = jnp.maximum(xs[i], xs[j])
        lo = jnp.minimum(xs[i], xs[j])
        xs[i], xs[j] = hi, lo
    return xs


def _bitonic16_desc(c):
    c = list(c)
    for stride in (8, 4, 2, 1):
        for i in range(16):
            if i & stride == 0:
                hi = jnp.maximum(c[i], c[i + stride])
                lo = jnp.minimum(c[i], c[i + stride])
                c[i], c[i + stride] = hi, lo
    return c


def _merge16_desc(a, b):
    return _bitonic16_desc([jnp.maximum(a[i], b[15 - i]) for i in range(16)])


def _top16_sorted(s):
    cols = _sort16_desc([s[v * SUBLANES:(v + 1) * SUBLANES, :] for v in range(16)])
    for shift in (4, 2, 1):
        other = [pltpu.roll(x, shift, 0) for x in cols]
        cols = _merge16_desc(cols, other)
    return cols


def _peer_score_body(q_ref, k1_ref, k2_ref, s1_ref, s2_ref, st_ref):
    tm = q_ref.shape[0]
    half = PEER_DK // 2
    k1 = k1_ref[...].astype(BF16)
    k2 = k2_ref[...].astype(BF16)
    sub = lax.broadcasted_iota(I32, (SUBLANES, tm), 0)
    a_top = None
    b_top = None
    for h in range(PEER_HEADS):
        qh = q_ref[:, h * PEER_DK:(h + 1) * PEER_DK]
        s1 = _dot_nt(k1, qh[:, :half].astype(BF16))
        s2 = _dot_nt(k2, qh[:, half:].astype(BF16))
        s1_ref[h] = s1
        s2_ref[h] = s2
        a_h = _top16_sorted(s1)
        b_h = _top16_sorted(s2)
        if h == 0:
            a_top, b_top = a_h, b_h
        else:
            a_top = [jnp.where(sub == h, x, y) for x, y in zip(a_h, a_top)]
            b_top = [jnp.where(sub == h, x, y) for x, y in zip(b_h, b_top)]
    ninf = jnp.full((SUBLANES, tm), -jnp.inf, F32)
    row0 = [a_top[0] + b_top[b] for b in range(16)]
    col0 = [a_top[a] + b_top[0] for a in range(1, 16)] + [ninf]
    mid = ([a_top[1] + b_top[b] for b in range(1, 8)] + [a_top[a] + b_top[1] for a in range(2, 8)]
           + [a_top[2] + b_top[b] for b in range(2, 5)])
    mid = _sort16_desc(mid)
    v0 = a_top[3] + b_top[2]
    v1 = a_top[4] + b_top[2]
    v2 = a_top[3] + b_top[3]
    tail = [v0, jnp.maximum(v1, v2), jnp.minimum(v1, v2)] + [ninf] * 13
    top = _merge16_desc(_merge16_desc(_merge16_desc(row0, col0), mid), tail)
    z = jnp.zeros((SUBLANES, tm), F32)
    for i in range(16):
        z = z + jnp.exp(top[i] - top[0])
    st_ref[0] = top[15]
    st_ref[1] = a_top[0]
    st_ref[2] = b_top[0]
    st_ref[3] = z


def peer_score(q, k1, k2):
    n, d = q.shape
    nk = k1.shape[0]
    assert nk == 128 and d == PEER_HEADS * PEER_DK
    tm = _pick(n, (256, 128))
    return pl.pallas_call(
        _peer_score_body,
        grid=(n // tm,),
        in_specs=[pl.BlockSpec((tm, d), lambda i: (i, 0)),
                  pl.BlockSpec(k1.shape, lambda i: (0, 0)),
                  pl.BlockSpec(k2.shape, lambda i: (0, 0))],
        out_specs=[pl.BlockSpec((PEER_HEADS, nk, tm), lambda i: (0, 0, i)),
                   pl.BlockSpec((PEER_HEADS, nk, tm), lambda i: (0, 0, i)),
                   pl.BlockSpec((4, PEER_HEADS, tm), lambda i: (0, 0, i))],
        out_shape=[SDS((PEER_HEADS, nk, n), F32), SDS((PEER_HEADS, nk, n), F32),
                   SDS((4, PEER_HEADS, n), F32)],
        compiler_params=_cparams(("parallel",)),
        name="peer_score",
    )(q, k1, k2)


def _peer_dense_body(xn_ref, u_ref, vt_ref, s1_ref, s2_ref, st_ref, res_ref, o_ref,
                     acc_ref, e2_ref, h_ref, act_ref, *, nc, n_steps):
    c = pl.program_id(1)
    nk = s2_ref.shape[1]

    @pl.when(c == 0)
    def _():
        acc_ref[...] = jnp.zeros_like(acc_ref)
        for h in range(PEER_HEADS):
            e2_ref[h] = jnp.exp(s2_ref[h] - st_ref[2, h:h + 1, :])

    h_ref[...] = _dot_nt(u_ref[...], xn_ref[...])
    for s in range(nc):
        e1 = c * nc + s
        w = jnp.zeros((nk, xn_ref.shape[0]), F32)
        for h in range(PEER_HEADS):
            s1row = s1_ref[h, pl.ds(e1, 1), :]
            p1 = jnp.exp(s1row - st_ref[1, h:h + 1, :]) / st_ref[3, h:h + 1, :]
            a = s1row + s2_ref[h]
            w = w + jnp.where(a >= st_ref[0, h:h + 1, :], p1 * e2_ref[h], 0.0)
        act_ref[s * nk:(s + 1) * nk, :] = (_gelu(h_ref[s * nk:(s + 1) * nk, :]) * w).astype(BF16)
    acc_ref[...] += _dot(vt_ref[...], act_ref[...])

    @pl.when(c == n_steps - 1)
    def _():
        o_ref[...] = res_ref[...] + acc_ref[...].T


def peer_dense(xn_bf, u_bf, vt_bf, s1t, s2t, st, res):
    n, d = xn_bf.shape
    ne = u_bf.shape[0]
    nk = s1t.shape[1]
    tm = _pick(n, (768, 512, 256, 128))
    nc = 8
    te = nc * nk
    n_steps = ne // te
    body = functools.partial(_peer_dense_body, nc=nc, n_steps=n_steps)
    once = pl.Buffered(1)
    return pl.pallas_call(
        body,
        grid=(n // tm, n_steps),
        in_specs=[pl.BlockSpec((tm, d), lambda i, c: (i, 0), pipeline_mode=once),
                  pl.BlockSpec((te, d), lambda i, c: (c, 0)),
                  pl.BlockSpec((d, te), lambda i, c: (0, c)),
                  pl.BlockSpec((PEER_HEADS, nk, tm), lambda i, c: (0, 0, i), pipeline_mode=once),
                  pl.BlockSpec((PEER_HEADS, nk, tm), lambda i, c: (0, 0, i), pipeline_mode=once),
                  pl.BlockSpec((4, PEER_HEADS, tm), lambda i, c: (0, 0, i), pipeline_mode=once),
                  pl.BlockSpec((tm, d), lambda i, c: (i, 0), pipeline_mode=once)],
        out_specs=pl.BlockSpec((tm, d), lambda i, c: (i, 0), pipeline_mode=once),
        out_shape=SDS((n, d), F32),
        scratch_shapes=[pltpu.VMEM((d, tm), F32), pltpu.VMEM((PEER_HEADS, nk, tm), F32),
                        pltpu.VMEM((te, tm), F32), pltpu.VMEM((te, tm), BF16)],
        compiler_params=_cparams(("parallel", "arbitrary")),
        name="peer_dense",
    )(xn_bf, u_bf, vt_bf, s1t, s2t, st, res)


def _flash_init(m_ref, l_ref, acc_ref):
    m_ref[...] = jnp.full(m_ref.shape, NEG, F32)
    l_ref[...] = jnp.zeros(l_ref.shape, F32)
    acc_ref[...] = jnp.zeros(acc_ref.shape, F32)


def _flash_update(s, mask, v_tiles, m_ref, l_ref, acc_ref):
    if mask is not None:
        s = jnp.where(mask, s, NEG)
    m_prev = m_ref[...]
    m_new = jnp.maximum(m_prev, jnp.max(s, axis=-1, keepdims=True))
    alpha = jnp.exp(m_prev - m_new)
    p = jnp.exp(s - m_new)
    if mask is not None:
        p = jnp.where(mask, p, 0.0)
    l_ref[...] = alpha * l_ref[...] + jnp.sum(p, axis=-1, keepdims=True)
    pb = p.astype(BF16)
    pv = None
    off = 0
    for v in v_tiles:
        part = _dot(pb[:, off:off + v.shape[0]], v)
        pv = part if pv is None else pv + part
        off += v.shape[0]
    acc_ref[...] = alpha * acc_ref[...] + pv
    m_ref[...] = m_new


def _flash_final(l_ref, acc_ref):
    return acc_ref[...] / jnp.maximum(l_ref[...], 1e-30)


def _softmax_masked(s, mask):
    s = jnp.where(mask, s, NEG)
    m = jnp.max(s, axis=-1, keepdims=True)
    e = jnp.where(mask, jnp.exp(s - m), 0.0)
    return e / jnp.maximum(jnp.sum(e, axis=-1, keepdims=True), 1e-30)


def _topk_mask(score, k):
    n = score.shape[-1]
    lane = lax.broadcasted_iota(I32, score.shape, 1).astype(F32)
    sel = jnp.zeros(score.shape, F32)
    for _ in range(k):
        m = jnp.max(score, axis=-1, keepdims=True)
        idx = jnp.min(jnp.where(score == m, lane, float(n)), axis=-1, keepdims=True)
        hit = lane == idx
        sel = jnp.where(hit, jnp.where(m >= 0.0, 1.0, 0.0), sel)
        score = jnp.where(hit, -jnp.inf, score)
    return sel


def _compress_body(x_ref, pe_ref, w1_ref, w2_ref, o_ref):
    xb = (x_ref[...] + pe_ref[...]).astype(BF16)
    h = _gelu(_dot(xb, w1_ref[...]))
    o_ref[...] = _dot(h.astype(BF16), w2_ref[...])


def compress(x, pe, w1_bf, w2_bf):
    rows, k = x.shape
    tr = _pick(rows, (256, 128, 64, 32, 16, 8))
    hid = w1_bf.shape[1]
    hd = w2_bf.shape[1]
    return pl.pallas_call(
        _compress_body,
        grid=(rows // tr,),
        in_specs=[pl.BlockSpec((tr, k), lambda i: (i, 0)),
                  pl.BlockSpec((1, k), lambda i: (0, 0)),
                  pl.BlockSpec((k, hid), lambda i: (0, 0)),
                  pl.BlockSpec((hid, hd), lambda i: (0, 0))],
        out_specs=pl.BlockSpec((tr, hd), lambda i: (i, 0)),
        out_shape=SDS((rows, hd), F32),
        compiler_params=_cparams(("parallel",)),
        name="nsa_compress",
    )(x, pe.reshape(1, k), w1_bf, w2_bf)


def _nsa_cmp_and_select(qs, slope, t1, kc_ref, vc_ref, *, r, nb, n_sel, ns_pad, k_top):
    tq = t1.shape[0]
    t = _rep_rows(t1, r)
    blk_end = lax.broadcasted_iota(I32, (1, nb), 1) * CMP_BLOCK + (CMP_BLOCK - 1)
    d_c = t - blk_end
    s = _dot_nt(qs, kc_ref[...].astype(BF16)) - slope * d_c.astype(F32)
    p_c = _softmax_masked(s, d_c >= 0)
    o_c = _dot(p_c.astype(BF16), vc_ref[...].astype(BF16))
    imp = p_c[0:tq]
    for i in range(1, r):
        imp = imp + p_c[i * tq:(i + 1) * tq]
    if ns_pad > nb:
        imp = jnp.concatenate([imp, jnp.zeros((tq, ns_pad - nb), F32)], axis=1)
    jsel = lax.broadcasted_iota(I32, (1, ns_pad), 1)
    cur = jnp.right_shift(t1, 6)
    forced = (jsel == 0) | (jsel == cur) | (jsel == cur - 1)
    score = jnp.where(forced, r + 1.0, jnp.where(jsel <= cur, imp, -1.0))
    score = jnp.where(jsel < n_sel, score, -2.0)
    return o_c, _topk_mask(score, k_top)


def _gate_cols(gsig, c, r):
    return jnp.concatenate([gsig[:, c * r + i:c * r + i + 1] for i in range(r)], axis=0)


def _flash_update_t(s, mask, v, m_ref, l_ref, acc_ref):
    if mask is not None:
        s = jnp.where(mask, s, NEG)
    m_prev = m_ref[...]
    m_new = jnp.maximum(m_prev, jnp.max(s, axis=0, keepdims=True))
    alpha = jnp.exp(m_prev - m_new)
    p = jnp.exp(s - m_new)
    if mask is not None:
        p = jnp.where(mask, p, 0.0)
    l_ref[...] = alpha * l_ref[...] + jnp.sum(p, axis=0, keepdims=True)
    acc_ref[...] = alpha * acc_ref[...] + _dot_tn(v, p.astype(BF16))
    m_ref[...] = m_new


def _nsa_p_body(q_ref, gt_ref, kc_ref, vc_ref, ks_ref, vs_ref, kw_ref, vw_ref, sl_ref, kp_ref, o_ref,
                m_ref, l_ref, acc_ref, *, tq, tk, wk, nb, n_sel, k_top, scale, r):
    qi = pl.program_id(2)
    a = qi * tq
    rows = r * tq
    qs = (_stack_heads(q_ref[...], r, HEAD_DIM) * scale).astype(BF16)
    slope = sl_ref[...]
    t1 = a + lax.broadcasted_iota(I32, (tq, 1), 0)
    t = _rep_rows(t1, r)
    o_c, sel = _nsa_cmp_and_select(qs, slope, t1, kc_ref, vc_ref, r=r, nb=nb, n_sel=n_sel,
                                   ns_pad=n_sel, k_top=k_top)
    sel_bf = sel.astype(BF16)
    lane = lax.broadcasted_iota(I32, (rows, LANES), 1)
    t_hi = jnp.right_shift(t, 6).astype(F32)
    t_lo = jnp.bitwise_and(t, SEL_BLOCK - 1).astype(F32)
    qb = jnp.where(lane == 0, slope * 64.0,
                   jnp.where(lane == 1, slope,
                             jnp.where(lane == 2, -(slope * 64.0) * t_hi,
                                       jnp.where(lane == 3, -slope * t_lo, 0.0)))).astype(BF16)
    qa = jnp.concatenate([qs, qb], axis=1)
    t_row = a + jnp.concatenate([lax.broadcasted_iota(I32, (1, tq), 1)] * r, axis=1)

    def attend(k_ref_, v_ref_, start, width, window):
        ka = jnp.concatenate([k_ref_[pl.ds(start, width), :].astype(BF16), kp_ref[pl.ds(start, width), :]], axis=1)
        v = v_ref_[pl.ds(start, width), :].astype(BF16)
        s = _dot_nt(ka, qa)
        pos = start + lax.broadcasted_iota(I32, (width, 1), 0)
        mask = pos <= t_row
        if window:
            mask = mask & (pos >= t_row - WINDOW)
        else:
            blk = lax.broadcasted_iota(I32, (width, n_sel), 1)
            key_blk = jnp.right_shift(start + lax.broadcasted_iota(I32, (width, n_sel), 0), 6)
            bm = _dot_nt(jnp.where(blk == key_blk, 1.0, 0.0).astype(BF16), sel_bf)
            mask = mask & (jnp.concatenate([bm] * r, axis=1) > 0.5)
        _flash_update_t(s, mask, v, m_ref, l_ref, acc_ref)

    def sel_step(j, carry):
        attend(ks_ref, vs_ref, pl.multiple_of(j * tk, tk), tk, False)
        return carry

    _flash_init(m_ref, l_ref, acc_ref)
    lax.fori_loop(0, (a + tq - 1) // tk + 1, sel_step, 0)
    o_s = (acc_ref[...] / jnp.maximum(l_ref[...], 1e-30)).T

    _flash_init(m_ref, l_ref, acc_ref)
    attend(kw_ref, vw_ref, pl.multiple_of(jnp.maximum(a + tq - wk, 0), SUBLANES), wk, True)
    o_w = (acc_ref[...] / jnp.maximum(l_ref[...], 1e-30)).T

    gsig = 1.0 / (1.0 + jnp.exp(-gt_ref[...]))
    o = _gate_cols(gsig, 0, r) * o_c + _gate_cols(gsig, 1, r) * o_s + _gate_cols(gsig, 2, r) * o_w
    o_ref[...] = _unstack_heads(o, r, tq).astype(BF16)


def nsa_prompt(hab, kc, vc, slopes, cols, *, b, t, g, r):
    hd = HEAD_DIM
    nb = kc.shape[2]
    n_sel = -(-t // SEL_BLOCK)
    assert n_sel == nb and t % SEL_BLOCK == 0
    tq = _pick(t, (2 * Q_BLOCK, Q_BLOCK, t))
    tk = _pick(t, (512, 256, 128, t))
    wk = min(WINDOW + tq, t)
    nq = t // tq
    qw = r * hd
    body = functools.partial(_nsa_p_body, tq=tq, tk=tk, wk=wk, nb=nb, n_sel=n_sel, k_top=min(SEL_TOPK, n_sel),
                             scale=hd ** -0.5, r=r)
    assert t <= 256 * SEL_BLOCK
    pos = np.arange(t)
    kp = np.zeros((t, LANES), np.float32)
    kp[:, 0], kp[:, 1], kp[:, 2], kp[:, 3] = pos >> 6, pos & (SEL_BLOCK - 1), 1.0, 1.0
    kpos = jnp.asarray(kp, BF16)

    def seq_spec(col0):
        return pl.BlockSpec((t, hd), lambda bb, gg, qi: (bb, col0 // hd + gg))

    return pl.pallas_call(
        body,
        grid=(b, g, nq),
        in_specs=[pl.BlockSpec((tq, qw), lambda bb, gg, qi: (bb * nq + qi, gg)),
                  pl.BlockSpec((tq, LANES), lambda bb, gg, qi: (bb * nq + qi, cols["gt"] // LANES + gg)),
                  pl.BlockSpec((None, None, nb, hd), lambda bb, gg, qi: (bb, gg, 0, 0)),
                  pl.BlockSpec((None, None, nb, hd), lambda bb, gg, qi: (bb, gg, 0, 0)),
                  seq_spec(cols["ks"]), seq_spec(cols["vs"]), seq_spec(cols["kw"]), seq_spec(cols["vw"]),
                  pl.BlockSpec((None, r * tq, 1), lambda bb, gg, qi: (gg, 0, 0)),
                  pl.BlockSpec((t, LANES), lambda bb, gg, qi: (0, 0))],
        out_specs=pl.BlockSpec((tq, qw), lambda bb, gg, qi: (bb * nq + qi, gg)),
        out_shape=SDS((b * t, g * qw), BF16),
        scratch_shapes=[pltpu.VMEM((1, r * tq), F32), pltpu.VMEM((1, r * tq), F32),
                        pltpu.VMEM((hd, r * tq), F32)],
        compiler_params=_cparams(("parallel", "parallel", "arbitrary")),
        name="nsa_prompt",
    )(hab, hab, kc, vc, hab, hab, hab, hab, jnp.repeat(slopes, tq, axis=1).reshape(g, r * tq, 1), kpos)


def _pad_rows(x, rows):
    return jnp.concatenate([x, jnp.zeros((rows - x.shape[0], x.shape[1]), x.dtype)], axis=0)


def _block_expand(ns_pad, first_pos, width):
    blk = lax.broadcasted_iota(I32, (ns_pad, width), 0)
    key_blk = jnp.right_shift(first_pos + lax.broadcasted_iota(I32, (ns_pad, width), 1), 6)
    return jnp.where(blk == key_blk, 1.0, 0.0).astype(BF16)


def _nsa_s_body(pt_ref, q_ref, gt_ref, ksn_ref, vsn_ref, kwn_ref, vwn_ref, kc_ref, vc_ref, win_ref,
                sl_ref, *rest, pp, n_chunks, ts, page, offset, wb, nb, n_sel, ns_pad, k_top, scale, g, r):
    pages = rest[:pp]
    o_ref = rest[pp]
    qs_ref, oc_ref, sel_ref, m_ref, l_ref, acc_ref = rest[pp + 1:]
    del pt_ref
    c = pl.program_id(1)
    hd = HEAD_DIM
    t1 = offset + lax.broadcasted_iota(I32, (ts, 1), 0)
    t = _rep_rows(t1, r)
    lane = lax.broadcasted_iota(I32, (1, page), 1)

    @pl.when(c == 0)
    def _():
        for gg in range(g):
            qs = (_stack_heads(q_ref[:, gg * r * hd:(gg + 1) * r * hd], r, hd) * scale).astype(BF16)
            qs_ref[gg] = qs
            o_c, sel = _nsa_cmp_and_select(qs, sl_ref[gg], t1, kc_ref.at[gg], vc_ref.at[gg], r=r, nb=nb,
                                           n_sel=n_sel, ns_pad=ns_pad, k_top=k_top)
            oc_ref[gg] = o_c
            sel_ref[gg] = _rep_rows(sel, r).astype(BF16)
            _flash_init(m_ref.at[gg], l_ref.at[gg], acc_ref.at[gg])

    width = pp * page
    first = c * width
    d = t - (first + lax.broadcasted_iota(I32, (1, width), 1))
    expand = _block_expand(ns_pad, first, width)
    for gg in range(g):
        ks = [pages[i][0, pl.ds(2 * g + gg, page, stride=4 * g), :].astype(BF16) for i in range(pp)]
        vs = [pages[i][0, pl.ds(3 * g + gg, page, stride=4 * g), :].astype(BF16) for i in range(pp)]
        s = jnp.concatenate([_dot_nt(qs_ref[gg], k) for k in ks], axis=1) - sl_ref[gg] * d.astype(F32)
        bm = _dot(sel_ref[gg], expand)
        _flash_update(s, (bm > 0.5) & (d >= 0), vs, m_ref.at[gg], l_ref.at[gg], acc_ref.at[gg])

    @pl.when(c == n_chunks - 1)
    def _():
        gsig = 1.0 / (1.0 + jnp.exp(-gt_ref[...]))
        dn = t - (offset + lane)
        expand_n = _block_expand(ns_pad, offset, page)
        outs = []
        for gg in range(g):
            k = _pad_rows(ksn_ref[:, gg * hd:(gg + 1) * hd], page).astype(BF16)
            v = _pad_rows(vsn_ref[:, gg * hd:(gg + 1) * hd], page).astype(BF16)
            s = _dot_nt(qs_ref[gg], k) - sl_ref[gg] * dn.astype(F32)
            bm = _dot(sel_ref[gg], expand_n)
            _flash_update(s, (bm > 0.5) & (dn >= 0) & (lane < ts), [v], m_ref.at[gg], l_ref.at[gg],
                          acc_ref.at[gg])
            o_s = _flash_final(l_ref.at[gg], acc_ref.at[gg])
            _flash_init(m_ref.at[gg], l_ref.at[gg], acc_ref.at[gg])
            kw = win_ref[pl.ds(gg, wb, stride=2 * g), :].astype(BF16)
            vw = win_ref[pl.ds(g + gg, wb, stride=2 * g), :].astype(BF16)
            kn = _pad_rows(kwn_ref[:, gg * hd:(gg + 1) * hd], page).astype(BF16)
            vn = _pad_rows(vwn_ref[:, gg * hd:(gg + 1) * hd], page).astype(BF16)
            dw = t - (offset - wb + lax.broadcasted_iota(I32, (1, wb), 1))
            s = jnp.concatenate([_dot_nt(qs_ref[gg], kw) - sl_ref[gg] * dw.astype(F32),
                                 _dot_nt(qs_ref[gg], kn) - sl_ref[gg] * dn.astype(F32)], axis=1)
            mask = jnp.concatenate([(dw >= 0) & (dw <= WINDOW), (dn >= 0) & (dn <= WINDOW) & (lane < ts)], axis=1)
            _flash_update(s, mask, [vw, vn], m_ref.at[gg], l_ref.at[gg], acc_ref.at[gg])
            o_w = _flash_final(l_ref.at[gg], acc_ref.at[gg])
            gs = gsig[:, gg * LANES:(gg + 1) * LANES]
            o = _gate_cols(gs, 0, r) * oc_ref[gg] + _gate_cols(gs, 1, r) * o_s + _gate_cols(gs, 2, r) * o_w
            outs.append(_unstack_heads(o, r, ts))
        o_ref[...] = jnp.concatenate(outs, axis=1)


def nsa_sample(hab, kc, vc, pool, win, page_table, slopes, cols, *, n_p, bs, ts, page, g, r):
    hd = HEAD_DIM
    n_pages = page_table.shape[1]
    offset = n_pages * page
    wb = win.shape[1] // (2 * g)
    assert offset % SEL_BLOCK == 0 and offset - wb >= 0 and wb == WINDOW
    nb = kc.shape[2]
    n_sel = -(-(offset + ts) // SEL_BLOCK)
    ns_pad = -(-n_sel // LANES) * LANES
    pp = _pick(n_pages, (16, 8, 4, 2, 1))
    n_chunks = n_pages // pp
    rb = n_p // ts
    body = functools.partial(_nsa_s_body, pp=pp, n_chunks=n_chunks, ts=ts, page=page, offset=offset, wb=wb,
                             nb=nb, n_sel=n_sel, ns_pad=ns_pad, k_top=min(SEL_TOPK, n_sel),
                             scale=hd ** -0.5, g=g, r=r)

    def row_spec(width, col0):
        return pl.BlockSpec((ts, width), lambda b, c, pt: (rb + b, col0 // width))

    def page_spec(i):
        return pl.BlockSpec((1, page * 4 * g, hd), lambda b, c, pt: (pt[b, c * pp + i], 0, 0))

    in_specs = [row_spec(g * r * hd, cols["q"]), row_spec(g * LANES, cols["gt"]),
                row_spec(g * hd, cols["ks"]), row_spec(g * hd, cols["vs"]),
                row_spec(g * hd, cols["kw"]), row_spec(g * hd, cols["vw"]),
                pl.BlockSpec((None, g, nb, hd), lambda b, c, pt: (b, 0, 0, 0)),
                pl.BlockSpec((None, g, nb, hd), lambda b, c, pt: (b, 0, 0, 0)),
                pl.BlockSpec((None, wb * 2 * g, hd), lambda b, c, pt: (b, 0, 0)),
                pl.BlockSpec((g, r * ts, 1), lambda b, c, pt: (0, 0, 0))]
    in_specs += [page_spec(i) for i in range(pp)]
    grid_spec = pltpu.PrefetchScalarGridSpec(
        num_scalar_prefetch=1, grid=(bs, n_chunks), in_specs=in_specs,
        out_specs=pl.BlockSpec((ts, g * r * hd), lambda b, c, pt: (b, 0)),
        scratch_shapes=[pltpu.VMEM((g, r * ts, hd), BF16), pltpu.VMEM((g, r * ts, hd), F32),
                        pltpu.VMEM((g, r * ts, ns_pad), BF16), pltpu.VMEM((g, r * ts, 1), F32),
                        pltpu.VMEM((g, r * ts, 1), F32), pltpu.VMEM((g, r * ts, hd), F32)])
    return pl.pallas_call(
        body, grid_spec=grid_spec, out_shape=SDS((bs * ts, g * r * hd), F32),
        compiler_params=_cparams(("parallel", "arbitrary")), name="nsa_sample",
    )(page_table, hab, hab, hab, hab, hab, hab, kc, vc, win,
      jnp.repeat(slopes, ts, axis=1).reshape(g, r * ts, 1), *([pool] * pp))


def _cmp_s_body(pt_ref, pek_ref, w1k_ref, w2k_ref, pev_ref, w1v_ref, w2v_ref, *rest, pp, page, g):
    pages = rest[:pp]
    kc_ref, vc_ref, stage_ref, x_ref = rest[pp:]
    del pt_ref
    hd = HEAD_DIM
    rows = pp * (page // CMP_BLOCK)
    for cg in range(2 * g):
        for i in range(pp):
            stage_ref[cg, i * page:(i + 1) * page, :] = pages[i][0, pl.ds(cg, page, stride=4 * g), :]
    for comp, (pe_ref, w1_ref, w2_ref, out_ref) in enumerate(
            ((pek_ref, w1k_ref, w2k_ref, kc_ref), (pev_ref, w1v_ref, w2v_ref, vc_ref))):
        for l in range(CMP_BLOCK):
            x = jnp.concatenate([stage_ref[comp * g + gg, pl.ds(l, rows, stride=CMP_BLOCK), :]
                                 for gg in range(g)], axis=0)
            x_ref[:, l * hd:(l + 1) * hd] = (x + pe_ref[l:l + 1, :]).astype(BF16)
        out = _dot(_gelu(_dot(x_ref[...], w1_ref[...])).astype(BF16), w2_ref[...])
        for gg in range(g):
            out_ref[gg] = out[gg * rows:(gg + 1) * rows]


def compress_sample(pool, page_table, p, *, page, g):
    hd = HEAD_DIM
    bs, n_pages = page_table.shape
    pp = _pick(n_pages, (16, 8, 4))
    rows = pp * (page // CMP_BLOCK)
    nb = n_pages * (page // CMP_BLOCK)
    hid = p["w1_k"].shape[1]
    body = functools.partial(_cmp_s_body, pp=pp, page=page, g=g)
    once = pl.Buffered(1)

    def wspecs():
        return [pl.BlockSpec((CMP_BLOCK, hd), lambda b, c, pt: (0, 0)),
                pl.BlockSpec((CMP_BLOCK * hd, hid), lambda b, c, pt: (0, 0), pipeline_mode=once),
                pl.BlockSpec((hid, hd), lambda b, c, pt: (0, 0))]

    def page_spec(i):
        return pl.BlockSpec((1, page * 4 * g, hd), lambda b, c, pt: (pt[b, c * pp + i], 0, 0))

    grid_spec = pltpu.PrefetchScalarGridSpec(
        num_scalar_prefetch=1, grid=(bs, n_pages // pp),
        in_specs=wspecs() + wspecs() + [page_spec(i) for i in range(pp)],
        out_specs=[pl.BlockSpec((None, g, rows, hd), lambda b, c, pt: (b, 0, c, 0)),
                   pl.BlockSpec((None, g, rows, hd), lambda b, c, pt: (b, 0, c, 0))],
        scratch_shapes=[pltpu.VMEM((2 * g, pp * page, hd), F32), pltpu.VMEM((g * rows, CMP_BLOCK * hd), BF16)])
    return pl.pallas_call(
        body, grid_spec=grid_spec, out_shape=[SDS((bs, g, nb, hd), F32), SDS((bs, g, nb, hd), F32)],
        compiler_params=_cparams(("parallel", "arbitrary")), name="nsa_compress_sample",
    )(page_table, p["pe_k"], p["w1_k"].astype(BF16), p["w2_k"].astype(BF16),
      p["pe_v"], p["w1_v"].astype(BF16), p["w2_v"].astype(BF16), *([pool] * pp))


def _ret_body(q_ref, k_ref, v_ref, g_ref, gn_ref, s0_ref, dm_ref, cr_ref, kd_ref, cd_ref, o_ref, s_ref,
              st_ref, *, scale, n_chunks, nh):
    c = pl.program_id(1)
    dk = HEAD_DIM

    @pl.when(c == 0)
    def _():
        st_ref[...] = s0_ref[...]

    for h in range(nh):
        cs = slice(h * dk, (h + 1) * dk)
        qb = q_ref[:, cs].astype(BF16)
        ks = k_ref[:, cs] * scale
        kb = ks.astype(BF16)
        vb = v_ref[:, cs].astype(BF16)
        st = st_ref[h]
        att = _dot_nt(qb, kb) * dm_ref[h]
        o = _dot(att.astype(BF16), vb) + _dot(qb, st.astype(BF16)) * cr_ref[h]
        st_ref[h] = st * cd_ref[h] + _dot_tn((ks * kd_ref[h]).astype(BF16), vb)
        mu = jnp.mean(o, axis=-1, keepdims=True)
        var = jnp.mean(jnp.square(o - mu), axis=-1, keepdims=True)
        on = (o - mu) * lax.rsqrt(var + GN_EPS) * gn_ref[:, cs]
        gate = g_ref[:, cs]
        o_ref[:, cs] = ((gate * (1.0 / (1.0 + jnp.exp(-gate)))) * on).astype(o_ref.dtype)

    @pl.when(c == n_chunks - 1)
    def _():
        s_ref[...] = st_ref[...]


def retention(hab, gn_w, s0, cols, *, row0, b, t, nh):
    dk = HEAD_DIM
    w = nh * dk
    ch = RET_CHUNK if t % RET_CHUNK == 0 else t
    n_chunks = t // ch
    lg = jnp.log1p(-jnp.exp2(-5.0 - jnp.arange(nh, dtype=F32)))
    i = jnp.arange(ch, dtype=F32)
    diff = i[:, None] - i[None, :]
    dmask = jnp.where(diff >= 0, jnp.exp(jnp.maximum(diff, 0.0)[None] * lg[:, None, None]), 0.0)
    cross = jnp.exp((i + 1.0)[None, :] * lg[:, None]).reshape(nh, ch, 1)
    kdec = jnp.exp((ch - 1.0 - i)[None, :] * lg[:, None]).reshape(nh, ch, 1)
    cdec = jnp.exp(ch * lg).reshape(nh, 1, 1)
    rb = row0 // ch
    body = functools.partial(_ret_body, scale=dk ** -0.5, n_chunks=n_chunks, nh=nh)

    def col_spec(col0):
        assert col0 % w == 0
        return pl.BlockSpec((ch, w), lambda bb, c: (rb + bb * n_chunks + c, col0 // w))

    def whole(shape):
        return pl.BlockSpec(shape, lambda bb, c: (0,) * len(shape))

    return pl.pallas_call(
        body,
        grid=(b, n_chunks),
        in_specs=[col_spec(cols["rq"]), col_spec(cols["rk"]), col_spec(cols["rv"]), col_spec(cols["rg"]),
                  whole((1, w)),
                  pl.BlockSpec((None, nh, dk, dk), lambda bb, c: (bb, 0, 0, 0)),
                  whole((nh, ch, ch)), whole((nh, ch, 1)), whole((nh, ch, 1)), whole((nh, 1, 1))],
        out_specs=[pl.BlockSpec((ch, w), lambda bb, c: (bb * n_chunks + c, 0)),
                   pl.BlockSpec((None, nh, dk, dk), lambda bb, c: (bb, 0, 0, 0))],
        out_shape=[SDS((b * t, w), BF16 if ch % 16 == 0 else F32), SDS((b, nh, dk, dk), F32)],
        scratch_shapes=[pltpu.VMEM((nh, dk, dk), F32)],
        compiler_params=_cparams(("parallel", "arbitrary")),
        name="retention",
    )(hab, hab, hab, hab, gn_w.reshape(1, w), s0, dmask, cross, kdec, cdec)


def _logf_body(x_ref, b_ref, o_ref):
    x = x_ref[...] + b_ref[...]
    o_ref[...] = -(jnp.maximum(-x, 0.0) + jnp.log1p(jnp.exp(-jnp.abs(x))))


def fox_logf(hc, b_pad, col0):
    n = hc.shape[0]
    tm = _pick(n, (768, 512, 256, 128, 64, 32, 16, 8))
    return pl.pallas_call(
        _logf_body,
        grid=(n // tm,),
        in_specs=[pl.BlockSpec((tm, LANES), lambda i: (i, col0 // LANES)),
                  pl.BlockSpec((1, LANES), lambda i: (0, 0))],
        out_specs=pl.BlockSpec((tm, LANES), lambda i: (i, 0)),
        out_shape=SDS((n, LANES), F32),
        compiler_params=_cparams(("parallel",)),
        name="fox_logf",
    )(hc, b_pad)


def _cumsum_rows_body(x_ref, o_ref, hi_ref, mid_ref, lo_ref, carry_ref):
    @pl.when(pl.program_id(1) == 0)
    def _():
        carry_ref[...] = jnp.zeros_like(carry_ref)

    tc = x_ref.shape[0]
    tri = jnp.where(lax.broadcasted_iota(I32, (tc, tc), 1) <= lax.broadcasted_iota(I32, (tc, tc), 0),
                    1.0, 0.0).astype(BF16)
    hi, mid, lo = _split3(x_ref[...])
    f = (_dot(tri, hi) + _dot(tri, mid)) + _dot(tri, lo) + carry_ref[...]
    o_ref[...] = f
    hi_ref[...], mid_ref[...], lo_ref[...] = _split3(f)
    carry_ref[...] = f[tc - 1:tc, :]


def cumsum_rows(x, *, b, t):
    tc = 128 if t % 128 == 0 else t
    nc = t // tc
    spec = pl.BlockSpec((tc, LANES), lambda bb, c: (bb * nc + c, 0))
    return pl.pallas_call(
        _cumsum_rows_body,
        grid=(b, nc),
        in_specs=[spec],
        out_specs=[spec, spec, spec, spec],
        out_shape=[SDS((b * t, LANES), F32)] + [SDS((b * t, LANES), BF16)] * 3,
        scratch_shapes=[pltpu.VMEM((1, LANES), F32)],
        compiler_params=_cparams(("parallel", "arbitrary")),
        name="fox_cumsum",
    )(x)


def _fox_p_body(q_ref, k_ref, v_ref, fq_ref, fa_ref, bs_ref, o_ref, m_ref, l_ref, acc_ref, *, tq, tk, scale, r):
    qi = pl.program_id(2)
    hd = HEAD_DIM
    rows = r * tq
    qs = (_stack_heads(q_ref[...], r, hd) * scale).astype(BF16)
    fq = jnp.concatenate([fq_ref[:, i:i + 1] for i in range(r)], axis=0)
    hi, mid, lo = _split3(fq)
    lane = lax.broadcasted_iota(I32, (rows, LANES), 1)
    qb = (bs_ref[...] + jnp.where(lane == 3 * r, hi.astype(F32), 0.0) + jnp.where(lane == 3 * r + 1, mid.astype(F32), 0.0)
          + jnp.where(lane == 3 * r + 2, lo.astype(F32), 0.0)).astype(BF16)
    qa = jnp.concatenate([qs, qb], axis=1)
    t_row = qi * tq + jnp.concatenate([lax.broadcasted_iota(I32, (1, tq), 1)] * r, axis=1)

    def tile(j, masked):
        start = pl.multiple_of(j * tk, tk)
        ka = jnp.concatenate([k_ref[pl.ds(start, tk), :].astype(BF16), fa_ref[pl.ds(start, tk), :]], axis=1)
        v = v_ref[pl.ds(start, tk), :].astype(BF16)
        s = _dot_nt(ka, qa)
        if masked:
            mask = start + lax.broadcasted_iota(I32, (tk, 1), 0) <= t_row
            s = jnp.where(mask, s, NEG)
        m_prev = m_ref[...]
        m_new = jnp.maximum(m_prev, jnp.max(s, axis=0, keepdims=True))
        alpha = jnp.exp(m_prev - m_new)
        p = jnp.exp(s - m_new)
        if masked:
            p = jnp.where(mask, p, 0.0)
        l_ref[...] = alpha * l_ref[...] + jnp.sum(p, axis=0, keepdims=True)
        acc_ref[...] = alpha * acc_ref[...] + _dot_tn(v, p.astype(BF16))
        m_ref[...] = m_new

    def full_tile(j, carry):
        tile(j, False)
        return carry

    _flash_init(m_ref, l_ref, acc_ref)
    n_full = (qi * tq) // tk
    lax.fori_loop(0, n_full, full_tile, 0)
    for jj in range(tq // tk):
        tile(n_full + jj, True)
    o = acc_ref[...] / jnp.maximum(l_ref[...], 1e-30)
    o_ref[...] = jnp.concatenate([o[:, i * tq:(i + 1) * tq].T for i in range(r)], axis=1).astype(BF16)


def fox_prompt(hc, f, f_split, cols, *, b, t, g, r):
    hd = HEAD_DIM
    tq = _pick(t, (512, 256, 128, t))
    tk = _pick(tq, (512, 256, 128, tq))
    nq = t // tq
    qw = r * hd
    fq = f.transpose(0, 2, 1, 3)
    fa = jnp.concatenate([piece.transpose(0, 2, 1, 3) for piece in f_split]
                         + [jnp.ones((b, g, t, 3), BF16), jnp.zeros((b, g, t, LANES - 3 * r - 3), BF16)],
                         axis=-1)
    rows = np.arange(r * tq)[:, None] // tq
    lanes = np.arange(LANES)[None, :]
    bsel = jnp.asarray(np.where((lanes < 3 * r) & (lanes % r == rows), -1.0, 0.0), F32)
    body = functools.partial(_fox_p_body, tq=tq, tk=tk, scale=hd ** -0.5, r=r)
    return pl.pallas_call(
        body,
        grid=(b, g, nq),
        in_specs=[pl.BlockSpec((tq, qw), lambda bb, gg, qi: (bb * nq + qi, gg)),
                  pl.BlockSpec((t, hd), lambda bb, gg, qi: (bb, cols["k"] // hd + gg)),
                  pl.BlockSpec((t, hd), lambda bb, gg, qi: (bb, cols["v"] // hd + gg)),
                  pl.BlockSpec((None, None, tq, r), lambda bb, gg, qi: (bb, gg, qi, 0)),
                  pl.BlockSpec((None, None, t, LANES), lambda bb, gg, qi: (bb, gg, 0, 0)),
                  pl.BlockSpec((r * tq, LANES), lambda bb, gg, qi: (0, 0))],
        out_specs=pl.BlockSpec((tq, qw), lambda bb, gg, qi: (bb * nq + qi, gg)),
        out_shape=SDS((b * t, g * qw), BF16),
        scratch_shapes=[pltpu.VMEM((1, r * tq), F32), pltpu.VMEM((1, r * tq), F32),
                        pltpu.VMEM((hd, r * tq), F32)],
        compiler_params=_cparams(("parallel", "parallel", "arbitrary")),
        name="fox_prompt",
    )(hc, hc, hc, fq, fa, bsel)


def _fox_f_body(pt_ref, new_ref, *rest, pp, n_chunks):
    pages = rest[:pp]
    fk_ref, fn_ref, carry_ref = rest[pp:]
    del pt_ref
    c = pl.program_id(1)
    page = pages[0].shape[2]

    @pl.when(c == 0)
    def _():
        carry_ref[...] = jnp.zeros_like(carry_ref)

    ut = jnp.where(lax.broadcasted_iota(I32, (page, page), 0) <= lax.broadcasted_iota(I32, (page, page), 1),
                   1.0, 0.0).astype(BF16)

    def csum(x, carry):
        hi, mid, lo = _split3(x)
        return (_dot(hi, ut) + _dot(mid, ut)) + _dot(lo, ut) + carry

    carry = carry_ref[...]
    for i in range(pp):
        f = csum(pages[i][0], carry)
        fk_ref[:, i * page:(i + 1) * page] = f
        carry = f[:, page - 1:page]
    carry_ref[...] = carry

    @pl.when(c == n_chunks - 1)
    def _():
        fn_ref[...] = csum(new_ref[...], carry)


def fox_f_sample(logf_pool_t, new_t, page_table):
    bs, n_pages = page_table.shape
    _, nh, page = logf_pool_t.shape
    pp = _pick(n_pages, (16, 8, 4, 2, 1))
    n_chunks = n_pages // pp
    body = functools.partial(_fox_f_body, pp=pp, n_chunks=n_chunks)
    in_specs = [pl.BlockSpec((None, nh, page), lambda b, c, pt: (b, 0, 0))]
    in_specs += [pl.BlockSpec((1, nh, page), functools.partial(lambda b, c, pt, i: (pt[b, c * pp + i], 0, 0), i=i))
                 for i in range(pp)]
    grid_spec = pltpu.PrefetchScalarGridSpec(
        num_scalar_prefetch=1, grid=(bs, n_chunks), in_specs=in_specs,
        out_specs=[pl.BlockSpec((None, nh, pp * page), lambda b, c, pt: (b, 0, c)),
                   pl.BlockSpec((None, nh, page), lambda b, c, pt: (b, 0, 0))],
        scratch_shapes=[pltpu.VMEM((nh, 1), F32)])
    return pl.pallas_call(
        body, grid_spec=grid_spec,
        out_shape=[SDS((bs, nh, n_pages * page), F32), SDS((bs, nh, page), F32)],
        compiler_params=_cparams(("parallel", "arbitrary")), name="fox_f_sample",
    )(page_table, new_t, *([logf_pool_t] * pp))


def _fox_s_body(pt_ref, q_ref, kn_ref, vn_ref, fq_ref, fk_ref, fn_ref, *rest, pp, n_chunks, ts, page, offset,
                scale, g, r):
    pages = rest[:pp]
    o_ref = rest[pp]
    qs_ref, m_ref, l_ref, acc_ref = rest[pp + 1:]
    del pt_ref
    c = pl.program_id(1)
    hd = HEAD_DIM

    @pl.when(c == 0)
    def _():
        for gg in range(g):
            qs_ref[gg] = (_stack_heads(q_ref[:, gg * r * hd:(gg + 1) * r * hd], r, hd) * scale).astype(BF16)
            _flash_init(m_ref.at[gg], l_ref.at[gg], acc_ref.at[gg])

    def fk_rows(f, gg):
        return jnp.concatenate([jnp.broadcast_to(f[gg * r + i:gg * r + i + 1, :], (ts, f.shape[1]))
                                for i in range(r)], axis=0)

    fk = fk_ref[...]
    for gg in range(g):
        ks = [pages[i][0, pl.ds(gg, page, stride=2 * g), :].astype(BF16) for i in range(pp)]
        vs = [pages[i][0, pl.ds(g + gg, page, stride=2 * g), :].astype(BF16) for i in range(pp)]
        s = jnp.concatenate([_dot_nt(qs_ref[gg], k) for k in ks], axis=1) + fq_ref[gg] - fk_rows(fk, gg)
        _flash_update(s, None, vs, m_ref.at[gg], l_ref.at[gg], acc_ref.at[gg])

    @pl.when(c == n_chunks - 1)
    def _():
        t = _rep_rows(offset + lax.broadcasted_iota(I32, (ts, 1), 0), r)
        lane = lax.broadcasted_iota(I32, (1, page), 1)
        outs = []
        for gg in range(g):
            k = _pad_rows(kn_ref[:, gg * hd:(gg + 1) * hd], page).astype(BF16)
            v = _pad_rows(vn_ref[:, gg * hd:(gg + 1) * hd], page).astype(BF16)
            s = _dot_nt(qs_ref[gg], k) + fq_ref[gg] - fk_rows(fn_ref[...], gg)
            _flash_update(s, (offset + lane <= t) & (lane < ts), [v], m_ref.at[gg], l_ref.at[gg], acc_ref.at[gg])
            outs.append(_unstack_heads(_flash_final(l_ref.at[gg], acc_ref.at[gg]), r, ts))
        o_ref[...] = jnp.concatenate(outs, axis=1)


def fox_sample(hc, fq, fk, fn, pool, page_table, cols, *, n_p, bs, ts, page, g, r):
    hd = HEAD_DIM
    n_pages = page_table.shape[1]
    nh = g * r
    offset = n_pages * page
    pp = _pick(n_pages, (16, 8, 4, 2, 1))
    n_chunks = n_pages // pp
    rb = n_p // ts
    body = functools.partial(_fox_s_body, pp=pp, n_chunks=n_chunks, ts=ts, page=page, offset=offset,
                             scale=hd ** -0.5, g=g, r=r)

    def row_spec(width, col0):
        return pl.BlockSpec((ts, width), lambda b, c, pt: (rb + b, col0 // width))

    def page_spec(i):
        return pl.BlockSpec((1, page * 2 * g, hd), lambda b, c, pt: (pt[b, c * pp + i], 0, 0))

    in_specs = [row_spec(nh * hd, cols["q"]), row_spec(g * hd, cols["k"]), row_spec(g * hd, cols["v"]),
                pl.BlockSpec((None, g, r * ts, 1), lambda b, c, pt: (b, 0, 0, 0)),
                pl.BlockSpec((None, nh, pp * page), lambda b, c, pt: (b, 0, c)),
                pl.BlockSpec((None, nh, page), lambda b, c, pt: (b, 0, 0))]
    in_specs += [page_spec(i) for i in range(pp)]
    grid_spec = pltpu.PrefetchScalarGridSpec(
        num_scalar_prefetch=1, grid=(bs, n_chunks), in_specs=in_specs,
        out_specs=pl.BlockSpec((ts, nh * hd), lambda b, c, pt: (b, 0)),
        scratch_shapes=[pltpu.VMEM((g, r * ts, hd), BF16), pltpu.VMEM((g, r * ts, 1), F32),
                        pltpu.VMEM((g, r * ts, 1), F32), pltpu.VMEM((g, r * ts, hd), F32)])
    return pl.pallas_call(
        body, grid_spec=grid_spec, out_shape=SDS((bs * ts, nh * hd), F32),
        compiler_params=_cparams(("parallel", "arbitrary")), name="fox_sample",
    )(page_table, hc, hc, hc, fq, fk, fn, *([pool] * pp))


def _ab_layout(nh_a, g, r, nh_r):
    hd = HEAD_DIM
    qa, kv = nh_a * hd, g * hd
    sizes = [("q", qa), ("kc", kv), ("vc", kv), ("ks", kv), ("vs", kv), ("kw", kv), ("vw", kv),
             ("gt_src", nh_a * 3), ("rq", nh_r * hd), ("rk", nh_r * hd), ("rv", nh_r * hd), ("rg", nh_r * hd)]
    src = {}
    pos = 0
    for name, w in sizes:
        src[name] = pos
        pos += w
    order = ["q", "rq", "rk", "rv", "rg", "kc", "vc", "ks", "vs", "kw", "vw"]
    widths = dict(sizes)
    idx = []
    cols = {}
    for name in order:
        cols[name] = len(idx)
        idx += list(range(src[name], src[name] + widths[name]))
    cols["gt"] = len(idx)
    for gg in range(g):
        blk = [-1] * LANES
        for c in range(3):
            for i in range(r):
                blk[c * r + i] = src["gt_src"] + (gg * r + i) * 3 + c
        idx += blk
    return np.array(idx, np.int32), cols


def _gather_cols(w, idx):
    parts = []
    i, n = 0, len(idx)
    while i < n:
        j = i + 1
        if idx[i] < 0:
            while j < n and idx[j] < 0:
                j += 1
            parts.append(jnp.zeros((w.shape[0], j - i), w.dtype))
        else:
            while j < n and idx[j] == idx[j - 1] + 1:
                j += 1
            parts.append(w[:, int(idx[i]):int(idx[i]) + (j - i)])
        i = j
    return jnp.concatenate(parts, axis=1)


def _even_layer(x, n_p, b, t, bs, ts, p, cache_kv, cache_win, state, page_table):
    hd = HEAD_DIM
    d = x.shape[1]
    nh_a = d // (2 * hd)
    g = nh_a // 4
    r = nh_a // g
    nh_r = d // (2 * hd)
    idx, cols = _ab_layout(nh_a, g, r, nh_r)
    tn = 768
    ncol = -(-len(idx) // tn) * tn
    idx = np.concatenate([idx, np.full(ncol - len(idx), -1, np.int32)])
    w_in = _gather_cols(p["w_in"], idx).astype(BF16)
    hab, _ = mm_norm(x, p["norm"], w_in, tn)

    kv4 = 4 * g * hd
    c_rows = cols["kc"]
    c_win = cols["kw"]
    new_rows_p = hab[:n_p, c_rows:c_rows + kv4].reshape(b, t, 4, g, hd)
    new_rows_s = hab[n_p:, c_rows:c_rows + kv4].reshape(bs, ts, 4, g, hd)
    new_win_p = hab[:n_p, c_win:c_win + 2 * g * hd].reshape(b, t, 2, g, hd)
    new_win_s = hab[n_p:, c_win:c_win + 2 * g * hd].reshape(bs, ts, 2, g, hd)
    win_state_p = new_win_p[:, -min(WINDOW, t):]
    wb = cache_win.shape[1]
    win_state_s = jnp.concatenate([cache_win, new_win_s], axis=1)[:, -wb:]

    slopes = jnp.exp2(-8.0 * (jnp.arange(nh_a, dtype=F32) + 1.0) / nh_a).reshape(g, r)
    w1k, w2k = p["w1_k"].astype(BF16), p["w2_k"].astype(BF16)
    w1v, w2v = p["w1_v"].astype(BF16), p["w2_v"].astype(BF16)

    def cmp_pair(blocks):
        bb, nb = blocks.shape[:2]
        flat = blocks.transpose(3, 0, 1, 4, 2, 5).reshape(2, bb * nb * g, CMP_BLOCK * hd)
        kc = compress(flat[0], p["pe_k"].reshape(-1), w1k, w2k).reshape(bb, nb, g, hd).transpose(0, 2, 1, 3)
        vc = compress(flat[1], p["pe_v"].reshape(-1), w1v, w2v).reshape(bb, nb, g, hd).transpose(0, 2, 1, 3)
        return kc, vc

    nb_p = t // CMP_BLOCK
    kc_p, vc_p = cmp_pair(new_rows_p[:, :nb_p * CMP_BLOCK, 0:2].reshape(b, nb_p, CMP_BLOCK, 2, g, hd))
    o_a_p = nsa_prompt(hab, kc_p, vc_p, slopes, cols, b=b, t=t, g=g, r=r)

    page = cache_kv.shape[1]
    n_pages = page_table.shape[1]
    nb_s = (n_pages * page + ts) // CMP_BLOCK
    assert nb_s * CMP_BLOCK == n_pages * page and page % CMP_BLOCK == 0
    pool = cache_kv.reshape(cache_kv.shape[0], page * 4 * g, hd)
    kc_s, vc_s = compress_sample(pool, page_table, p, page=page, g=g)
    o_a_s = nsa_sample(hab, kc_s, vc_s, pool, cache_win.reshape(bs, wb * 2 * g, hd), page_table, slopes, cols,
                       n_p=n_p, bs=bs, ts=ts, page=page, g=g, r=r)

    o_b_p, st_p = retention(hab, p["gn"], jnp.zeros((b, nh_r, hd, hd), F32), cols, row0=0, b=b, t=t, nh=nh_r)
    o_b_s, st_s = retention(hab, p["gn"], state, cols, row0=n_p, b=bs, t=ts, nh=nh_r)

    o = jnp.concatenate([jnp.concatenate([o_a_p, o_b_p.astype(BF16)], axis=1),
                         jnp.concatenate([o_a_s.astype(BF16), o_b_s.astype(BF16)], axis=1)], axis=0)
    x = mm_res(o, p["w_out"].astype(BF16), x)
    return x, (new_rows_p, win_state_p, st_p, new_rows_s, win_state_s, st_s)


def _odd_layer(x, n_p, b, t, bs, ts, p, cache_kv, cache_logf, page_table):
    hd = HEAD_DIM
    d = x.shape[1]
    nh = d // hd
    g = nh // 4
    r = nh // g
    cols = {"q": 0, "k": nh * hd, "v": (nh + g) * hd, "f": (nh + 2 * g) * hd}
    ncol_src = p["w_in"].shape[1]
    tn = 640
    ncol = -(-(cols["f"] + LANES) // tn) * tn
    w_in = jnp.pad(p["w_in"], ((0, 0), (0, ncol - ncol_src))).astype(BF16)
    hc, _ = mm_norm(x, p["norm"], w_in, tn)

    new_rows_p = hc[:n_p, cols["k"]:cols["f"]].reshape(b, t, 2, g, hd)
    new_rows_s = hc[n_p:, cols["k"]:cols["f"]].reshape(bs, ts, 2, g, hd)
    b_pad = jnp.pad(p["b_f"].astype(F32), (0, LANES - nh)).reshape(1, LANES)
    logf = fox_logf(hc, b_pad, cols["f"])
    new_logf_p = logf[:n_p, :nh].reshape(b, t, nh)
    new_logf_s = logf[n_p:, :nh].reshape(bs, ts, nh)

    f_p, *f_split = [a[:, :nh].reshape(b, t, g, r) for a in cumsum_rows(logf, b=b, t=t)]
    o_p = fox_prompt(hc, f_p, f_split, cols, b=b, t=t, g=g, r=r)

    page = cache_kv.shape[1]
    new_t = jnp.pad(new_logf_s.transpose(0, 2, 1), ((0, 0), (0, 0), (0, page - ts)))
    fk, fn = fox_f_sample(cache_logf.transpose(0, 2, 1), new_t, page_table)
    fq = fn[:, :, :ts].reshape(bs, g, r * ts, 1)
    o_s = fox_sample(hc, fq, fk, fn, cache_kv.reshape(cache_kv.shape[0], page * 2 * g, hd), page_table, cols,
                     n_p=n_p, bs=bs, ts=ts, page=page, g=g, r=r)

    x = mm_res(jnp.concatenate([o_p, o_s.astype(BF16)], axis=0), p["w_out"].astype(BF16), x)
    return x, (new_rows_p, new_logf_p, new_rows_s, new_logf_s)


def _peer_layer(x, nw, wq, k1, k2, u, v):
    q, xn = mm_norm(x, nw, wq.astype(BF16), _pick(wq.shape[1], (1024, 512, 256, 128)))
    s1t, s2t, st = peer_score(q, k1, k2)
    return peer_dense(xn, u.astype(BF16), v.T.astype(BF16), s1t, s2t, st, x)


def kernel(x_prompt, x_sample, cache_nsa_kv, cache_nsa_win, state_ret, cache_fox_kv, cache_fox_logf,
           page_table, norm_mix, norm_ffn, norm_final, w_in_ab, w_out_ab, cmp_pe_k, cmp_w1_k, cmp_w2_k,
           cmp_pe_v, cmp_w1_v, cmp_w2_v, ret_gn, w_in_c, b_forget, w_out_c, peer_wq, peer_k1, peer_k2,
           peer_u, peer_v):
    b, t, d = x_prompt.shape
    bs, ts, _ = x_sample.shape
    n_p, n_s = b * t, bs * ts
    depth = norm_mix.shape[0]
    x = jnp.concatenate([x_prompt.reshape(n_p, d), x_sample.reshape(n_s, d)], axis=0)
    even, odd = [], []
    for l in range(depth):
        if l % 2 == 0:
            e = l // 2
            p = dict(norm=norm_mix[l], w_in=w_in_ab[e], w_out=w_out_ab[e], pe_k=cmp_pe_k[e], w1_k=cmp_w1_k[e],
                     w2_k=cmp_w2_k[e], pe_v=cmp_pe_v[e], w1_v=cmp_w1_v[e], w2_v=cmp_w2_v[e], gn=ret_gn[e])
            x, outs = _even_layer(x, n_p, b, t, bs, ts, p, cache_nsa_kv[e], cache_nsa_win[e], state_ret[e],
                                  page_table)
            even.append(outs)
        else:
            o = l // 2
            p = dict(norm=norm_mix[l], w_in=w_in_c[o], b_f=b_forget[o], w_out=w_out_c[o])
            x, outs = _odd_layer(x, n_p, b, t, bs, ts, p, cache_fox_kv[o], cache_fox_logf[o], page_table)
            odd.append(outs)
        x = _peer_layer(x, norm_ffn[l], peer_wq[l], peer_k1[l], peer_k2[l], peer_u[l], peer_v[l])
    y = rms_final(x, norm_final)
    y_prompt = y[:n_p].reshape(b, t, d)
    y_sample = y[n_p:].reshape(bs, ts, d)

    def stack(group, i):
        return jnp.stack([o[i] for o in group])

    return (y_prompt, y_sample, stack(even, 0), stack(even, 1), stack(even, 2), stack(odd, 0), stack(odd, 1),
            stack(even, 3), stack(even, 4), stack(even, 5), stack(odd, 2), stack(odd, 3))
```

```python
import functools
import math

import numpy as np
import jax
import jax.numpy as jnp
from jax import lax
from jax.experimental import pallas as pl
from jax.experimental.pallas import tpu as pltpu

F32 = jnp.float32
BF16 = jnp.bfloat16
I32 = jnp.int32

HEAD_DIM = 128
CMP_BLOCK = 64
SEL_BLOCK = 64
SEL_TOPK = 16
WINDOW = 512
CMP_HIDDEN = 256
RET_CHUNK = 128
Q_BLOCK = 128
PEER_HEADS = 8
PEER_DK = 256
PEER_TOPK = 16
EPS = 1e-6
GN_EPS = 1e-5
NEG = -1e30

LANES = 128
SUBLANES = 8
VMEM_LIMIT = 56 * 1024 * 1024

SDS = jax.ShapeDtypeStruct


def _cparams(sem):
    return pltpu.CompilerParams(dimension_semantics=sem, vmem_limit_bytes=VMEM_LIMIT)


def _dot(a, b):
    return jnp.dot(a, b, preferred_element_type=F32)


def _dot_nt(a, b):
    return lax.dot_general(a, b, (((1,), (1,)), ((), ())), preferred_element_type=F32)


def _dot_tn(a, b):
    return lax.dot_general(a, b, (((0,), (0,)), ((), ())), preferred_element_type=F32)


def _pick(n, cands):
    for c in cands:
        if c <= n and n % c == 0:
            return c
    raise ValueError(f"no tile for {n} in {cands}")


def _gelu(x):
    c = math.sqrt(2.0 / math.pi)
    return x * (0.5 * (1.0 + jnp.tanh(c * (x + 0.044715 * (x * x * x)))))


def _split3(x):
    hi = x.astype(BF16)
    r1 = x - hi.astype(F32)
    mid = r1.astype(BF16)
    lo = (r1 - mid.astype(F32)).astype(BF16)
    return hi, mid, lo


def _stack_heads(x, nh, hd):
    return jnp.concatenate([x[:, r * hd:(r + 1) * hd] for r in range(nh)], axis=0)


def _unstack_heads(x, nh, t):
    return jnp.concatenate([x[r * t:(r + 1) * t, :] for r in range(nh)], axis=1)


def _rep_rows(x, k):
    return jnp.concatenate([x] * k, axis=0)


def _mm_norm_body(x_ref, nw_ref, w_ref, o_ref, xn_ref):
    @pl.when(pl.program_id(1) == 0)
    def _():
        x = x_ref[...]
        ms = jnp.mean(x * x, axis=-1, keepdims=True)
        xn_ref[...] = (x * lax.rsqrt(ms + EPS) * nw_ref[...]).astype(BF16)

    o_ref[...] = _dot(xn_ref[...], w_ref[...])


def mm_norm(x, nw, w_bf, tn):
    n, d = x.shape
    nn = w_bf.shape[1]
    tm = _pick(n, (768, 512, 256, 128, 64, 32, 16))
    return pl.pallas_call(
        _mm_norm_body,
        grid=(n // tm, nn // tn),
        in_specs=[pl.BlockSpec((tm, d), lambda i, j: (i, 0)),
                  pl.BlockSpec((1, d), lambda i, j: (0, 0)),
                  pl.BlockSpec((d, tn), lambda i, j: (0, j))],
        out_specs=[pl.BlockSpec((tm, tn), lambda i, j: (i, j)),
                   pl.BlockSpec((tm, d), lambda i, j: (i, 0))],
        out_shape=[SDS((n, nn), F32), SDS((n, d), BF16)],
        compiler_params=_cparams(("parallel", "arbitrary")),
        name="mm_norm",
    )(x, nw.reshape(1, d), w_bf)


def _mm_res_body(a_ref, w_ref, r_ref, o_ref):
    o_ref[...] = r_ref[...] + _dot(a_ref[...], w_ref[...])


def mm_res(a_bf, w_bf, res):
    n, k = a_bf.shape
    nn = w_bf.shape[1]
    tm = _pick(n, (768, 512, 256, 128, 64, 32, 16))
    tn = _pick(nn, (1024, 512, 256, 128))
    return pl.pallas_call(
        _mm_res_body,
        grid=(n // tm, nn // tn),
        in_specs=[pl.BlockSpec((tm, k), lambda i, j: (i, 0)),
                  pl.BlockSpec((k, tn), lambda i, j: (0, j)),
                  pl.BlockSpec((tm, tn), lambda i, j: (i, j))],
        out_specs=pl.BlockSpec((tm, tn), lambda i, j: (i, j)),
        out_shape=SDS((n, nn), F32),
        compiler_params=_cparams(("parallel", "arbitrary")),
        name="mm_res",
    )(a_bf, w_bf, res)


def _rms_body(x_ref, nw_ref, o_ref):
    x = x_ref[...]
    ms = jnp.mean(x * x, axis=-1, keepdims=True)
    o_ref[...] = x * lax.rsqrt(ms + EPS) * nw_ref[...]


def rms_final(x, nw):
    n, d = x.shape
    tm = _pick(n, (768, 512, 256, 128, 64, 32, 16, 8))
    return pl.pallas_call(
        _rms_body,
        grid=(n // tm,),
        in_specs=[pl.BlockSpec((tm, d), lambda i: (i, 0)), pl.BlockSpec((1, d), lambda i: (0, 0))],
        out_specs=pl.BlockSpec((tm, d), lambda i: (i, 0)),
        out_shape=SDS((n, d), F32),
        compiler_params=_cparams(("parallel",)),
        name="rms_final",
    )(x, nw.reshape(1, d))


def _batcher_pairs(n):
    pairs = []
    p = 1
    while p < n:
        k = p
        while k >= 1:
            for j in range(k % p, n - k, 2 * k):
                for i in range(min(k, n - j - k)):
                    if (i + j) // (2 * p) == (i + j + k) // (2 * p):
                        pairs.append((i + j, i + j + k))
            k //= 2
        p *= 2
    return pairs


_SORT16 = _batcher_pairs(16)


def _sort16_desc(xs):
    xs = list(xs)
    for i, j in _SORT16:
        hi = jnp.maximum(xs[i], xs[j])
        lo = jnp.minimum(xs[i], xs[j])
        xs[i], xs[j] = hi, lo
    return xs


def _bitonic16_desc(c):
    c = list(c)
    for stride in (8, 4, 2, 1):
        for i in range(16):
            if i & stride == 0:
                hi = jnp.maximum(c[i], c[i + stride])
                lo = jnp.minimum(c[i], c[i + stride])
                c[i], c[i + stride] = hi, lo
    return c


def _merge16_desc(a, b):
    return _bitonic16_desc([jnp.maximum(a[i], b[15 - i]) for i in range(16)])


def _top16_sorted(s):
    cols = _sort16_desc([s[v * SUBLANES:(v + 1) * SUBLANES, :] for v in range(16)])
    for shift in (4, 2, 1):
        other = [pltpu.roll(x, shift, 0) for x in cols]
        cols = _merge16_desc(cols, other)
    return cols


def _peer_score_body(q_ref, k1_ref, k2_ref, s1_ref, s2_ref, st_ref):
    tm = q_ref.shape[0]
    half = PEER_DK // 2
    k1 = k1_ref[...].astype(BF16)
    k2 = k2_ref[...].astype(BF16)
    sub = lax.broadcasted_iota(I32, (SUBLANES, tm), 0)
    a_top = None
    b_top = None
    for h in range(PEER_HEADS):
        qh = q_ref[:, h * PEER_DK:(h + 1) * PEER_DK]
        s1 = _dot_nt(k1, qh[:, :half].astype(BF16))
        s2 = _dot_nt(k2, qh[:, half:].astype(BF16))
        s1_ref[h] = s1
        s2_ref[h] = s2
        a_h = _top16_sorted(s1)
        b_h = _top16_sorted(s2)
        if h == 0:
            a_top, b_top = a_h, b_h
        else:
            a_top = [jnp.where(sub == h, x, y) for x, y in zip(a_h, a_top)]
            b_top = [jnp.where(sub == h, x, y) for x, y in zip(b_h, b_top)]
    ninf = jnp.full((SUBLANES, tm), -jnp.inf, F32)
    row0 = [a_top[0] + b_top[b] for b in range(16)]
    col0 = [a_top[a] + b_top[0] for a in range(1, 16)] + [ninf]
    mid = ([a_top[1] + b_top[b] for b in range(1, 8)] + [a_top[a] + b_top[1] for a in range(2, 8)]
           + [a_top[2] + b_top[b] for b in range(2, 5)])
    mid = _sort16_desc(mid)
    v0 = a_top[3] + b_top[2]
    v1 = a_top[4] + b_top[2]
    v2 = a_top[3] + b_top[3]
    tail = [v0, jnp.maximum(v1, v2), jnp.minimum(v1, v2)] + [ninf] * 13
    top = _merge16_desc(_merge16_desc(_merge16_desc(row0, col0), mid), tail)
    z = jnp.zeros((SUBLANES, tm), F32)
    for i in range(16):
        z = z + jnp.exp(top[i] - top[0])
    st_ref[0] = top[15]
    st_ref[1] = a_top[0]
    st_ref[2] = b_top[0]
    st_ref[3] = z


def peer_score(q, k1, k2):
    n, d = q.shape
    nk = k1.shape[0]
    assert nk == 128 and d == PEER_HEADS * PEER_DK
    tm = _pick(n, (256, 128))
    return pl.pallas_call(
        _peer_score_body,
        grid=(n // tm,),
        in_specs=[pl.BlockSpec((tm, d), lambda i: (i, 0)),
                  pl.BlockSpec(k1.shape, lambda i: (0, 0)),
                  pl.BlockSpec(k2.shape, lambda i: (0, 0))],
        out_specs=[pl.BlockSpec((PEER_HEADS, nk, tm), lambda i: (0, 0, i)),
                   pl.BlockSpec((PEER_HEADS, nk, tm), lambda i: (0, 0, i)),
                   pl.BlockSpec((4, PEER_HEADS, tm), lambda i: (0, 0, i))],
        out_shape=[SDS((PEER_HEADS, nk, n), F32), SDS((PEER_HEADS, nk, n), F32),
                   SDS((4, PEER_HEADS, n), F32)],
        compiler_params=_cparams(("parallel",)),
        name="peer_score",
    )(q, k1, k2)


def _peer_dense_body(xn_ref, u_ref, vt_ref, s1_ref, s2_ref, st_ref, res_ref, o_ref,
                     acc_ref, e2_ref, h_ref, act_ref, *, nc, n_steps):
    c = pl.program_id(1)
    nk = s2_ref.shape[1]

    @pl.when(c == 0)
    def _():
        acc_ref[...] = jnp.zeros_like(acc_ref)
        for h in range(PEER_HEADS):
            e2_ref[h] = jnp.exp(s2_ref[h] - st_ref[2, h:h + 1, :])

    h_ref[...] = _dot_nt(u_ref[...], xn_ref[...])
    for s in range(nc):
        e1 = c * nc + s
        w = jnp.zeros((nk, xn_ref.shape[0]), F32)
        for h in range(PEER_HEADS):
            s1row = s1_ref[h, pl.ds(e1, 1), :]
            p1 = jnp.exp(s1row - st_ref[1, h:h + 1, :]) / st_ref[3, h:h + 1, :]
            a = s1row + s2_ref[h]
            w = w + jnp.where(a >= st_ref[0, h:h + 1, :], p1 * e2_ref[h], 0.0)
        act_ref[s * nk:(s + 1) * nk, :] = (_gelu(h_ref[s * nk:(s + 1) * nk, :]) * w).astype(BF16)
    acc_ref[...] += _dot(vt_ref[...], act_ref[...])

    @pl.when(c == n_steps - 1)
    def _():
        o_ref[...] = res_ref[...] + acc_ref[...].T


def peer_dense(xn_bf, u_bf, vt_bf, layer, s1t, s2t, st, res):
    n, d = xn_bf.shape
    ne = u_bf.shape[1]
    nk = s1t.shape[1]
    tm = _pick(n, (768, 512, 256, 128))
    nc = 8
    te = nc * nk
    n_steps = ne // te
    body = functools.partial(_peer_dense_body, nc=nc, n_steps=n_steps)
    once = pl.Buffered(1)
    return pl.pallas_call(
        body,
        grid=(n // tm, n_steps),
        in_specs=[pl.BlockSpec((tm, d), lambda i, c: (i, 0), pipeline_mode=once),
                  pl.BlockSpec((None, te, d), lambda i, c: (layer, c, 0)),
                  pl.BlockSpec((None, d, te), lambda i, c: (layer, 0, c)),
                  pl.BlockSpec((PEER_HEADS, nk, tm), lambda i, c: (0, 0, i), pipeline_mode=once),
                  pl.BlockSpec((PEER_HEADS, nk, tm), lambda i, c: (0, 0, i), pipeline_mode=once),
                  pl.BlockSpec((4, PEER_HEADS, tm), lambda i, c: (0, 0, i), pipeline_mode=once),
                  pl.BlockSpec((tm, d), lambda i, c: (i, 0), pipeline_mode=once)],
        out_specs=pl.BlockSpec((tm, d), lambda i, c: (i, 0), pipeline_mode=once),
        out_shape=SDS((n, d), F32),
        scratch_shapes=[pltpu.VMEM((d, tm), F32), pltpu.VMEM((PEER_HEADS, nk, tm), F32),
                        pltpu.VMEM((te, tm), F32), pltpu.VMEM((te, tm), BF16)],
        compiler_params=_cparams(("parallel", "arbitrary")),
        name="peer_dense",
    )(xn_bf, u_bf, vt_bf, s1t, s2t, st, res)


def _flash_init(m_ref, l_ref, acc_ref):
    m_ref[...] = jnp.full(m_ref.shape, NEG, F32)
    l_ref[...] = jnp.zeros(l_ref.shape, F32)
    acc_ref[...] = jnp.zeros(acc_ref.shape, F32)


def _flash_update(s, mask, v_tiles, m_ref, l_ref, acc_ref):
    if mask is not None:
        s = jnp.where(mask, s, NEG)
    m_prev = m_ref[...]
    m_new = jnp.maximum(m_prev, jnp.max(s, axis=-1, keepdims=True))
    alpha = jnp.exp(m_prev - m_new)
    p = jnp.exp(s - m_new)
    if mask is not None:
        p = jnp.where(mask, p, 0.0)
    l_ref[...] = alpha * l_ref[...] + jnp.sum(p, axis=-1, keepdims=True)
    pb = p.astype(BF16)
    pv = None
    off = 0
    for v in v_tiles:
        part = _dot(pb[:, off:off + v.shape[0]], v)
        pv = part if pv is None else pv + part
        off += v.shape[0]
    acc_ref[...] = alpha * acc_ref[...] + pv
    m_ref[...] = m_new


def _flash_final(l_ref, acc_ref):
    return acc_ref[...] / jnp.maximum(l_ref[...], 1e-30)


def _softmax_masked(s, mask):
    s = jnp.where(mask, s, NEG)
    m = jnp.max(s, axis=-1, keepdims=True)
    e = jnp.where(mask, jnp.exp(s - m), 0.0)
    return e / jnp.maximum(jnp.sum(e, axis=-1, keepdims=True), 1e-30)


def _topk_mask(score, k):
    n = score.shape[-1]
    lane = lax.broadcasted_iota(I32, score.shape, 1).astype(F32)
    sel = jnp.zeros(score.shape, F32)
    for _ in range(k):
        m = jnp.max(score, axis=-1, keepdims=True)
        idx = jnp.min(jnp.where(score == m, lane, float(n)), axis=-1, keepdims=True)
        hit = lane == idx
        sel = jnp.where(hit, jnp.where(m >= 0.0, 1.0, 0.0), sel)
        score = jnp.where(hit, -jnp.inf, score)
    return sel


def _compress_body(x_ref, pe_ref, w1_ref, w2_ref, o_ref):
    xb = (x_ref[...] + pe_ref[...]).astype(BF16)
    h = _gelu(_dot(xb, w1_ref[...]))
    o_ref[...] = _dot(h.astype(BF16), w2_ref[...])


def compress(x, pe, w1_bf, w2_bf):
    rows, k = x.shape
    tr = _pick(rows, (256, 128, 64, 32, 16, 8))
    hid = w1_bf.shape[1]
    hd = w2_bf.shape[1]
    return pl.pallas_call(
        _compress_body,
        grid=(rows // tr,),
        in_specs=[pl.BlockSpec((tr, k), lambda i: (i, 0)),
                  pl.BlockSpec((1, k), lambda i: (0, 0)),
                  pl.BlockSpec((k, hid), lambda i: (0, 0)),
                  pl.BlockSpec((hid, hd), lambda i: (0, 0))],
        out_specs=pl.BlockSpec((tr, hd), lambda i: (i, 0)),
        out_shape=SDS((rows, hd), F32),
        compiler_params=_cparams(("parallel",)),
        name="nsa_compress",
    )(x, pe.reshape(1, k), w1_bf, w2_bf)


def _nsa_cmp_and_select(qs, slope, t1, kc_ref, vc_ref, *, r, nb, n_sel, ns_pad, k_top):
    tq = t1.shape[0]
    t = _rep_rows(t1, r)
    blk_end = lax.broadcasted_iota(I32, (1, nb), 1) * CMP_BLOCK + (CMP_BLOCK - 1)
    d_c = t - blk_end
    s = _dot_nt(qs, kc_ref[...].astype(BF16)) - slope * d_c.astype(F32)
    p_c = _softmax_masked(s, d_c >= 0)
    o_c = _dot(p_c.astype(BF16), vc_ref[...].astype(BF16))
    imp = p_c[0:tq]
    for i in range(1, r):
        imp = imp + p_c[i * tq:(i + 1) * tq]
    if ns_pad > nb:
        imp = jnp.concatenate([imp, jnp.zeros((tq, ns_pad - nb), F32)], axis=1)
    jsel = lax.broadcasted_iota(I32, (1, ns_pad), 1)
    cur = jnp.right_shift(t1, 6)
    forced = (jsel == 0) | (jsel == cur) | (jsel == cur - 1)
    score = jnp.where(forced, r + 1.0, jnp.where(jsel <= cur, imp, -1.0))
    score = jnp.where(jsel < n_sel, score, -2.0)
    return o_c, _topk_mask(score, k_top)


def _gate_cols(gsig, c, r):
    return jnp.concatenate([gsig[:, c * r + i:c * r + i + 1] for i in range(r)], axis=0)


def _flash_update_t(s, mask, v, m_ref, l_ref, acc_ref):
    if mask is not None:
        s = jnp.where(mask, s, NEG)
    m_prev = m_ref[...]
    m_new = jnp.maximum(m_prev, jnp.max(s, axis=0, keepdims=True))
    alpha = jnp.exp(m_prev - m_new)
    p = jnp.exp(s - m_new)
    if mask is not None:
        p = jnp.where(mask, p, 0.0)
    l_ref[...] = alpha * l_ref[...] + jnp.sum(p, axis=0, keepdims=True)
    acc_ref[...] = alpha * acc_ref[...] + _dot_tn(v, p.astype(BF16))
    m_ref[...] = m_new


def _nsa_p_body(q_ref, gt_ref, kc_ref, vc_ref, ks_ref, vs_ref, kw_ref, vw_ref, sl_ref, kp_ref, o_ref,
                m_ref, l_ref, acc_ref, *, tq, tk, wk, nb, n_sel, k_top, scale, r):
    qi = pl.program_id(2)
    a = qi * tq
    rows = r * tq
    qs = (_stack_heads(q_ref[...], r, HEAD_DIM) * scale).astype(BF16)
    slope = sl_ref[...]
    t1 = a + lax.broadcasted_iota(I32, (tq, 1), 0)
    t = _rep_rows(t1, r)
    o_c, sel = _nsa_cmp_and_select(qs, slope, t1, kc_ref, vc_ref, r=r, nb=nb, n_sel=n_sel,
                                   ns_pad=n_sel, k_top=k_top)
    sel_bf = sel.astype(BF16)
    lane = lax.broadcasted_iota(I32, (rows, LANES), 1)
    t_hi = jnp.right_shift(t, 6).astype(F32)
    t_lo = jnp.bitwise_and(t, SEL_BLOCK - 1).astype(F32)
    qb = jnp.where(lane == 0, slope * 64.0,
                   jnp.where(lane == 1, slope,
                             jnp.where(lane == 2, -(slope * 64.0) * t_hi,
                                       jnp.where(lane == 3, -slope * t_lo, 0.0)))).astype(BF16)
    qa = jnp.concatenate([qs, qb], axis=1)
    t_row = a + jnp.concatenate([lax.broadcasted_iota(I32, (1, tq), 1)] * r, axis=1)

    def attend(k_ref_, v_ref_, start, width, window):
        ka = jnp.concatenate([k_ref_[pl.ds(start, width), :].astype(BF16), kp_ref[pl.ds(start, width), :]], axis=1)
        v = v_ref_[pl.ds(start, width), :].astype(BF16)
        s = _dot_nt(ka, qa)
        pos = start + lax.broadcasted_iota(I32, (width, 1), 0)
        mask = pos <= t_row
        if window:
            mask = mask & (pos >= t_row - WINDOW)
        else:
            blk = lax.broadcasted_iota(I32, (width, n_sel), 1)
            key_blk = jnp.right_shift(start + lax.broadcasted_iota(I32, (width, n_sel), 0), 6)
            bm = _dot_nt(jnp.where(blk == key_blk, 1.0, 0.0).astype(BF16), sel_bf)
            mask = mask & (jnp.concatenate([bm] * r, axis=1) > 0.5)
        _flash_update_t(s, mask, v, m_ref, l_ref, acc_ref)

    def sel_step(j, carry):
        attend(ks_ref, vs_ref, pl.multiple_of(j * tk, tk), tk, False)
        return carry

    _flash_init(m_ref, l_ref, acc_ref)
    lax.fori_loop(0, (a + tq - 1) // tk + 1, sel_step, 0)
    o_s = (acc_ref[...] / jnp.maximum(l_ref[...], 1e-30)).T

    _flash_init(m_ref, l_ref, acc_ref)
    attend(kw_ref, vw_ref, pl.multiple_of(jnp.maximum(a + tq - wk, 0), SUBLANES), wk, True)
    o_w = (acc_ref[...] / jnp.maximum(l_ref[...], 1e-30)).T

    gsig = 1.0 / (1.0 + jnp.exp(-gt_ref[...]))
    o = _gate_cols(gsig, 0, r) * o_c + _gate_cols(gsig, 1, r) * o_s + _gate_cols(gsig, 2, r) * o_w
    o_ref[...] = _unstack_heads(o, r, tq).astype(BF16)


def nsa_prompt(hab, kc, vc, slopes, cols, *, b, t, g, r):
    hd = HEAD_DIM
    nb = kc.shape[2]
    n_sel = -(-t // SEL_BLOCK)
    assert n_sel == nb and t % SEL_BLOCK == 0
    tq = _pick(t, (2 * Q_BLOCK, Q_BLOCK, t))
    tk = _pick(t, (512, 256, 128, t))
    wk = min(WINDOW + tq, t)
    nq = t // tq
    qw = r * hd
    body = functools.partial(_nsa_p_body, tq=tq, tk=tk, wk=wk, nb=nb, n_sel=n_sel, k_top=min(SEL_TOPK, n_sel),
                             scale=hd ** -0.5, r=r)
    assert t <= 256 * SEL_BLOCK
    pos = np.arange(t)
    kp = np.zeros((t, LANES), np.float32)
    kp[:, 0], kp[:, 1], kp[:, 2], kp[:, 3] = pos >> 6, pos & (SEL_BLOCK - 1), 1.0, 1.0
    kpos = jnp.asarray(kp, BF16)

    def seq_spec(col0):
        return pl.BlockSpec((t, hd), lambda bb, gg, qi: (bb, col0 // hd + gg))

    return pl.pallas_call(
        body,
        grid=(b, g, nq),
        in_specs=[pl.BlockSpec((tq, qw), lambda bb, gg, qi: (bb * nq + qi, gg)),
                  pl.BlockSpec((tq, LANES), lambda bb, gg, qi: (bb * nq + qi, cols["gt"] // LANES + gg)),
                  pl.BlockSpec((None, None, nb, hd), lambda bb, gg, qi: (bb, gg, 0, 0)),
                  pl.BlockSpec((None, None, nb, hd), lambda bb, gg, qi: (bb, gg, 0, 0)),
                  seq_spec(cols["ks"]), seq_spec(cols["vs"]), seq_spec(cols["kw"]), seq_spec(cols["vw"]),
                  pl.BlockSpec((None, r * tq, 1), lambda bb, gg, qi: (gg, 0, 0)),
                  pl.BlockSpec((t, LANES), lambda bb, gg, qi: (0, 0))],
        out_specs=pl.BlockSpec((tq, qw), lambda bb, gg, qi: (bb * nq + qi, gg)),
        out_shape=SDS((b * t, g * qw), BF16),
        scratch_shapes=[pltpu.VMEM((1, r * tq), F32), pltpu.VMEM((1, r * tq), F32),
                        pltpu.VMEM((hd, r * tq), F32)],
        compiler_params=_cparams(("parallel", "parallel", "arbitrary")),
        name="nsa_prompt",
    )(hab, hab, kc, vc, hab, hab, hab, hab, jnp.repeat(slopes, tq, axis=1).reshape(g, r * tq, 1), kpos)


def _pad_rows(x, rows):
    return jnp.concatenate([x, jnp.zeros((rows - x.shape[0], x.shape[1]), x.dtype)], axis=0)


def _block_expand(ns_pad, first_pos, width):
    blk = lax.broadcasted_iota(I32, (ns_pad, width), 0)
    key_blk = jnp.right_shift(first_pos + lax.broadcasted_iota(I32, (ns_pad, width), 1), 6)
    return jnp.where(blk == key_blk, 1.0, 0.0).astype(BF16)


def _nsa_s_body(pt_ref, q_ref, gt_ref, ksn_ref, vsn_ref, kwn_ref, vwn_ref, kc_ref, vc_ref, win_ref,
                sl_ref, *rest, pp, n_chunks, ts, page, offset, wb, nb, n_sel, ns_pad, k_top, scale, g, r):
    pages = rest[:pp]
    o_ref = rest[pp]
    qs_ref, oc_ref, sel_ref, m_ref, l_ref, acc_ref = rest[pp + 1:]
    del pt_ref
    c = pl.program_id(1)
    hd = HEAD_DIM
    t1 = offset + lax.broadcasted_iota(I32, (ts, 1), 0)
    t = _rep_rows(t1, r)
    lane = lax.broadcasted_iota(I32, (1, page), 1)

    @pl.when(c == 0)
    def _():
        for gg in range(g):
            qs = (_stack_heads(q_ref[:, gg * r * hd:(gg + 1) * r * hd], r, hd) * scale).astype(BF16)
            qs_ref[gg] = qs
            o_c, sel = _nsa_cmp_and_select(qs, sl_ref[gg], t1, kc_ref.at[gg], vc_ref.at[gg], r=r, nb=nb,
                                           n_sel=n_sel, ns_pad=ns_pad, k_top=k_top)
            oc_ref[gg] = o_c
            sel_ref[gg] = _rep_rows(sel, r).astype(BF16)
            _flash_init(m_ref.at[gg], l_ref.at[gg], acc_ref.at[gg])

    width = pp * page
    first = c * width
    d = t - (first + lax.broadcasted_iota(I32, (1, width), 1))
    expand = _block_expand(ns_pad, first, width)
    for gg in range(g):
        ks = [pages[i][0, pl.ds(2 * g + gg, page, stride=4 * g), :].astype(BF16) for i in range(pp)]
        vs = [pages[i][0, pl.ds(3 * g + gg, page, stride=4 * g), :].astype(BF16) for i in range(pp)]
        s = jnp.concatenate([_dot_nt(qs_ref[gg], k) for k in ks], axis=1) - sl_ref[gg] * d.astype(F32)
        bm = _dot(sel_ref[gg], expand)
        _flash_update(s, (bm > 0.5) & (d >= 0), vs, m_ref.at[gg], l_ref.at[gg], acc_ref.at[gg])

    @pl.when(c == n_chunks - 1)
    def _():
        gsig = 1.0 / (1.0 + jnp.exp(-gt_ref[...]))
        dn = t - (offset + lane)
        expand_n = _block_expand(ns_pad, offset, page)
        outs = []
        for gg in range(g):
            k = _pad_rows(ksn_ref[:, gg * hd:(gg + 1) * hd], page).astype(BF16)
            v = _pad_rows(vsn_ref[:, gg * hd:(gg + 1) * hd], page).astype(BF16)
            s = _dot_nt(qs_ref[gg], k) - sl_ref[gg] * dn.astype(F32)
            bm = _dot(sel_ref[gg], expand_n)
            _flash_update(s, (bm > 0.5) & (dn >= 0) & (lane < ts), [v], m_ref.at[gg], l_ref.at[gg],
                          acc_ref.at[gg])
            o_s = _flash_final(l_ref.at[gg], acc_ref.at[gg])
            _flash_init(m_ref.at[gg], l_ref.at[gg], acc_ref.at[gg])
            kw = win_ref[pl.ds(gg, wb, stride=2 * g), :].astype(BF16)
            vw = win_ref[pl.ds(g + gg, wb, stride=2 * g), :].astype(BF16)
            kn = _pad_rows(kwn_ref[:, gg * hd:(gg + 1) * hd], page).astype(BF16)
            vn = _pad_rows(vwn_ref[:, gg * hd:(gg + 1) * hd], page).astype(BF16)
            dw = t - (offset - wb + lax.broadcasted_iota(I32, (1, wb), 1))
            s = jnp.concatenate([_dot_nt(qs_ref[gg], kw) - sl_ref[gg] * dw.astype(F32),
                                 _dot_nt(qs_ref[gg], kn) - sl_ref[gg] * dn.astype(F32)], axis=1)
            mask = jnp.concatenate([(dw >= 0) & (dw <= WINDOW), (dn >= 0) & (dn <= WINDOW) & (lane < ts)], axis=1)
            _flash_update(s, mask, [vw, vn], m_ref.at[gg], l_ref.at[gg], acc_ref.at[gg])
            o_w = _flash_final(l_ref.at[gg], acc_ref.at[gg])
            gs = gsig[:, gg * LANES:(gg + 1) * LANES]
            o = _gate_cols(gs, 0, r) * oc_ref[gg] + _gate_cols(gs, 1, r) * o_s + _gate_cols(gs, 2, r) * o_w
            outs.append(_unstack_heads(o, r, ts))
        o_ref[...] = jnp.concatenate(outs, axis=1)


def nsa_sample(hab, kc, vc, pool, win, page_table, slopes, cols, *, n_p, bs, ts, page, g, r):
    hd = HEAD_DIM
    n_pages = page_table.shape[1]
    offset = n_pages * page
    wb = win.shape[1] // (2 * g)
    assert offset % SEL_BLOCK == 0 and offset - wb >= 0 and wb == WINDOW
    nb = kc.shape[2]
    n_sel = -(-(offset + ts) // SEL_BLOCK)
    ns_pad = -(-n_sel // LANES) * LANES
    pp = _pick(n_pages, (32, 16, 8, 4, 2, 1))
    n_chunks = n_pages // pp
    rb = n_p // ts
    body = functools.partial(_nsa_s_body, pp=pp, n_chunks=n_chunks, ts=ts, page=page, offset=offset, wb=wb,
                             nb=nb, n_sel=n_sel, ns_pad=ns_pad, k_top=min(SEL_TOPK, n_sel),
                             scale=hd ** -0.5, g=g, r=r)

    def row_spec(width, col0):
        return pl.BlockSpec((ts, width), lambda b, c, pt: (rb + b, col0 // width))

    def page_spec(i):
        return pl.BlockSpec((1, page * 4 * g, hd), lambda b, c, pt: (pt[b, c * pp + i], 0, 0))

    in_specs = [row_spec(g * r * hd, cols["q"]), row_spec(g * LANES, cols["gt"]),
                row_spec(g * hd, cols["ks"]), row_spec(g * hd, cols["vs"]),
                row_spec(g * hd, cols["kw"]), row_spec(g * hd, cols["vw"]),
                pl.BlockSpec((None, g, nb, hd), lambda b, c, pt: (b, 0, 0, 0)),
                pl.BlockSpec((None, g, nb, hd), lambda b, c, pt: (b, 0, 0, 0)),
                pl.BlockSpec((None, wb * 2 * g, hd), lambda b, c, pt: (b, 0, 0)),
                pl.BlockSpec((g, r * ts, 1), lambda b, c, pt: (0, 0, 0))]
    in_specs += [page_spec(i) for i in range(pp)]
    grid_spec = pltpu.PrefetchScalarGridSpec(
        num_scalar_prefetch=1, grid=(bs, n_chunks), in_specs=in_specs,
        out_specs=pl.BlockSpec((ts, g * r * hd), lambda b, c, pt: (b, 0)),
        scratch_shapes=[pltpu.VMEM((g, r * ts, hd), BF16), pltpu.VMEM((g, r * ts, hd), F32),
                        pltpu.VMEM((g, r * ts, ns_pad), BF16), pltpu.VMEM((g, r * ts, 1), F32),
                        pltpu.VMEM((g, r * ts, 1), F32), pltpu.VMEM((g, r * ts, hd), F32)])
    return pl.pallas_call(
        body, grid_spec=grid_spec, out_shape=SDS((bs * ts, g * r * hd), F32),
        compiler_params=_cparams(("parallel", "arbitrary")), name="nsa_sample",
    )(page_table, hab, hab, hab, hab, hab, hab, kc, vc, win,
      jnp.repeat(slopes, ts, axis=1).reshape(g, r * ts, 1), *([pool] * pp))


def _cmp_s_body(pt_ref, pek_ref, w1k_ref, w2k_ref, pev_ref, w1v_ref, w2v_ref, *rest, pp, page, g):
    pages = rest[:pp]
    kc_ref, vc_ref, stage_ref, x_ref = rest[pp:]
    del pt_ref
    hd = HEAD_DIM
    rows = pp * (page // CMP_BLOCK)
    for cg in range(2 * g):
        for i in range(pp):
            stage_ref[cg, i * page:(i + 1) * page, :] = pages[i][0, pl.ds(cg, page, stride=4 * g), :]
    for comp, (pe_ref, w1_ref, w2_ref, out_ref) in enumerate(
            ((pek_ref, w1k_ref, w2k_ref, kc_ref), (pev_ref, w1v_ref, w2v_ref, vc_ref))):
        for l in range(CMP_BLOCK):
            x = jnp.concatenate([stage_ref[comp * g + gg, pl.ds(l, rows, stride=CMP_BLOCK), :]
                                 for gg in range(g)], axis=0)
            x_ref[:, l * hd:(l + 1) * hd] = (x + pe_ref[l:l + 1, :]).astype(BF16)
        out = _dot(_gelu(_dot(x_ref[...], w1_ref[...])).astype(BF16), w2_ref[...])
        for gg in range(g):
            out_ref[gg] = out[gg * rows:(gg + 1) * rows]


def compress_sample(pool, page_table, p, *, page, g):
    hd = HEAD_DIM
    bs, n_pages = page_table.shape
    pp = _pick(n_pages, (16, 8, 4))
    rows = pp * (page // CMP_BLOCK)
    nb = n_pages * (page // CMP_BLOCK)
    hid = p["w1_k"].shape[1]
    body = functools.partial(_cmp_s_body, pp=pp, page=page, g=g)
    once = pl.Buffered(1)

    def wspecs():
        return [pl.BlockSpec((CMP_BLOCK, hd), lambda b, c, pt: (0, 0)),
                pl.BlockSpec((CMP_BLOCK * hd, hid), lambda b, c, pt: (0, 0), pipeline_mode=once),
                pl.BlockSpec((hid, hd), lambda b, c, pt: (0, 0))]

    def page_spec(i):
        return pl.BlockSpec((1, page * 4 * g, hd), lambda b, c, pt: (pt[b, c * pp + i], 0, 0))

    grid_spec = pltpu.PrefetchScalarGridSpec(
        num_scalar_prefetch=1, grid=(bs, n_pages // pp),
        in_specs=wspecs() + wspecs() + [page_spec(i) for i in range(pp)],
        out_specs=[pl.BlockSpec((None, g, rows, hd), lambda b, c, pt: (b, 0, c, 0)),
                   pl.BlockSpec((None, g, rows, hd), lambda b, c, pt: (b, 0, c, 0))],
        scratch_shapes=[pltpu.VMEM((2 * g, pp * page, hd), F32), pltpu.VMEM((g * rows, CMP_BLOCK * hd), BF16)])
    return pl.pallas_call(
        body, grid_spec=grid_spec, out_shape=[SDS((bs, g, nb, hd), F32), SDS((bs, g, nb, hd), F32)],
        compiler_params=_cparams(("parallel", "arbitrary")), name="nsa_compress_sample",
    )(page_table, p["pe_k"], p["w1_k"].astype(BF16), p["w2_k"].astype(BF16),
      p["pe_v"], p["w1_v"].astype(BF16), p["w2_v"].astype(BF16), *([pool] * pp))


def _ret_body(q_ref, k_ref, v_ref, g_ref, gn_ref, s0_ref, dm_ref, cr_ref, kd_ref, cd_ref, o_ref, s_ref,
              st_ref, *, scale, n_chunks, nh):
    c = pl.program_id(1)
    dk = HEAD_DIM

    @pl.when(c == 0)
    def _():
        st_ref[...] = s0_ref[...]

    for h in range(nh):
        cs = slice(h * dk, (h + 1) * dk)
        qb = q_ref[:, cs].astype(BF16)
        ks = k_ref[:, cs] * scale
        kb = ks.astype(BF16)
        vb = v_ref[:, cs].astype(BF16)
        st = st_ref[h]
        att = _dot_nt(qb, kb) * dm_ref[h]
        o = _dot(att.astype(BF16), vb) + _dot(qb, st.astype(BF16)) * cr_ref[h]
        st_ref[h] = st * cd_ref[h] + _dot_tn((ks * kd_ref[h]).astype(BF16), vb)
        mu = jnp.mean(o, axis=-1, keepdims=True)
        var = jnp.mean(jnp.square(o - mu), axis=-1, keepdims=True)
        on = (o - mu) * lax.rsqrt(var + GN_EPS) * gn_ref[:, cs]
        gate = g_ref[:, cs]
        o_ref[:, cs] = ((gate * (1.0 / (1.0 + jnp.exp(-gate)))) * on).astype(o_ref.dtype)

    @pl.when(c == n_chunks - 1)
    def _():
        s_ref[...] = st_ref[...]


def retention(hab, gn_w, s0, cols, *, row0, b, t, nh):
    dk = HEAD_DIM
    w = nh * dk
    ch = RET_CHUNK if t % RET_CHUNK == 0 else t
    n_chunks = t // ch
    lg = jnp.log1p(-jnp.exp2(-5.0 - jnp.arange(nh, dtype=F32)))
    i = jnp.arange(ch, dtype=F32)
    diff = i[:, None] - i[None, :]
    dmask = jnp.where(diff >= 0, jnp.exp(jnp.maximum(diff, 0.0)[None] * lg[:, None, None]), 0.0)
    cross = jnp.exp((i + 1.0)[None, :] * lg[:, None]).reshape(nh, ch, 1)
    kdec = jnp.exp((ch - 1.0 - i)[None, :] * lg[:, None]).reshape(nh, ch, 1)
    cdec = jnp.exp(ch * lg).reshape(nh, 1, 1)
    rb = row0 // ch
    body = functools.partial(_ret_body, scale=dk ** -0.5, n_chunks=n_chunks, nh=nh)

    def col_spec(col0):
        assert col0 % w == 0
        return pl.BlockSpec((ch, w), lambda bb, c: (rb + bb * n_chunks + c, col0 // w))

    def whole(shape):
        return pl.BlockSpec(shape, lambda bb, c: (0,) * len(shape))

    return pl.pallas_call(
        body,
        grid=(b, n_chunks),
        in_specs=[col_spec(cols["rq"]), col_spec(cols["rk"]), col_spec(cols["rv"]), col_spec(cols["rg"]),
                  whole((1, w)),
                  pl.BlockSpec((None, nh, dk, dk), lambda bb, c: (bb, 0, 0, 0)),
                  whole((nh, ch, ch)), whole((nh, ch, 1)), whole((nh, ch, 1)), whole((nh, 1, 1))],
        out_specs=[pl.BlockSpec((ch, w), lambda bb, c: (bb * n_chunks + c, 0)),
                   pl.BlockSpec((None, nh, dk, dk), lambda bb, c: (bb, 0, 0, 0))],
        out_shape=[SDS((b * t, w), BF16 if ch % 16 == 0 else F32), SDS((b, nh, dk, dk), F32)],
        scratch_shapes=[pltpu.VMEM((nh, dk, dk), F32)],
        compiler_params=_cparams(("parallel", "arbitrary")),
        name="retention",
    )(hab, hab, hab, hab, gn_w.reshape(1, w), s0, dmask, cross, kdec, cdec)


def _logf_body(x_ref, b_ref, o_ref):
    x = x_ref[...] + b_ref[...]
    o_ref[...] = -(jnp.maximum(-x, 0.0) + jnp.log1p(jnp.exp(-jnp.abs(x))))


def fox_logf(hc, b_pad, col0):
    n = hc.shape[0]
    tm = _pick(n, (768, 512, 256, 128, 64, 32, 16, 8))
    return pl.pallas_call(
        _logf_body,
        grid=(n // tm,),
        in_specs=[pl.BlockSpec((tm, LANES), lambda i: (i, col0 // LANES)),
                  pl.BlockSpec((1, LANES), lambda i: (0, 0))],
        out_specs=pl.BlockSpec((tm, LANES), lambda i: (i, 0)),
        out_shape=SDS((n, LANES), F32),
        compiler_params=_cparams(("parallel",)),
        name="fox_logf",
    )(hc, b_pad)


def _cumsum_rows_body(x_ref, o_ref, hi_ref, mid_ref, lo_ref, carry_ref):
    @pl.when(pl.program_id(1) == 0)
    def _():
        carry_ref[...] = jnp.zeros_like(carry_ref)

    tc = x_ref.shape[0]
    tri = jnp.where(lax.broadcasted_iota(I32, (tc, tc), 1) <= lax.broadcasted_iota(I32, (tc, tc), 0),
                    1.0, 0.0).astype(BF16)
    hi, mid, lo = _split3(x_ref[...])
    f = (_dot(tri, hi) + _dot(tri, mid)) + _dot(tri, lo) + carry_ref[...]
    o_ref[...] = f
    hi_ref[...], mid_ref[...], lo_ref[...] = _split3(f)
    carry_ref[...] = f[tc - 1:tc, :]


def cumsum_rows(x, *, b, t):
    tc = 128 if t % 128 == 0 else t
    nc = t // tc
    spec = pl.BlockSpec((tc, LANES), lambda bb, c: (bb * nc + c, 0))
    return pl.pallas_call(
        _cumsum_rows_body,
        grid=(b, nc),
        in_specs=[spec],
        out_specs=[spec, spec, spec, spec],
        out_shape=[SDS((b * t, LANES), F32)] + [SDS((b * t, LANES), BF16)] * 3,
        scratch_shapes=[pltpu.VMEM((1, LANES), F32)],
        compiler_params=_cparams(("parallel", "arbitrary")),
        name="fox_cumsum",
    )(x)


def _fox_p_body(q_ref, k_ref, v_ref, fq_ref, fa_ref, bs_ref, o_ref, m_ref, l_ref, acc_ref, *, tq, tk, scale, r):
    qi = pl.program_id(2)
    hd = HEAD_DIM
    rows = r * tq
    qs = (_stack_heads(q_ref[...], r, hd) * scale).astype(BF16)
    fq = jnp.concatenate([fq_ref[:, i:i + 1] for i in range(r)], axis=0)
    hi, mid, lo = _split3(fq)
    lane = lax.broadcasted_iota(I32, (rows, LANES), 1)
    qb = (bs_ref[...] + jnp.where(lane == 3 * r, hi.astype(F32), 0.0) + jnp.where(lane == 3 * r + 1, mid.astype(F32), 0.0)
          + jnp.where(lane == 3 * r + 2, lo.astype(F32), 0.0)).astype(BF16)
    qa = jnp.concatenate([qs, qb], axis=1)
    t_row = qi * tq + jnp.concatenate([lax.broadcasted_iota(I32, (1, tq), 1)] * r, axis=1)

    def tile(j, masked):
        start = pl.multiple_of(j * tk, tk)
        ka = jnp.concatenate([k_ref[pl.ds(start, tk), :].astype(BF16), fa_ref[pl.ds(start, tk), :]], axis=1)
        v = v_ref[pl.ds(start, tk), :].astype(BF16)
        s = _dot_nt(ka, qa)
        if masked:
            mask = start + lax.broadcasted_iota(I32, (tk, 1), 0) <= t_row
            s = jnp.where(mask, s, NEG)
        m_prev = m_ref[...]
        m_new = jnp.maximum(m_prev, jnp.max(s, axis=0, keepdims=True))
        alpha = jnp.exp(m_prev - m_new)
        p = jnp.exp(s - m_new)
        if masked:
            p = jnp.where(mask, p, 0.0)
        l_ref[...] = alpha * l_ref[...] + jnp.sum(p, axis=0, keepdims=True)
        acc_ref[...] = alpha * acc_ref[...] + _dot_tn(v, p.astype(BF16))
        m_ref[...] = m_new

    def full_tile(j, carry):
        tile(j, False)
        return carry

    _flash_init(m_ref, l_ref, acc_ref)
    n_full = (qi * tq) // tk
    lax.fori_loop(0, n_full, full_tile, 0)
    for jj in range(tq // tk):
        tile(n_full + jj, True)
    o = acc_ref[...] / jnp.maximum(l_ref[...], 1e-30)
    o_ref[...] = jnp.concatenate([o[:, i * tq:(i + 1) * tq].T for i in range(r)], axis=1).astype(BF16)


def fox_prompt(hc, f, f_split, cols, *, b, t, g, r):
    hd = HEAD_DIM
    tq = _pick(t, (512, 256, 128, t))
    tk = _pick(tq, (512, 256, 128, tq))
    nq = t // tq
    qw = r * hd
    fq = f.transpose(0, 2, 1, 3)
    fa = jnp.concatenate([piece.transpose(0, 2, 1, 3) for piece in f_split]
                         + [jnp.ones((b, g, t, 3), BF16), jnp.zeros((b, g, t, LANES - 3 * r - 3), BF16)],
                         axis=-1)
    rows = np.arange(r * tq)[:, None] // tq
    lanes = np.arange(LANES)[None, :]
    bsel = jnp.asarray(np.where((lanes < 3 * r) & (lanes % r == rows), -1.0, 0.0), F32)
    body = functools.partial(_fox_p_body, tq=tq, tk=tk, scale=hd ** -0.5, r=r)
    return pl.pallas_call(
        body,
        grid=(b, g, nq),
        in_specs=[pl.BlockSpec((tq, qw), lambda bb, gg, qi: (bb * nq + qi, gg)),
                  pl.BlockSpec((t, hd), lambda bb, gg, qi: (bb, cols["k"] // hd + gg)),
                  pl.BlockSpec((t, hd), lambda bb, gg, qi: (bb, cols["v"] // hd + gg)),
                  pl.BlockSpec((None, None, tq, r), lambda bb, gg, qi: (bb, gg, qi, 0)),
                  pl.BlockSpec((None, None, t, LANES), lambda bb, gg, qi: (bb, gg, 0, 0)),
                  pl.BlockSpec((r * tq, LANES), lambda bb, gg, qi: (0, 0))],
        out_specs=pl.BlockSpec((tq, qw), lambda bb, gg, qi: (bb * nq + qi, gg)),
        out_shape=SDS((b * t, g * qw), BF16),
        scratch_shapes=[pltpu.VMEM((1, r * tq), F32), pltpu.VMEM((1, r * tq), F32),
                        pltpu.VMEM((hd, r * tq), F32)],
        compiler_params=_cparams(("parallel", "parallel", "arbitrary")),
        name="fox_prompt",
    )(hc, hc, hc, fq, fa, bsel)


def _fox_f_body(pt_ref, new_ref, *rest, pp, n_chunks):
    pages = rest[:pp]
    fk_ref, fn_ref, carry_ref = rest[pp:]
    del pt_ref
    c = pl.program_id(1)
    page = pages[0].shape[2]

    @pl.when(c == 0)
    def _():
        carry_ref[...] = jnp.zeros_like(carry_ref)

    ut = jnp.where(lax.broadcasted_iota(I32, (page, page), 0) <= lax.broadcasted_iota(I32, (page, page), 1),
                   1.0, 0.0).astype(BF16)

    def csum(x, carry):
        hi, mid, lo = _split3(x)
        return (_dot(hi, ut) + _dot(mid, ut)) + _dot(lo, ut) + carry

    carry = carry_ref[...]
    for i in range(pp):
        f = csum(pages[i][0], carry)
        fk_ref[:, i * page:(i + 1) * page] = f
        carry = f[:, page - 1:page]
    carry_ref[...] = carry

    @pl.when(c == n_chunks - 1)
    def _():
        fn_ref[...] = csum(new_ref[...], carry)


def fox_f_sample(logf_pool_t, new_t, page_table):
    bs, n_pages = page_table.shape
    _, nh, page = logf_pool_t.shape
    pp = _pick(n_pages, (64, 32, 16, 8, 4, 2, 1))
    n_chunks = n_pages // pp
    body = functools.partial(_fox_f_body, pp=pp, n_chunks=n_chunks)
    in_specs = [pl.BlockSpec((None, nh, page), lambda b, c, pt: (b, 0, 0))]
    in_specs += [pl.BlockSpec((1, nh, page), functools.partial(lambda b, c, pt, i: (pt[b, c * pp + i], 0, 0), i=i))
                 for i in range(pp)]
    grid_spec = pltpu.PrefetchScalarGridSpec(
        num_scalar_prefetch=1, grid=(bs, n_chunks), in_specs=in_specs,
        out_specs=[pl.BlockSpec((None, nh, pp * page), lambda b, c, pt: (b, 0, c)),
                   pl.BlockSpec((None, nh, page), lambda b, c, pt: (b, 0, 0))],
        scratch_shapes=[pltpu.VMEM((nh, 1), F32)])
    return pl.pallas_call(
        body, grid_spec=grid_spec,
        out_shape=[SDS((bs, nh, n_pages * page), F32), SDS((bs, nh, page), F32)],
        compiler_params=_cparams(("parallel", "arbitrary")), name="fox_f_sample",
    )(page_table, new_t, *([logf_pool_t] * pp))


def _fox_s_body(pt_ref, q_ref, kn_ref, vn_ref, fq_ref, fk_ref, fn_ref, *rest, pp, n_chunks, ts, page, offset,
                scale, g, r):
    pages = rest[:pp]
    o_ref = rest[pp]
    qs_ref, m_ref, l_ref, acc_ref = rest[pp + 1:]
    del pt_ref
    c = pl.program_id(1)
    hd = HEAD_DIM

    @pl.when(c == 0)
    def _():
        for gg in range(g):
            qs_ref[gg] = (_stack_heads(q_ref[:, gg * r * hd:(gg + 1) * r * hd], r, hd) * scale).astype(BF16)
            _flash_init(m_ref.at[gg], l_ref.at[gg], acc_ref.at[gg])

    def fk_rows(f, gg):
        return jnp.concatenate([jnp.broadcast_to(f[gg * r + i:gg * r + i + 1, :], (ts, f.shape[1]))
                                for i in range(r)], axis=0)

    fk = fk_ref[...]
    for gg in range(g):
        ks = [pages[i][0, pl.ds(gg, page, stride=2 * g), :].astype(BF16) for i in range(pp)]
        vs = [pages[i][0, pl.ds(g + gg, page, stride=2 * g), :].astype(BF16) for i in range(pp)]
        s = jnp.concatenate([_dot_nt(qs_ref[gg], k) for k in ks], axis=1) + fq_ref[gg] - fk_rows(fk, gg)
        _flash_update(s, None, vs, m_ref.at[gg], l_ref.at[gg], acc_ref.at[gg])

    @pl.when(c == n_chunks - 1)
    def _():
        t = _rep_rows(offset + lax.broadcasted_iota(I32, (ts, 1), 0), r)
        lane = lax.broadcasted_iota(I32, (1, page), 1)
        outs = []
        for gg in range(g):
            k = _pad_rows(kn_ref[:, gg * hd:(gg + 1) * hd], page).astype(BF16)
            v = _pad_rows(vn_ref[:, gg * hd:(gg + 1) * hd], page).astype(BF16)
            s = _dot_nt(qs_ref[gg], k) + fq_ref[gg] - fk_rows(fn_ref[...], gg)
            _flash_update(s, (offset + lane <= t) & (lane < ts), [v], m_ref.at[gg], l_ref.at[gg], acc_ref.at[gg])
            outs.append(_unstack_heads(_flash_final(l_ref.at[gg], acc_ref.at[gg]), r, ts))
        o_ref[...] = jnp.concatenate(outs, axis=1)


def fox_sample(hc, fq, fk, fn, pool, page_table, cols, *, n_p, bs, ts, page, g, r):
    hd = HEAD_DIM
    n_pages = page_table.shape[1]
    nh = g * r
    offset = n_pages * page
    pp = _pick(n_pages, (32, 16, 8, 4, 2, 1))
    n_chunks = n_pages // pp
    rb = n_p // ts
    body = functools.partial(_fox_s_body, pp=pp, n_chunks=n_chunks, ts=ts, page=page, offset=offset,
                             scale=hd ** -0.5, g=g, r=r)

    def row_spec(width, col0):
        return pl.BlockSpec((ts, width), lambda b, c, pt: (rb + b, col0 // width))

    def page_spec(i):
        return pl.BlockSpec((1, page * 2 * g, hd), lambda b, c, pt: (pt[b, c * pp + i], 0, 0))

    in_specs = [row_spec(nh * hd, cols["q"]), row_spec(g * hd, cols["k"]), row_spec(g * hd, cols["v"]),
                pl.BlockSpec((None, g, r * ts, 1), lambda b, c, pt: (b, 0, 0, 0)),
                pl.BlockSpec((None, nh, pp * page), lambda b, c, pt: (b, 0, c)),
                pl.BlockSpec((None, nh, page), lambda b, c, pt: (b, 0, 0))]
    in_specs += [page_spec(i) for i in range(pp)]
    grid_spec = pltpu.PrefetchScalarGridSpec(
        num_scalar_prefetch=1, grid=(bs, n_chunks), in_specs=in_specs,
        out_specs=pl.BlockSpec((ts, nh * hd), lambda b, c, pt: (b, 0)),
        scratch_shapes=[pltpu.VMEM((g, r * ts, hd), BF16), pltpu.VMEM((g, r * ts, 1), F32),
                        pltpu.VMEM((g, r * ts, 1), F32), pltpu.VMEM((g, r * ts, hd), F32)])
    return pl.pallas_call(
        body, grid_spec=grid_spec, out_shape=SDS((bs * ts, nh * hd), F32),
        compiler_params=_cparams(("parallel", "arbitrary")), name="fox_sample",
    )(page_table, hc, hc, hc, fq, fk, fn, *([pool] * pp))


def _ab_layout(nh_a, g, r, nh_r):
    hd = HEAD_DIM
    qa, kv = nh_a * hd, g * hd
    sizes = [("q", qa), ("kc", kv), ("vc", kv), ("ks", kv), ("vs", kv), ("kw", kv), ("vw", kv),
             ("gt_src", nh_a * 3), ("rq", nh_r * hd), ("rk", nh_r * hd), ("rv", nh_r * hd), ("rg", nh_r * hd)]
    src = {}
    pos = 0
    for name, w in sizes:
        src[name] = pos
        pos += w
    order = ["q", "rq", "rk", "rv", "rg", "kc", "vc", "ks", "vs", "kw", "vw"]
    widths = dict(sizes)
    idx = []
    cols = {}
    for name in order:
        cols[name] = len(idx)
        idx += list(range(src[name], src[name] + widths[name]))
    cols["gt"] = len(idx)
    for gg in range(g):
        blk = [-1] * LANES
        for c in range(3):
            for i in range(r):
                blk[c * r + i] = src["gt_src"] + (gg * r + i) * 3 + c
        idx += blk
    return np.array(idx, np.int32), cols


def _gather_cols(w, idx):
    parts = []
    i, n = 0, len(idx)
    while i < n:
        j = i + 1
        if idx[i] < 0:
            while j < n and idx[j] < 0:
                j += 1
            parts.append(jnp.zeros((w.shape[0], j - i), w.dtype))
        else:
            while j < n and idx[j] == idx[j - 1] + 1:
                j += 1
            parts.append(w[:, int(idx[i]):int(idx[i]) + (j - i)])
        i = j
    return jnp.concatenate(parts, axis=1)


def _even_layer(x, n_p, b, t, bs, ts, p, cache_kv, cache_win, state, page_table):
    hd = HEAD_DIM
    d = x.shape[1]
    nh_a = d // (2 * hd)
    g = nh_a // 4
    r = nh_a // g
    nh_r = d // (2 * hd)
    idx, cols = _ab_layout(nh_a, g, r, nh_r)
    tn = 768
    ncol = -(-len(idx) // tn) * tn
    idx = np.concatenate([idx, np.full(ncol - len(idx), -1, np.int32)])
    w_in = _gather_cols(p["w_in"], idx).astype(BF16)
    hab, _ = mm_norm(x, p["norm"], w_in, tn)

    kv4 = 4 * g * hd
    c_rows = cols["kc"]
    c_win = cols["kw"]
    new_rows_p = hab[:n_p, c_rows:c_rows + kv4].reshape(b, t, 4, g, hd)
    new_rows_s = hab[n_p:, c_rows:c_rows + kv4].reshape(bs, ts, 4, g, hd)
    new_win_p = hab[:n_p, c_win:c_win + 2 * g * hd].reshape(b, t, 2, g, hd)
    new_win_s = hab[n_p:, c_win:c_win + 2 * g * hd].reshape(bs, ts, 2, g, hd)
    win_state_p = new_win_p[:, -min(WINDOW, t):]
    wb = cache_win.shape[1]
    win_state_s = jnp.concatenate([cache_win, new_win_s], axis=1)[:, -wb:]

    slopes = jnp.exp2(-8.0 * (jnp.arange(nh_a, dtype=F32) + 1.0) / nh_a).reshape(g, r)
    w1k, w2k = p["w1_k"].astype(BF16), p["w2_k"].astype(BF16)
    w1v, w2v = p["w1_v"].astype(BF16), p["w2_v"].astype(BF16)

    def cmp_pair(blocks):
        bb, nb = blocks.shape[:2]
        flat = blocks.transpose(3, 0, 1, 4, 2, 5).reshape(2, bb * nb * g, CMP_BLOCK * hd)
        kc = compress(flat[0], p["pe_k"].reshape(-1), w1k, w2k).reshape(bb, nb, g, hd).transpose(0, 2, 1, 3)
        vc = compress(flat[1], p["pe_v"].reshape(-1), w1v, w2v).reshape(bb, nb, g, hd).transpose(0, 2, 1, 3)
        return kc, vc

    nb_p = t // CMP_BLOCK
    kc_p, vc_p = cmp_pair(new_rows_p[:, :nb_p * CMP_BLOCK, 0:2].reshape(b, nb_p, CMP_BLOCK, 2, g, hd))
    o_a_p = nsa_prompt(hab, kc_p, vc_p, slopes, cols, b=b, t=t, g=g, r=r)

    page = cache_kv.shape[1]
    n_pages = page_table.shape[1]
    nb_s = (n_pages * page + ts) // CMP_BLOCK
    assert nb_s * CMP_BLOCK == n_pages * page and page % CMP_BLOCK == 0
    pool = cache_kv.reshape(cache_kv.shape[0], page * 4 * g, hd)
    kc_s, vc_s = compress_sample(pool, page_table, p, page=page, g=g)
    o_a_s = nsa_sample(hab, kc_s, vc_s, pool, cache_win.reshape(bs, wb * 2 * g, hd), page_table, slopes, cols,
                       n_p=n_p, bs=bs, ts=ts, page=page, g=g, r=r)

    o_b_p, st_p = retention(hab, p["gn"], jnp.zeros((b, nh_r, hd, hd), F32), cols, row0=0, b=b, t=t, nh=nh_r)
    o_b_s, st_s = retention(hab, p["gn"], state, cols, row0=n_p, b=bs, t=ts, nh=nh_r)

    o = jnp.concatenate([jnp.concatenate([o_a_p, o_b_p.astype(BF16)], axis=1),
                         jnp.concatenate([o_a_s.astype(BF16), o_b_s.astype(BF16)], axis=1)], axis=0)
    x = mm_res(o, p["w_out"].astype(BF16), x)
    return x, (new_rows_p, win_state_p, st_p, new_rows_s, win_state_s, st_s)


def _odd_layer(x, n_p, b, t, bs, ts, p, cache_kv, cache_logf, page_table):
    hd = HEAD_DIM
    d = x.shape[1]
    nh = d // hd
    g = nh // 4
    r = nh // g
    cols = {"q": 0, "k": nh * hd, "v": (nh + g) * hd, "f": (nh + 2 * g) * hd}
    ncol_src = p["w_in"].shape[1]
    tn = 640
    ncol = -(-(cols["f"] + LANES) // tn) * tn
    w_in = jnp.pad(p["w_in"], ((0, 0), (0, ncol - ncol_src))).astype(BF16)
    hc, _ = mm_norm(x, p["norm"], w_in, tn)

    new_rows_p = hc[:n_p, cols["k"]:cols["f"]].reshape(b, t, 2, g, hd)
    new_rows_s = hc[n_p:, cols["k"]:cols["f"]].reshape(bs, ts, 2, g, hd)
    b_pad = jnp.pad(p["b_f"].astype(F32), (0, LANES - nh)).reshape(1, LANES)
    logf = fox_logf(hc, b_pad, cols["f"])
    new_logf_p = logf[:n_p, :nh].reshape(b, t, nh)
    new_logf_s = logf[n_p:, :nh].reshape(bs, ts, nh)

    f_p, *f_split = [a[:, :nh].reshape(b, t, g, r) for a in cumsum_rows(logf, b=b, t=t)]
    o_p = fox_prompt(hc, f_p, f_split, cols, b=b, t=t, g=g, r=r)

    page = cache_kv.shape[1]
    new_t = jnp.pad(new_logf_s.transpose(0, 2, 1), ((0, 0), (0, 0), (0, page - ts)))
    fk, fn = fox_f_sample(cache_logf.transpose(0, 2, 1), new_t, page_table)
    fq = fn[:, :, :ts].reshape(bs, g, r * ts, 1)
    o_s = fox_sample(hc, fq, fk, fn, cache_kv.reshape(cache_kv.shape[0], page * 2 * g, hd), page_table, cols,
                     n_p=n_p, bs=bs, ts=ts, page=page, g=g, r=r)

    x = mm_res(jnp.concatenate([o_p, o_s.astype(BF16)], axis=0), p["w_out"].astype(BF16), x)
    return x, (new_rows_p, new_logf_p, new_rows_s, new_logf_s)


def _peer_layer(x, nw, wq, k1, k2, u_bf, vt_bf, layer):
    q, xn = mm_norm(x, nw, wq.astype(BF16), _pick(wq.shape[1], (1024, 512, 256, 128)))
    s1t, s2t, st = peer_score(q, k1, k2)
    return peer_dense(xn, u_bf, vt_bf, layer, s1t, s2t, st, x)


def kernel(x_prompt, x_sample, cache_nsa_kv, cache_nsa_win, state_ret, cache_fox_kv, cache_fox_logf,
           page_table, norm_mix, norm_ffn, norm_final, w_in_ab, w_out_ab, cmp_pe_k, cmp_w1_k, cmp_w2_k,
           cmp_pe_v, cmp_w1_v, cmp_w2_v, ret_gn, w_in_c, b_forget, w_out_c, peer_wq, peer_k1, peer_k2,
           peer_u, peer_v):
    b, t, d = x_prompt.shape
    bs, ts, _ = x_sample.shape
    n_p, n_s = b * t, bs * ts
    depth = norm_mix.shape[0]
    x = jnp.concatenate([x_prompt.reshape(n_p, d), x_sample.reshape(n_s, d)], axis=0)
    u_bf = peer_u.astype(BF16)
    vt_bf = jnp.swapaxes(peer_v, 1, 2).astype(BF16)
    even, odd = [], []
    for l in range(depth):
        if l % 2 == 0:
            e = l // 2
            p = dict(norm=norm_mix[l], w_in=w_in_ab[e], w_out=w_out_ab[e], pe_k=cmp_pe_k[e], w1_k=cmp_w1_k[e],
                     w2_k=cmp_w2_k[e], pe_v=cmp_pe_v[e], w1_v=cmp_w1_v[e], w2_v=cmp_w2_v[e], gn=ret_gn[e])
            x, outs = _even_layer(x, n_p, b, t, bs, ts, p, cache_nsa_kv[e], cache_nsa_win[e], state_ret[e],
                                  page_table)
            even.append(outs)
        else:
            o = l // 2
            p = dict(norm=norm_mix[l], w_in=w_in_c[o], b_f=b_forget[o], w_out=w_out_c[o])
            x, outs = _odd_layer(x, n_p, b, t, bs, ts, p, cache_fox_kv[o], cache_fox_logf[o], page_table)
            odd.append(outs)
        x = _peer_layer(x, norm_ffn[l], peer_wq[l], peer_k1[l], peer_k2[l], u_bf, vt_bf, l)
    y = rms_final(x, norm_final)
    y_prompt = y[:n_p].reshape(b, t, d)
    y_sample = y[n_p:].reshape(bs, ts, d)

    def stack(group, i):
        return jnp.stack([o[i] for o in group])

    return (y_prompt, y_sample, stack(even, 0), stack(even, 1), stack(even, 2), stack(odd, 0), stack(odd, 1),
            stack(even, 3), stack(even, 4), stack(even, 5), stack(odd, 2), stack(odd, 3))
```

```python
import functools
import math

import numpy as np
import jax
import jax.numpy as jnp
from jax import lax
from jax.experimental import pallas as pl
from jax.experimental.pallas import tpu as pltpu

F32 = jnp.float32
BF16 = jnp.bfloat16
I32 = jnp.int32

HEAD_DIM = 128
CMP_BLOCK = 64
SEL_BLOCK = 64
SEL_TOPK = 16
WINDOW = 512
CMP_HIDDEN = 256
RET_CHUNK = 128
Q_BLOCK = 128
PEER_HEADS = 8
PEER_DK = 256
PEER_TOPK = 16
EPS = 1e-6
GN_EPS = 1e-5
NEG = -1e30

LANES = 128
SUBLANES = 8
VMEM_LIMIT = 56 * 1024 * 1024

SDS = jax.ShapeDtypeStruct


def _cparams(sem):
    return pltpu.CompilerParams(dimension_semantics=sem, vmem_limit_bytes=VMEM_LIMIT)


def _dot(a, b):
    return jnp.dot(a, b, preferred_element_type=F32)


def _dot_nt(a, b):
    return lax.dot_general(a, b, (((1,), (1,)), ((), ())), preferred_element_type=F32)


def _dot_tn(a, b):
    return lax.dot_general(a, b, (((0,), (0,)), ((), ())), preferred_element_type=F32)


def _pick(n, cands):
    for c in cands:
        if c <= n and n % c == 0:
            return c
    raise ValueError(f"no tile for {n} in {cands}")


def _gelu(x):
    c = math.sqrt(2.0 / math.pi)
    return x * (0.5 * (1.0 + jnp.tanh(c * (x + 0.044715 * (x * x * x)))))


def _split3(x):
    hi = x.astype(BF16)
    r1 = x - hi.astype(F32)
    mid = r1.astype(BF16)
    lo = (r1 - mid.astype(F32)).astype(BF16)
    return hi, mid, lo


def _stack_heads(x, nh, hd):
    return jnp.concatenate([x[:, r * hd:(r + 1) * hd] for r in range(nh)], axis=0)


def _unstack_heads(x, nh, t):
    return jnp.concatenate([x[r * t:(r + 1) * t, :] for r in range(nh)], axis=1)


def _rep_rows(x, k):
    return jnp.concatenate([x] * k, axis=0)


def _mm_norm_body(x_ref, nw_ref, w_ref, o_ref, xn_ref):
    @pl.when(pl.program_id(1) == 0)
    def _():
        x = x_ref[...]
        ms = jnp.mean(x * x, axis=-1, keepdims=True)
        xn_ref[...] = (x * lax.rsqrt(ms + EPS) * nw_ref[...]).astype(BF16)

    o_ref[...] = _dot(xn_ref[...], w_ref[...])


def mm_norm(x, nw, w_bf, tn):
    n, d = x.shape
    nn = w_bf.shape[1]
    tm = _pick(n, (768, 512, 256, 128, 64, 32, 16))
    return pl.pallas_call(
        _mm_norm_body,
        grid=(n // tm, nn // tn),
        in_specs=[pl.BlockSpec((tm, d), lambda i, j: (i, 0)),
                  pl.BlockSpec((1, d), lambda i, j: (0, 0)),
                  pl.BlockSpec((d, tn), lambda i, j: (0, j))],
        out_specs=[pl.BlockSpec((tm, tn), lambda i, j: (i, j)),
                   pl.BlockSpec((tm, d), lambda i, j: (i, 0))],
        out_shape=[SDS((n, nn), F32), SDS((n, d), BF16)],
        compiler_params=_cparams(("parallel", "arbitrary")),
        name="mm_norm",
    )(x, nw.reshape(1, d), w_bf)


def _mm_res_body(a_ref, w_ref, r_ref, o_ref):
    o_ref[...] = r_ref[...] + _dot(a_ref[...], w_ref[...])


def mm_res(a_bf, w_bf, res):
    n, k = a_bf.shape
    nn = w_bf.shape[1]
    tm = _pick(n, (768, 512, 256, 128, 64, 32, 16))
    tn = _pick(nn, (1024, 512, 256, 128))
    return pl.pallas_call(
        _mm_res_body,
        grid=(n // tm, nn // tn),
        in_specs=[pl.BlockSpec((tm, k), lambda i, j: (i, 0)),
                  pl.BlockSpec((k, tn), lambda i, j: (0, j)),
                  pl.BlockSpec((tm, tn), lambda i, j: (i, j))],
        out_specs=pl.BlockSpec((tm, tn), lambda i, j: (i, j)),
        out_shape=SDS((n, nn), F32),
        compiler_params=_cparams(("parallel", "arbitrary")),
        name="mm_res",
    )(a_bf, w_bf, res)


def _rms_body(x_ref, nw_ref, o_ref):
    x = x_ref[...]
    ms = jnp.mean(x * x, axis=-1, keepdims=True)
    o_ref[...] = x * lax.rsqrt(ms + EPS) * nw_ref[...]


def rms_final(x, nw):
    n, d = x.shape
    tm = _pick(n, (768, 512, 256, 128, 64, 32, 16, 8))
    return pl.pallas_call(
        _rms_body,
        grid=(n // tm,),
        in_specs=[pl.BlockSpec((tm, d), lambda i: (i, 0)), pl.BlockSpec((1, d), lambda i: (0, 0))],
        out_specs=pl.BlockSpec((tm, d), lambda i: (i, 0)),
        out_shape=SDS((n, d), F32),
        compiler_params=_cparams(("parallel",)),
        name="rms_final",
    )(x, nw.reshape(1, d))


def _batcher_pairs(n):
    pairs = []
    p = 1
    while p < n:
        k = p
        while k >= 1:
            for j in range(k % p, n - k, 2 * k):
                for i in range(min(k, n - j - k)):
                    if (i + j) // (2 * p) == (i + j + k) // (2 * p):
                        pairs.append((i + j, i + j + k))
            k //= 2
        p *= 2
    return pairs


_SORT16 = _batcher_pairs(16)


def _sort16_desc(xs):
    xs = list(xs)
    for i, j in _SORT16:
        hi = jnp.maximum(xs[i], xs[j])
        lo = jnp.minimum(xs[i], xs[j])
        xs[i], xs[j] = hi, lo
    return xs


def _bitonic16_desc(c):
    c = list(c)
    for stride in (8, 4, 2, 1):
        for i in range(16):
            if i & stride == 0:
                hi = jnp.maximum(c[i], c[i + stride])
                lo = jnp.minimum(c[i], c[i + stride])
                c[i], c[i + stride] = hi, lo
    return c


def _merge16_desc(a, b):
    return _bitonic16_desc([jnp.maximum(a[i], b[15 - i]) for i in range(16)])


def _top16_sorted(s):
    cols = _sort16_desc([s[v * SUBLANES:(v + 1) * SUBLANES, :] for v in range(16)])
    for shift in (4, 2, 1):
        other = [pltpu.roll(x, shift, 0) for x in cols]
        cols = _merge16_desc(cols, other)
    return cols


def _peer_score_body(q_ref, k1_ref, k2_ref, s1_ref, s2_ref, st_ref):
    tm = q_ref.shape[0]
    half = PEER_DK // 2
    k1 = k1_ref[...].astype(BF16)
    k2 = k2_ref[...].astype(BF16)
    sub = lax.broadcasted_iota(I32, (SUBLANES, tm), 0)
    a_top = None
    b_top = None
    for h in range(PEER_HEADS):
        qh = q_ref[:, h * PEER_DK:(h + 1) * PEER_DK]
        s1 = _dot_nt(k1, qh[:, :half].astype(BF16))
        s2 = _dot_nt(k2, qh[:, half:].astype(BF16))
        s1_ref[h] = s1
        s2_ref[h] = s2
        a_h = _top16_sorted(s1)
        b_h = _top16_sorted(s2)
        if h == 0:
            a_top, b_top = a_h, b_h
        else:
            a_top = [jnp.where(sub == h, x, y) for x, y in zip(a_h, a_top)]
            b_top = [jnp.where(sub == h, x, y) for x, y in zip(b_h, b_top)]
    ninf = jnp.full((SUBLANES, tm), -jnp.inf, F32)
    row0 = [a_top[0] + b_top[b] for b in range(16)]
    col0 = [a_top[a] + b_top[0] for a in range(1, 16)] + [ninf]
    mid = ([a_top[1] + b_top[b] for b in range(1, 8)] + [a_top[a] + b_top[1] for a in range(2, 8)]
           + [a_top[2] + b_top[b] for b in range(2, 5)])
    mid = _sort16_desc(mid)
    v0 = a_top[3] + b_top[2]
    v1 = a_top[4] + b_top[2]
    v2 = a_top[3] + b_top[3]
    tail = [v0, jnp.maximum(v1, v2), jnp.minimum(v1, v2)] + [ninf] * 13
    top = _merge16_desc(_merge16_desc(_merge16_desc(row0, col0), mid), tail)
    z = jnp.zeros((SUBLANES, tm), F32)
    for i in range(16):
        z = z + jnp.exp(top[i] - top[0])
    st_ref[0] = top[15]
    st_ref[1] = a_top[0]
    st_ref[2] = b_top[0]
    st_ref[3] = z


def peer_score(q, k1, k2):
    n, d = q.shape
    nk = k1.shape[0]
    assert nk == 128 and d == PEER_HEADS * PEER_DK
    tm = _pick(n, (256, 128))
    return pl.pallas_call(
        _peer_score_body,
        grid=(n // tm,),
        in_specs=[pl.BlockSpec((tm, d), lambda i: (i, 0)),
                  pl.BlockSpec(k1.shape, lambda i: (0, 0)),
                  pl.BlockSpec(k2.shape, lambda i: (0, 0))],
        out_specs=[pl.BlockSpec((PEER_HEADS, nk, tm), lambda i: (0, 0, i)),
                   pl.BlockSpec((PEER_HEADS, nk, tm), lambda i: (0, 0, i)),
                   pl.BlockSpec((4, PEER_HEADS, tm), lambda i: (0, 0, i))],
        out_shape=[SDS((PEER_HEADS, nk, n), F32), SDS((PEER_HEADS, nk, n), F32),
                   SDS((4, PEER_HEADS, n), F32)],
        compiler_params=_cparams(("parallel",)),
        name="peer_score",
    )(q, k1, k2)


def _peer_dense_body(xn_ref, u_ref, vt_ref, s1_ref, s2_ref, st_ref, res_ref, o_ref,
                     acc_ref, e2_ref, h_ref, act_ref, *, nc, n_steps):
    c = pl.program_id(1)
    nk = s2_ref.shape[1]

    @pl.when(c == 0)
    def _():
        acc_ref[...] = jnp.zeros_like(acc_ref)
        for h in range(PEER_HEADS):
            e2_ref[h] = jnp.exp(s2_ref[h] - st_ref[2, h:h + 1, :])

    h_ref[...] = _dot_nt(u_ref[...], xn_ref[...])
    for s in range(nc):
        e1 = c * nc + s
        w = jnp.zeros((nk, xn_ref.shape[0]), F32)
        for h in range(PEER_HEADS):
            s1row = s1_ref[h, pl.ds(e1, 1), :]
            p1 = jnp.exp(s1row - st_ref[1, h:h + 1, :]) / st_ref[3, h:h + 1, :]
            a = s1row + s2_ref[h]
            w = w + jnp.where(a >= st_ref[0, h:h + 1, :], p1 * e2_ref[h], 0.0)
        act_ref[s * nk:(s + 1) * nk, :] = (_gelu(h_ref[s * nk:(s + 1) * nk, :]) * w).astype(BF16)
    acc_ref[...] += _dot(vt_ref[...], act_ref[...])

    @pl.when(c == n_steps - 1)
    def _():
        o_ref[...] = res_ref[...] + acc_ref[...].T


def peer_dense(xn_bf, u_bf, vt_bf, layer, s1t, s2t, st, res):
    n, d = xn_bf.shape
    ne = u_bf.shape[1]
    nk = s1t.shape[1]
    tm = _pick(n, (768, 512, 256, 128))
    nc = 8
    te = nc * nk
    n_steps = ne // te
    body = functools.partial(_peer_dense_body, nc=nc, n_steps=n_steps)
    once = pl.Buffered(1)
    return pl.pallas_call(
        body,
        grid=(n // tm, n_steps),
        in_specs=[pl.BlockSpec((tm, d), lambda i, c: (i, 0), pipeline_mode=once),
                  pl.BlockSpec((None, te, d), lambda i, c: (layer, c, 0)),
                  pl.BlockSpec((None, d, te), lambda i, c: (layer, 0, c)),
                  pl.BlockSpec((PEER_HEADS, nk, tm), lambda i, c: (0, 0, i), pipeline_mode=once),
                  pl.BlockSpec((PEER_HEADS, nk, tm), lambda i, c: (0, 0, i), pipeline_mode=once),
                  pl.BlockSpec((4, PEER_HEADS, tm), lambda i, c: (0, 0, i), pipeline_mode=once),
                  pl.BlockSpec((tm, d), lambda i, c: (i, 0), pipeline_mode=once)],
        out_specs=pl.BlockSpec((tm, d), lambda i, c: (i, 0), pipeline_mode=once),
        out_shape=SDS((n, d), F32),
        scratch_shapes=[pltpu.VMEM((d, tm), F32), pltpu.VMEM((PEER_HEADS, nk, tm), F32),
                        pltpu.VMEM((te, tm), F32), pltpu.VMEM((te, tm), BF16)],
        compiler_params=_cparams(("parallel", "arbitrary")),
        name="peer_dense",
    )(xn_bf, u_bf, vt_bf, s1t, s2t, st, res)


def _flash_init(m_ref, l_ref, acc_ref):
    m_ref[...] = jnp.full(m_ref.shape, NEG, F32)
    l_ref[...] = jnp.zeros(l_ref.shape, F32)
    acc_ref[...] = jnp.zeros(acc_ref.shape, F32)


def _flash_update(s, mask, v_tiles, m_ref, l_ref, acc_ref):
    if mask is not None:
        s = jnp.where(mask, s, NEG)
    m_prev = m_ref[...]
    m_new = jnp.maximum(m_prev, jnp.max(s, axis=-1, keepdims=True))
    alpha = jnp.exp(m_prev - m_new)
    p = jnp.exp(s - m_new)
    if mask is not None:
        p = jnp.where(mask, p, 0.0)
    l_ref[...] = alpha * l_ref[...] + jnp.sum(p, axis=-1, keepdims=True)
    pb = p.astype(BF16)
    pv = None
    off = 0
    for v in v_tiles:
        part = _dot(pb[:, off:off + v.shape[0]], v)
        pv = part if pv is None else pv + part
        off += v.shape[0]
    acc_ref[...] = alpha * acc_ref[...] + pv
    m_ref[...] = m_new


def _flash_final(l_ref, acc_ref):
    return acc_ref[...] / jnp.maximum(l_ref[...], 1e-30)


def _softmax_masked(s, mask):
    s = jnp.where(mask, s, NEG)
    m = jnp.max(s, axis=-1, keepdims=True)
    e = jnp.where(mask, jnp.exp(s - m), 0.0)
    return e / jnp.maximum(jnp.sum(e, axis=-1, keepdims=True), 1e-30)


def _topk_mask(score, k):
    n = score.shape[-1]
    lane = lax.broadcasted_iota(I32, score.shape, 1).astype(F32)
    sel = jnp.zeros(score.shape, F32)
    for _ in range(k):
        m = jnp.max(score, axis=-1, keepdims=True)
        idx = jnp.min(jnp.where(score == m, lane, float(n)), axis=-1, keepdims=True)
        hit = lane == idx
        sel = jnp.where(hit, jnp.where(m >= 0.0, 1.0, 0.0), sel)
        score = jnp.where(hit, -jnp.inf, score)
    return sel


def _compress_body(x_ref, pe_ref, w1_ref, w2_ref, o_ref):
    xb = (x_ref[...] + pe_ref[...]).astype(BF16)
    h = _gelu(_dot(xb, w1_ref[...]))
    o_ref[...] = _dot(h.astype(BF16), w2_ref[...])


def compress(x, pe, w1_bf, w2_bf):
    rows, k = x.shape
    tr = _pick(rows, (256, 128, 64, 32, 16, 8))
    hid = w1_bf.shape[1]
    hd = w2_bf.shape[1]
    return pl.pallas_call(
        _compress_body,
        grid=(rows // tr,),
        in_specs=[pl.BlockSpec((tr, k), lambda i: (i, 0)),
                  pl.BlockSpec((1, k), lambda i: (0, 0)),
                  pl.BlockSpec((k, hid), lambda i: (0, 0)),
                  pl.BlockSpec((hid, hd), lambda i: (0, 0))],
        out_specs=pl.BlockSpec((tr, hd), lambda i: (i, 0)),
        out_shape=SDS((rows, hd), F32),
        compiler_params=_cparams(("parallel",)),
        name="nsa_compress",
    )(x, pe.reshape(1, k), w1_bf, w2_bf)


def _nsa_cmp_and_select(qs, slope, t1, kc_ref, vc_ref, *, r, nb, n_sel, ns_pad, k_top):
    tq = t1.shape[0]
    t = _rep_rows(t1, r)
    blk_end = lax.broadcasted_iota(I32, (1, nb), 1) * CMP_BLOCK + (CMP_BLOCK - 1)
    d_c = t - blk_end
    s = _dot_nt(qs, kc_ref[...].astype(BF16)) - slope * d_c.astype(F32)
    p_c = _softmax_masked(s, d_c >= 0)
    o_c = _dot(p_c.astype(BF16), vc_ref[...].astype(BF16))
    imp = p_c[0:tq]
    for i in range(1, r):
        imp = imp + p_c[i * tq:(i + 1) * tq]
    if ns_pad > nb:
        imp = jnp.concatenate([imp, jnp.zeros((tq, ns_pad - nb), F32)], axis=1)
    jsel = lax.broadcasted_iota(I32, (1, ns_pad), 1)
    cur = jnp.right_shift(t1, 6)
    forced = (jsel == 0) | (jsel == cur) | (jsel == cur - 1)
    score = jnp.where(forced, r + 1.0, jnp.where(jsel <= cur, imp, -1.0))
    score = jnp.where(jsel < n_sel, score, -2.0)
    return o_c, _topk_mask(score, k_top)


def _gate_cols(gsig, c, r):
    return jnp.concatenate([gsig[:, c * r + i:c * r + i + 1] for i in range(r)], axis=0)


def _flash_update_t(s, mask, v, m_ref, l_ref, acc_ref):
    if mask is not None:
        s = jnp.where(mask, s, NEG)
    m_prev = m_ref[...]
    m_new = jnp.maximum(m_prev, jnp.max(s, axis=0, keepdims=True))
    alpha = jnp.exp(m_prev - m_new)
    p = jnp.exp(s - m_new)
    if mask is not None:
        p = jnp.where(mask, p, 0.0)
    l_ref[...] = alpha * l_ref[...] + jnp.sum(p, axis=0, keepdims=True)
    acc_ref[...] = alpha * acc_ref[...] + _dot_tn(v, p.astype(BF16))
    m_ref[...] = m_new


def _nsa_p_body(q_ref, gt_ref, kc_ref, vc_ref, ks_ref, vs_ref, kw_ref, vw_ref, sl_ref, kp_ref, o_ref,
                m_ref, l_ref, acc_ref, *, tq, tk, wk, nb, n_sel, k_top, scale, r):
    qi = pl.program_id(2)
    a = qi * tq
    rows = r * tq
    qs = (_stack_heads(q_ref[...], r, HEAD_DIM) * scale).astype(BF16)
    slope = sl_ref[...]
    t1 = a + lax.broadcasted_iota(I32, (tq, 1), 0)
    t = _rep_rows(t1, r)
    o_c, sel = _nsa_cmp_and_select(qs, slope, t1, kc_ref, vc_ref, r=r, nb=nb, n_sel=n_sel,
                                   ns_pad=n_sel, k_top=k_top)
    sel_bf = sel.astype(BF16)
    lane = lax.broadcasted_iota(I32, (rows, LANES), 1)
    t_hi = jnp.right_shift(t, 6).astype(F32)
    t_lo = jnp.bitwise_and(t, SEL_BLOCK - 1).astype(F32)
    qb = jnp.where(lane == 0, slope * 64.0,
                   jnp.where(lane == 1, slope,
                             jnp.where(lane == 2, -(slope * 64.0) * t_hi,
                                       jnp.where(lane == 3, -slope * t_lo, 0.0)))).astype(BF16)
    qa = jnp.concatenate([qs, qb], axis=1)
    t_row = a + jnp.concatenate([lax.broadcasted_iota(I32, (1, tq), 1)] * r, axis=1)

    def attend(k_ref_, v_ref_, start, width, window):
        ka = jnp.concatenate([k_ref_[pl.ds(start, width), :].astype(BF16), kp_ref[pl.ds(start, width), :]], axis=1)
        v = v_ref_[pl.ds(start, width), :].astype(BF16)
        s = _dot_nt(ka, qa)
        pos = start + lax.broadcasted_iota(I32, (width, 1), 0)
        mask = pos <= t_row
        if window:
            mask = mask & (pos >= t_row - WINDOW)
        else:
            blk = lax.broadcasted_iota(I32, (width, n_sel), 1)
            key_blk = jnp.right_shift(start + lax.broadcasted_iota(I32, (width, n_sel), 0), 6)
            bm = _dot_nt(jnp.where(blk == key_blk, 1.0, 0.0).astype(BF16), sel_bf)
            mask = mask & (jnp.concatenate([bm] * r, axis=1) > 0.5)
        _flash_update_t(s, mask, v, m_ref, l_ref, acc_ref)

    def sel_step(j, carry):
        attend(ks_ref, vs_ref, pl.multiple_of(j * tk, tk), tk, False)
        return carry

    _flash_init(m_ref, l_ref, acc_ref)
    lax.fori_loop(0, (a + tq - 1) // tk + 1, sel_step, 0)
    o_s = (acc_ref[...] / jnp.maximum(l_ref[...], 1e-30)).T

    _flash_init(m_ref, l_ref, acc_ref)
    attend(kw_ref, vw_ref, pl.multiple_of(jnp.maximum(a + tq - wk, 0), SUBLANES), wk, True)
    o_w = (acc_ref[...] / jnp.maximum(l_ref[...], 1e-30)).T

    gsig = 1.0 / (1.0 + jnp.exp(-gt_ref[...]))
    o = _gate_cols(gsig, 0, r) * o_c + _gate_cols(gsig, 1, r) * o_s + _gate_cols(gsig, 2, r) * o_w
    o_ref[...] = _unstack_heads(o, r, tq).astype(BF16)


def nsa_prompt(hab, kc, vc, slopes, cols, *, b, t, g, r):
    hd = HEAD_DIM
    nb = kc.shape[2]
    n_sel = -(-t // SEL_BLOCK)
    assert n_sel == nb and t % SEL_BLOCK == 0
    tq = _pick(t, (4 * Q_BLOCK, 2 * Q_BLOCK, Q_BLOCK, t))
    tk = _pick(t, (512, 256, 128, t))
    wk = min(WINDOW + tq, t)
    nq = t // tq
    qw = r * hd
    body = functools.partial(_nsa_p_body, tq=tq, tk=tk, wk=wk, nb=nb, n_sel=n_sel, k_top=min(SEL_TOPK, n_sel),
                             scale=hd ** -0.5, r=r)
    assert t <= 256 * SEL_BLOCK
    pos = np.arange(t)
    kp = np.zeros((t, LANES), np.float32)
    kp[:, 0], kp[:, 1], kp[:, 2], kp[:, 3] = pos >> 6, pos & (SEL_BLOCK - 1), 1.0, 1.0
    kpos = jnp.asarray(kp, BF16)

    def seq_spec(col0):
        return pl.BlockSpec((t, hd), lambda bb, gg, qi: (bb, col0 // hd + gg))

    return pl.pallas_call(
        body,
        grid=(b, g, nq),
        in_specs=[pl.BlockSpec((tq, qw), lambda bb, gg, qi: (bb * nq + qi, gg)),
                  pl.BlockSpec((tq, LANES), lambda bb, gg, qi: (bb * nq + qi, cols["gt"] // LANES + gg)),
                  pl.BlockSpec((None, None, nb, hd), lambda bb, gg, qi: (bb, gg, 0, 0)),
                  pl.BlockSpec((None, None, nb, hd), lambda bb, gg, qi: (bb, gg, 0, 0)),
                  seq_spec(cols["ks"]), seq_spec(cols["vs"]), seq_spec(cols["kw"]), seq_spec(cols["vw"]),
                  pl.BlockSpec((None, r * tq, 1), lambda bb, gg, qi: (gg, 0, 0)),
                  pl.BlockSpec((t, LANES), lambda bb, gg, qi: (0, 0))],
        out_specs=pl.BlockSpec((tq, qw), lambda bb, gg, qi: (bb * nq + qi, gg)),
        out_shape=SDS((b * t, g * qw), BF16),
        scratch_shapes=[pltpu.VMEM((1, r * tq), F32), pltpu.VMEM((1, r * tq), F32),
                        pltpu.VMEM((hd, r * tq), F32)],
        compiler_params=_cparams(("parallel", "parallel", "arbitrary")),
        name="nsa_prompt",
    )(hab, hab, kc, vc, hab, hab, hab, hab, jnp.repeat(slopes, tq, axis=1).reshape(g, r * tq, 1), kpos)


def _pad_rows(x, rows):
    return jnp.concatenate([x, jnp.zeros((rows - x.shape[0], x.shape[1]), x.dtype)], axis=0)


def _block_expand(ns_pad, first_pos, width):
    blk = lax.broadcasted_iota(I32, (ns_pad, width), 0)
    key_blk = jnp.right_shift(first_pos + lax.broadcasted_iota(I32, (ns_pad, width), 1), 6)
    return jnp.where(blk == key_blk, 1.0, 0.0).astype(BF16)


def _nsa_s_body(pt_ref, q_ref, gt_ref, ksn_ref, vsn_ref, kwn_ref, vwn_ref, kc_ref, vc_ref, win_ref,
                sl_ref, *rest, pp, n_chunks, ts, page, offset, wb, nb, n_sel, ns_pad, k_top, scale, g, r):
    pages = rest[:pp]
    o_ref = rest[pp]
    qs_ref, oc_ref, sel_ref, m_ref, l_ref, acc_ref = rest[pp + 1:]
    del pt_ref
    c = pl.program_id(1)
    hd = HEAD_DIM
    t1 = offset + lax.broadcasted_iota(I32, (ts, 1), 0)
    t = _rep_rows(t1, r)
    lane = lax.broadcasted_iota(I32, (1, page), 1)

    @pl.when(c == 0)
    def _():
        for gg in range(g):
            qs = (_stack_heads(q_ref[:, gg * r * hd:(gg + 1) * r * hd], r, hd) * scale).astype(BF16)
            qs_ref[gg] = qs
            o_c, sel = _nsa_cmp_and_select(qs, sl_ref[gg], t1, kc_ref.at[gg], vc_ref.at[gg], r=r, nb=nb,
                                           n_sel=n_sel, ns_pad=ns_pad, k_top=k_top)
            oc_ref[gg] = o_c
            sel_ref[gg] = _rep_rows(sel, r).astype(BF16)
            _flash_init(m_ref.at[gg], l_ref.at[gg], acc_ref.at[gg])

    width = pp * page
    first = c * width
    d = t - (first + lax.broadcasted_iota(I32, (1, width), 1))
    expand = _block_expand(ns_pad, first, width)
    for gg in range(g):
        ks = [pages[i][0, pl.ds(2 * g + gg, page, stride=4 * g), :].astype(BF16) for i in range(pp)]
        vs = [pages[i][0, pl.ds(3 * g + gg, page, stride=4 * g), :].astype(BF16) for i in range(pp)]
        s = jnp.concatenate([_dot_nt(qs_ref[gg], k) for k in ks], axis=1) - sl_ref[gg] * d.astype(F32)
        bm = _dot(sel_ref[gg], expand)
        _flash_update(s, (bm > 0.5) & (d >= 0), vs, m_ref.at[gg], l_ref.at[gg], acc_ref.at[gg])

    @pl.when(c == n_chunks - 1)
    def _():
        gsig = 1.0 / (1.0 + jnp.exp(-gt_ref[...]))
        dn = t - (offset + lane)
        expand_n = _block_expand(ns_pad, offset, page)
        outs = []
        for gg in range(g):
            k = _pad_rows(ksn_ref[:, gg * hd:(gg + 1) * hd], page).astype(BF16)
            v = _pad_rows(vsn_ref[:, gg * hd:(gg + 1) * hd], page).astype(BF16)
            s = _dot_nt(qs_ref[gg], k) - sl_ref[gg] * dn.astype(F32)
            bm = _dot(sel_ref[gg], expand_n)
            _flash_update(s, (bm > 0.5) & (dn >= 0) & (lane < ts), [v], m_ref.at[gg], l_ref.at[gg],
                          acc_ref.at[gg])
            o_s = _flash_final(l_ref.at[gg], acc_ref.at[gg])
            _flash_init(m_ref.at[gg], l_ref.at[gg], acc_ref.at[gg])
            kw = win_ref[pl.ds(gg, wb, stride=2 * g), :].astype(BF16)
            vw = win_ref[pl.ds(g + gg, wb, stride=2 * g), :].astype(BF16)
            kn = _pad_rows(kwn_ref[:, gg * hd:(gg + 1) * hd], page).astype(BF16)
            vn = _pad_rows(vwn_ref[:, gg * hd:(gg + 1) * hd], page).astype(BF16)
            dw = t - (offset - wb + lax.broadcasted_iota(I32, (1, wb), 1))
            s = jnp.concatenate([_dot_nt(qs_ref[gg], kw) - sl_ref[gg] * dw.astype(F32),
                                 _dot_nt(qs_ref[gg], kn) - sl_ref[gg] * dn.astype(F32)], axis=1)
            mask = jnp.concatenate([(dw >= 0) & (dw <= WINDOW), (dn >= 0) & (dn <= WINDOW) & (lane < ts)], axis=1)
            _flash_update(s, mask, [vw, vn], m_ref.at[gg], l_ref.at[gg], acc_ref.at[gg])
            o_w = _flash_final(l_ref.at[gg], acc_ref.at[gg])
            gs = gsig[:, gg * LANES:(gg + 1) * LANES]
            o = _gate_cols(gs, 0, r) * oc_ref[gg] + _gate_cols(gs, 1, r) * o_s + _gate_cols(gs, 2, r) * o_w
            outs.append(_unstack_heads(o, r, ts))
        o_ref[...] = jnp.concatenate(outs, axis=1)


def nsa_sample(hab, kc, vc, pool, win, page_table, slopes, cols, *, n_p, bs, ts, page, g, r):
    hd = HEAD_DIM
    n_pages = page_table.shape[1]
    offset = n_pages * page
    wb = win.shape[1] // (2 * g)
    assert offset % SEL_BLOCK == 0 and offset - wb >= 0 and wb == WINDOW
    nb = kc.shape[2]
    n_sel = -(-(offset + ts) // SEL_BLOCK)
    ns_pad = -(-n_sel // LANES) * LANES
    pp = _pick(n_pages, (32, 16, 8, 4, 2, 1))
    n_chunks = n_pages // pp
    rb = n_p // ts
    body = functools.partial(_nsa_s_body, pp=pp, n_chunks=n_chunks, ts=ts, page=page, offset=offset, wb=wb,
                             nb=nb, n_sel=n_sel, ns_pad=ns_pad, k_top=min(SEL_TOPK, n_sel),
                             scale=hd ** -0.5, g=g, r=r)

    def row_spec(width, col0):
        return pl.BlockSpec((ts, width), lambda b, c, pt: (rb + b, col0 // width))

    def page_spec(i):
        return pl.BlockSpec((1, page * 4 * g, hd), lambda b, c, pt: (pt[b, c * pp + i], 0, 0))

    in_specs = [row_spec(g * r * hd, cols["q"]), row_spec(g * LANES, cols["gt"]),
                row_spec(g * hd, cols["ks"]), row_spec(g * hd, cols["vs"]),
                row_spec(g * hd, cols["kw"]), row_spec(g * hd, cols["vw"]),
                pl.BlockSpec((None, g, nb, hd), lambda b, c, pt: (b, 0, 0, 0)),
                pl.BlockSpec((None, g, nb, hd), lambda b, c, pt: (b, 0, 0, 0)),
                pl.BlockSpec((None, wb * 2 * g, hd), lambda b, c, pt: (b, 0, 0)),
                pl.BlockSpec((g, r * ts, 1), lambda b, c, pt: (0, 0, 0))]
    in_specs += [page_spec(i) for i in range(pp)]
    grid_spec = pltpu.PrefetchScalarGridSpec(
        num_scalar_prefetch=1, grid=(bs, n_chunks), in_specs=in_specs,
        out_specs=pl.BlockSpec((ts, g * r * hd), lambda b, c, pt: (b, 0)),
        scratch_shapes=[pltpu.VMEM((g, r * ts, hd), BF16), pltpu.VMEM((g, r * ts, hd), F32),
                        pltpu.VMEM((g, r * ts, ns_pad), BF16), pltpu.VMEM((g, r * ts, 1), F32),
                        pltpu.VMEM((g, r * ts, 1), F32), pltpu.VMEM((g, r * ts, hd), F32)])
    return pl.pallas_call(
        body, grid_spec=grid_spec, out_shape=SDS((bs * ts, g * r * hd), F32),
        compiler_params=_cparams(("parallel", "arbitrary")), name="nsa_sample",
    )(page_table, hab, hab, hab, hab, hab, hab, kc, vc, win,
      jnp.repeat(slopes, ts, axis=1).reshape(g, r * ts, 1), *([pool] * pp))


def _cmp_s_body(pt_ref, pek_ref, w1k_ref, w2k_ref, pev_ref, w1v_ref, w2v_ref, *rest, pp, page, g):
    pages = rest[:pp]
    kc_ref, vc_ref, stage_ref, x_ref = rest[pp:]
    del pt_ref
    hd = HEAD_DIM
    rows = pp * (page // CMP_BLOCK)
    for cg in range(2 * g):
        for i in range(pp):
            stage_ref[cg, i * page:(i + 1) * page, :] = pages[i][0, pl.ds(cg, page, stride=4 * g), :]
    for comp, (pe_ref, w1_ref, w2_ref, out_ref) in enumerate(
            ((pek_ref, w1k_ref, w2k_ref, kc_ref), (pev_ref, w1v_ref, w2v_ref, vc_ref))):
        for l in range(CMP_BLOCK):
            x = jnp.concatenate([stage_ref[comp * g + gg, pl.ds(l, rows, stride=CMP_BLOCK), :]
                                 for gg in range(g)], axis=0)
            x_ref[:, l * hd:(l + 1) * hd] = (x + pe_ref[l:l + 1, :]).astype(BF16)
        out = _dot(_gelu(_dot(x_ref[...], w1_ref[...])).astype(BF16), w2_ref[...])
        for gg in range(g):
            out_ref[gg] = out[gg * rows:(gg + 1) * rows]


def compress_sample(pool, page_table, p, *, page, g):
    hd = HEAD_DIM
    bs, n_pages = page_table.shape
    pp = _pick(n_pages, (16, 8, 4))
    rows = pp * (page // CMP_BLOCK)
    nb = n_pages * (page // CMP_BLOCK)
    hid = p["w1_k"].shape[1]
    body = functools.partial(_cmp_s_body, pp=pp, page=page, g=g)
    once = pl.Buffered(1)

    def wspecs():
        return [pl.BlockSpec((CMP_BLOCK, hd), lambda b, c, pt: (0, 0)),
                pl.BlockSpec((CMP_BLOCK * hd, hid), lambda b, c, pt: (0, 0), pipeline_mode=once),
                pl.BlockSpec((hid, hd), lambda b, c, pt: (0, 0))]

    def page_spec(i):
        return pl.BlockSpec((1, page * 4 * g, hd), lambda b, c, pt: (pt[b, c * pp + i], 0, 0))

    grid_spec = pltpu.PrefetchScalarGridSpec(
        num_scalar_prefetch=1, grid=(bs, n_pages // pp),
        in_specs=wspecs() + wspecs() + [page_spec(i) for i in range(pp)],
        out_specs=[pl.BlockSpec((None, g, rows, hd), lambda b, c, pt: (b, 0, c, 0)),
                   pl.BlockSpec((None, g, rows, hd), lambda b, c, pt: (b, 0, c, 0))],
        scratch_shapes=[pltpu.VMEM((2 * g, pp * page, hd), F32), pltpu.VMEM((g * rows, CMP_BLOCK * hd), BF16)])
    return pl.pallas_call(
        body, grid_spec=grid_spec, out_shape=[SDS((bs, g, nb, hd), F32), SDS((bs, g, nb, hd), F32)],
        compiler_params=_cparams(("parallel", "arbitrary")), name="nsa_compress_sample",
    )(page_table, p["pe_k"], p["w1_k"].astype(BF16), p["w2_k"].astype(BF16),
      p["pe_v"], p["w1_v"].astype(BF16), p["w2_v"].astype(BF16), *([pool] * pp))


def _ret_body(q_ref, k_ref, v_ref, g_ref, gn_ref, s0_ref, dm_ref, cr_ref, kd_ref, cd_ref, o_ref, s_ref,
              st_ref, *, scale, n_chunks, nh):
    c = pl.program_id(1)
    dk = HEAD_DIM

    @pl.when(c == 0)
    def _():
        st_ref[...] = s0_ref[...]

    for h in range(nh):
        cs = slice(h * dk, (h + 1) * dk)
        qb = q_ref[:, cs].astype(BF16)
        ks = k_ref[:, cs] * scale
        kb = ks.astype(BF16)
        vb = v_ref[:, cs].astype(BF16)
        st = st_ref[h]
        att = _dot_nt(qb, kb) * dm_ref[h]
        o = _dot(att.astype(BF16), vb) + _dot(qb, st.astype(BF16)) * cr_ref[h]
        st_ref[h] = st * cd_ref[h] + _dot_tn((ks * kd_ref[h]).astype(BF16), vb)
        mu = jnp.mean(o, axis=-1, keepdims=True)
        var = jnp.mean(jnp.square(o - mu), axis=-1, keepdims=True)
        on = (o - mu) * lax.rsqrt(var + GN_EPS) * gn_ref[:, cs]
        gate = g_ref[:, cs]
        o_ref[:, cs] = ((gate * (1.0 / (1.0 + jnp.exp(-gate)))) * on).astype(o_ref.dtype)

    @pl.when(c == n_chunks - 1)
    def _():
        s_ref[...] = st_ref[...]


def retention(hab, gn_w, s0, cols, *, row0, b, t, nh):
    dk = HEAD_DIM
    w = nh * dk
    ch = RET_CHUNK if t % RET_CHUNK == 0 else t
    n_chunks = t // ch
    lg = jnp.log1p(-jnp.exp2(-5.0 - jnp.arange(nh, dtype=F32)))
    i = jnp.arange(ch, dtype=F32)
    diff = i[:, None] - i[None, :]
    dmask = jnp.where(diff >= 0, jnp.exp(jnp.maximum(diff, 0.0)[None] * lg[:, None, None]), 0.0)
    cross = jnp.exp((i + 1.0)[None, :] * lg[:, None]).reshape(nh, ch, 1)
    kdec = jnp.exp((ch - 1.0 - i)[None, :] * lg[:, None]).reshape(nh, ch, 1)
    cdec = jnp.exp(ch * lg).reshape(nh, 1, 1)
    rb = row0 // ch
    body = functools.partial(_ret_body, scale=dk ** -0.5, n_chunks=n_chunks, nh=nh)

    def col_spec(col0):
        assert col0 % w == 0
        return pl.BlockSpec((ch, w), lambda bb, c: (rb + bb * n_chunks + c, col0 // w))

    def whole(shape):
        return pl.BlockSpec(shape, lambda bb, c: (0,) * len(shape))

    return pl.pallas_call(
        body,
        grid=(b, n_chunks),
        in_specs=[col_spec(cols["rq"]), col_spec(cols["rk"]), col_spec(cols["rv"]), col_spec(cols["rg"]),
                  whole((1, w)),
                  pl.BlockSpec((None, nh, dk, dk), lambda bb, c: (bb, 0, 0, 0)),
                  whole((nh, ch, ch)), whole((nh, ch, 1)), whole((nh, ch, 1)), whole((nh, 1, 1))],
        out_specs=[pl.BlockSpec((ch, w), lambda bb, c: (bb * n_chunks + c, 0)),
                   pl.BlockSpec((None, nh, dk, dk), lambda bb, c: (bb, 0, 0, 0))],
        out_shape=[SDS((b * t, w), BF16 if ch % 16 == 0 else F32), SDS((b, nh, dk, dk), F32)],
        scratch_shapes=[pltpu.VMEM((nh, dk, dk), F32)],
        compiler_params=_cparams(("parallel", "arbitrary")),
        name="retention",
    )(hab, hab, hab, hab, gn_w.reshape(1, w), s0, dmask, cross, kdec, cdec)


def _logf_body(x_ref, b_ref, o_ref):
    x = x_ref[...] + b_ref[...]
    o_ref[...] = -(jnp.maximum(-x, 0.0) + jnp.log1p(jnp.exp(-jnp.abs(x))))


def fox_logf(hc, b_pad, col0):
    n = hc.shape[0]
    tm = _pick(n, (768, 512, 256, 128, 64, 32, 16, 8))
    return pl.pallas_call(
        _logf_body,
        grid=(n // tm,),
        in_specs=[pl.BlockSpec((tm, LANES), lambda i: (i, col0 // LANES)),
                  pl.BlockSpec((1, LANES), lambda i: (0, 0))],
        out_specs=pl.BlockSpec((tm, LANES), lambda i: (i, 0)),
        out_shape=SDS((n, LANES), F32),
        compiler_params=_cparams(("parallel",)),
        name="fox_logf",
    )(hc, b_pad)


def _cumsum_rows_body(x_ref, o_ref, hi_ref, mid_ref, lo_ref, carry_ref):
    @pl.when(pl.program_id(1) == 0)
    def _():
        carry_ref[...] = jnp.zeros_like(carry_ref)

    tc = x_ref.shape[0]
    tri = jnp.where(lax.broadcasted_iota(I32, (tc, tc), 1) <= lax.broadcasted_iota(I32, (tc, tc), 0),
                    1.0, 0.0).astype(BF16)
    hi, mid, lo = _split3(x_ref[...])
    f = (_dot(tri, hi) + _dot(tri, mid)) + _dot(tri, lo) + carry_ref[...]
    o_ref[...] = f
    hi_ref[...], mid_ref[...], lo_ref[...] = _split3(f)
    carry_ref[...] = f[tc - 1:tc, :]


def cumsum_rows(x, *, b, t):
    tc = 128 if t % 128 == 0 else t
    nc = t // tc
    spec = pl.BlockSpec((tc, LANES), lambda bb, c: (bb * nc + c, 0))
    return pl.pallas_call(
        _cumsum_rows_body,
        grid=(b, nc),
        in_specs=[spec],
        out_specs=[spec, spec, spec, spec],
        out_shape=[SDS((b * t, LANES), F32)] + [SDS((b * t, LANES), BF16)] * 3,
        scratch_shapes=[pltpu.VMEM((1, LANES), F32)],
        compiler_params=_cparams(("parallel", "arbitrary")),
        name="fox_cumsum",
    )(x)


def _fox_p_body(q_ref, k_ref, v_ref, fq_ref, fa_ref, bs_ref, o_ref, m_ref, l_ref, acc_ref, *, tq, tk, scale, r):
    qi = pl.program_id(2)
    hd = HEAD_DIM
    rows = r * tq
    qs = (_stack_heads(q_ref[...], r, hd) * scale).astype(BF16)
    fq = jnp.concatenate([fq_ref[:, i:i + 1] for i in range(r)], axis=0)
    hi, mid, lo = _split3(fq)
    lane = lax.broadcasted_iota(I32, (rows, LANES), 1)
    qb = (bs_ref[...] + jnp.where(lane == 3 * r, hi.astype(F32), 0.0) + jnp.where(lane == 3 * r + 1, mid.astype(F32), 0.0)
          + jnp.where(lane == 3 * r + 2, lo.astype(F32), 0.0)).astype(BF16)
    qa = jnp.concatenate([qs, qb], axis=1)
    t_row = qi * tq + jnp.concatenate([lax.broadcasted_iota(I32, (1, tq), 1)] * r, axis=1)

    def tile(j, masked):
        start = pl.multiple_of(j * tk, tk)
        ka = jnp.concatenate([k_ref[pl.ds(start, tk), :].astype(BF16), fa_ref[pl.ds(start, tk), :]], axis=1)
        v = v_ref[pl.ds(start, tk), :].astype(BF16)
        s = _dot_nt(ka, qa)
        if masked:
            mask = start + lax.broadcasted_iota(I32, (tk, 1), 0) <= t_row
            s = jnp.where(mask, s, NEG)
        m_prev = m_ref[...]
        m_new = jnp.maximum(m_prev, jnp.max(s, axis=0, keepdims=True))
        alpha = jnp.exp(m_prev - m_new)
        p = jnp.exp(s - m_new)
        if masked:
            p = jnp.where(mask, p, 0.0)
        l_ref[...] = alpha * l_ref[...] + jnp.sum(p, axis=0, keepdims=True)
        acc_ref[...] = alpha * acc_ref[...] + _dot_tn(v, p.astype(BF16))
        m_ref[...] = m_new

    def full_tile(j, carry):
        tile(j, False)
        return carry

    _flash_init(m_ref, l_ref, acc_ref)
    n_full = (qi * tq) // tk
    lax.fori_loop(0, n_full, full_tile, 0)
    for jj in range(tq // tk):
        tile(n_full + jj, True)
    o = acc_ref[...] / jnp.maximum(l_ref[...], 1e-30)
    o_ref[...] = jnp.concatenate([o[:, i * tq:(i + 1) * tq].T for i in range(r)], axis=1).astype(BF16)


def fox_prompt(hc, f, f_split, cols, *, b, t, g, r):
    hd = HEAD_DIM
    tq = _pick(t, (512, 256, 128, t))
    tk = _pick(tq, (512, 256, 128, tq))
    nq = t // tq
    qw = r * hd
    fq = f.transpose(0, 2, 1, 3)
    fa = jnp.concatenate([piece.transpose(0, 2, 1, 3) for piece in f_split]
                         + [jnp.ones((b, g, t, 3), BF16), jnp.zeros((b, g, t, LANES - 3 * r - 3), BF16)],
                         axis=-1)
    rows = np.arange(r * tq)[:, None] // tq
    lanes = np.arange(LANES)[None, :]
    bsel = jnp.asarray(np.where((lanes < 3 * r) & (lanes % r == rows), -1.0, 0.0), F32)
    body = functools.partial(_fox_p_body, tq=tq, tk=tk, scale=hd ** -0.5, r=r)
    return pl.pallas_call(
        body,
        grid=(b, g, nq),
        in_specs=[pl.BlockSpec((tq, qw), lambda bb, gg, qi: (bb * nq + qi, gg)),
                  pl.BlockSpec((t, hd), lambda bb, gg, qi: (bb, cols["k"] // hd + gg)),
                  pl.BlockSpec((t, hd), lambda bb, gg, qi: (bb, cols["v"] // hd + gg)),
                  pl.BlockSpec((None, None, tq, r), lambda bb, gg, qi: (bb, gg, qi, 0)),
                  pl.BlockSpec((None, None, t, LANES), lambda bb, gg, qi: (bb, gg, 0, 0)),
                  pl.BlockSpec((r * tq, LANES), lambda bb, gg, qi: (0, 0))],
        out_specs=pl.BlockSpec((tq, qw), lambda bb, gg, qi: (bb * nq + qi, gg)),
        out_shape=SDS((b * t, g * qw), BF16),
        scratch_shapes=[pltpu.VMEM((1, r * tq), F32), pltpu.VMEM((1, r * tq), F32),
                        pltpu.VMEM((hd, r * tq), F32)],
        compiler_params=_cparams(("parallel", "parallel", "arbitrary")),
        name="fox_prompt",
    )(hc, hc, hc, fq, fa, bsel)


def _fox_f_body(pt_ref, new_ref, *rest, pp, n_chunks):
    pages = rest[:pp]
    fk_ref, fn_ref, carry_ref = rest[pp:]
    del pt_ref
    c = pl.program_id(1)
    page = pages[0].shape[2]

    @pl.when(c == 0)
    def _():
        carry_ref[...] = jnp.zeros_like(carry_ref)

    ut = jnp.where(lax.broadcasted_iota(I32, (page, page), 0) <= lax.broadcasted_iota(I32, (page, page), 1),
                   1.0, 0.0).astype(BF16)

    def csum(x, carry):
        hi, mid, lo = _split3(x)
        return (_dot(hi, ut) + _dot(mid, ut)) + _dot(lo, ut) + carry

    carry = carry_ref[...]
    for i in range(pp):
        f = csum(pages[i][0], carry)
        fk_ref[:, i * page:(i + 1) * page] = f
        carry = f[:, page - 1:page]
    carry_ref[...] = carry

    @pl.when(c == n_chunks - 1)
    def _():
        fn_ref[...] = csum(new_ref[...], carry)


def fox_f_sample(logf_pool_t, new_t, page_table):
    bs, n_pages = page_table.shape
    _, nh, page = logf_pool_t.shape
    pp = _pick(n_pages, (64, 32, 16, 8, 4, 2, 1))
    n_chunks = n_pages // pp
    body = functools.partial(_fox_f_body, pp=pp, n_chunks=n_chunks)
    in_specs = [pl.BlockSpec((None, nh, page), lambda b, c, pt: (b, 0, 0))]
    in_specs += [pl.BlockSpec((1, nh, page), functools.partial(lambda b, c, pt, i: (pt[b, c * pp + i], 0, 0), i=i))
                 for i in range(pp)]
    grid_spec = pltpu.PrefetchScalarGridSpec(
        num_scalar_prefetch=1, grid=(bs, n_chunks), in_specs=in_specs,
        out_specs=[pl.BlockSpec((None, nh, pp * page), lambda b, c, pt: (b, 0, c)),
                   pl.BlockSpec((None, nh, page), lambda b, c, pt: (b, 0, 0))],
        scratch_shapes=[pltpu.VMEM((nh, 1), F32)])
    return pl.pallas_call(
        body, grid_spec=grid_spec,
        out_shape=[SDS((bs, nh, n_pages * page), F32), SDS((bs, nh, page), F32)],
        compiler_params=_cparams(("parallel", "arbitrary")), name="fox_f_sample",
    )(page_table, new_t, *([logf_pool_t] * pp))


def _fox_s_body(pt_ref, q_ref, kn_ref, vn_ref, fq_ref, fk_ref, fn_ref, *rest, pp, n_chunks, ts, page, offset,
                scale, g, r):
    pages = rest[:pp]
    o_ref = rest[pp]
    qs_ref, m_ref, l_ref, acc_ref = rest[pp + 1:]
    del pt_ref
    c = pl.program_id(1)
    hd = HEAD_DIM

    @pl.when(c == 0)
    def _():
        for gg in range(g):
            qs_ref[gg] = (_stack_heads(q_ref[:, gg * r * hd:(gg + 1) * r * hd], r, hd) * scale).astype(BF16)
            _flash_init(m_ref.at[gg], l_ref.at[gg], acc_ref.at[gg])

    def fk_rows(f, gg):
        return jnp.concatenate([jnp.broadcast_to(f[gg * r + i:gg * r + i + 1, :], (ts, f.shape[1]))
                                for i in range(r)], axis=0)

    fk = fk_ref[...]
    for gg in range(g):
        ks = [pages[i][0, pl.ds(gg, page, stride=2 * g), :].astype(BF16) for i in range(pp)]
        vs = [pages[i][0, pl.ds(g + gg, page, stride=2 * g), :].astype(BF16) for i in range(pp)]
        s = jnp.concatenate([_dot_nt(qs_ref[gg], k) for k in ks], axis=1) + fq_ref[gg] - fk_rows(fk, gg)
        _flash_update(s, None, vs, m_ref.at[gg], l_ref.at[gg], acc_ref.at[gg])

    @pl.when(c == n_chunks - 1)
    def _():
        t = _rep_rows(offset + lax.broadcasted_iota(I32, (ts, 1), 0), r)
        lane = lax.broadcasted_iota(I32, (1, page), 1)
        outs = []
        for gg in range(g):
            k = _pad_rows(kn_ref[:, gg * hd:(gg + 1) * hd], page).astype(BF16)
            v = _pad_rows(vn_ref[:, gg * hd:(gg + 1) * hd], page).astype(BF16)
            s = _dot_nt(qs_ref[gg], k) + fq_ref[gg] - fk_rows(fn_ref[...], gg)
            _flash_update(s, (offset + lane <= t) & (lane < ts), [v], m_ref.at[gg], l_ref.at[gg], acc_ref.at[gg])
            outs.append(_unstack_heads(_flash_final(l_ref.at[gg], acc_ref.at[gg]), r, ts))
        o_ref[...] = jnp.concatenate(outs, axis=1)


def fox_sample(hc, fq, fk, fn, pool, page_table, cols, *, n_p, bs, ts, page, g, r):
    hd = HEAD_DIM
    n_pages = page_table.shape[1]
    nh = g * r
    offset = n_pages * page
    pp = _pick(n_pages, (32, 16, 8, 4, 2, 1))
    n_chunks = n_pages // pp
    rb = n_p // ts
    body = functools.partial(_fox_s_body, pp=pp, n_chunks=n_chunks, ts=ts, page=page, offset=offset,
                             scale=hd ** -0.5, g=g, r=r)

    def row_spec(width, col0):
        return pl.BlockSpec((ts, width), lambda b, c, pt: (rb + b, col0 // width))

    def page_spec(i):
        return pl.BlockSpec((1, page * 2 * g, hd), lambda b, c, pt: (pt[b, c * pp + i], 0, 0))

    in_specs = [row_spec(nh * hd, cols["q"]), row_spec(g * hd, cols["k"]), row_spec(g * hd, cols["v"]),
                pl.BlockSpec((None, g, r * ts, 1), lambda b, c, pt: (b, 0, 0, 0)),
                pl.BlockSpec((None, nh, pp * page), lambda b, c, pt: (b, 0, c)),
                pl.BlockSpec((None, nh, page), lambda b, c, pt: (b, 0, 0))]
    in_specs += [page_spec(i) for i in range(pp)]
    grid_spec = pltpu.PrefetchScalarGridSpec(
        num_scalar_prefetch=1, grid=(bs, n_chunks), in_specs=in_specs,
        out_specs=pl.BlockSpec((ts, nh * hd), lambda b, c, pt: (b, 0)),
        scratch_shapes=[pltpu.VMEM((g, r * ts, hd), BF16), pltpu.VMEM((g, r * ts, 1), F32),
                        pltpu.VMEM((g, r * ts, 1), F32), pltpu.VMEM((g, r * ts, hd), F32)])
    return pl.pallas_call(
        body, grid_spec=grid_spec, out_shape=SDS((bs * ts, nh * hd), F32),
        compiler_params=_cparams(("parallel", "arbitrary")), name="fox_sample",
    )(page_table, hc, hc, hc, fq, fk, fn, *([pool] * pp))


def _ab_layout(nh_a, g, r, nh_r):
    hd = HEAD_DIM
    qa, kv = nh_a * hd, g * hd
    sizes = [("q", qa), ("kc", kv), ("vc", kv), ("ks", kv), ("vs", kv), ("kw", kv), ("vw", kv),
             ("gt_src", nh_a * 3), ("rq", nh_r * hd), ("rk", nh_r * hd), ("rv", nh_r * hd), ("rg", nh_r * hd)]
    src = {}
    pos = 0
    for name, w in sizes:
        src[name] = pos
        pos += w
    order = ["q", "rq", "rk", "rv", "rg", "kc", "vc", "ks", "vs", "kw", "vw"]
    widths = dict(sizes)
    idx = []
    cols = {}
    for name in order:
        cols[name] = len(idx)
        idx += list(range(src[name], src[name] + widths[name]))
    cols["gt"] = len(idx)
    for gg in range(g):
        blk = [-1] * LANES
        for c in range(3):
            for i in range(r):
                blk[c * r + i] = src["gt_src"] + (gg * r + i) * 3 + c
        idx += blk
    return np.array(idx, np.int32), cols


def _gather_cols(w, idx):
    parts = []
    i, n = 0, len(idx)
    while i < n:
        j = i + 1
        if idx[i] < 0:
            while j < n and idx[j] < 0:
                j += 1
            parts.append(jnp.zeros((w.shape[0], j - i), w.dtype))
        else:
            while j < n and idx[j] == idx[j - 1] + 1:
                j += 1
            parts.append(w[:, int(idx[i]):int(idx[i]) + (j - i)])
        i = j
    return jnp.concatenate(parts, axis=1)


def _even_layer(x, n_p, b, t, bs, ts, p, cache_kv, cache_win, state, page_table):
    hd = HEAD_DIM
    d = x.shape[1]
    nh_a = d // (2 * hd)
    g = nh_a // 4
    r = nh_a // g
    nh_r = d // (2 * hd)
    idx, cols = _ab_layout(nh_a, g, r, nh_r)
    tn = 768
    ncol = -(-len(idx) // tn) * tn
    idx = np.concatenate([idx, np.full(ncol - len(idx), -1, np.int32)])
    w_in = _gather_cols(p["w_in"], idx).astype(BF16)
    hab, _ = mm_norm(x, p["norm"], w_in, tn)

    kv4 = 4 * g * hd
    c_rows = cols["kc"]
    c_win = cols["kw"]
    new_rows_p = hab[:n_p, c_rows:c_rows + kv4].reshape(b, t, 4, g, hd)
    new_rows_s = hab[n_p:, c_rows:c_rows + kv4].reshape(bs, ts, 4, g, hd)
    new_win_p = hab[:n_p, c_win:c_win + 2 * g * hd].reshape(b, t, 2, g, hd)
    new_win_s = hab[n_p:, c_win:c_win + 2 * g * hd].reshape(bs, ts, 2, g, hd)
    win_state_p = new_win_p[:, -min(WINDOW, t):]
    wb = cache_win.shape[1]
    win_state_s = jnp.concatenate([cache_win, new_win_s], axis=1)[:, -wb:]

    slopes = jnp.exp2(-8.0 * (jnp.arange(nh_a, dtype=F32) + 1.0) / nh_a).reshape(g, r)
    w1k, w2k = p["w1_k"].astype(BF16), p["w2_k"].astype(BF16)
    w1v, w2v = p["w1_v"].astype(BF16), p["w2_v"].astype(BF16)

    def cmp_pair(blocks):
        bb, nb = blocks.shape[:2]
        flat = blocks.transpose(3, 0, 1, 4, 2, 5).reshape(2, bb * nb * g, CMP_BLOCK * hd)
        kc = compress(flat[0], p["pe_k"].reshape(-1), w1k, w2k).reshape(bb, nb, g, hd).transpose(0, 2, 1, 3)
        vc = compress(flat[1], p["pe_v"].reshape(-1), w1v, w2v).reshape(bb, nb, g, hd).transpose(0, 2, 1, 3)
        return kc, vc

    nb_p = t // CMP_BLOCK
    kc_p, vc_p = cmp_pair(new_rows_p[:, :nb_p * CMP_BLOCK, 0:2].reshape(b, nb_p, CMP_BLOCK, 2, g, hd))
    o_a_p = nsa_prompt(hab, kc_p, vc_p, slopes, cols, b=b, t=t, g=g, r=r)

    page = cache_kv.shape[1]
    n_pages = page_table.shape[1]
    nb_s = (n_pages * page + ts) // CMP_BLOCK
    assert nb_s * CMP_BLOCK == n_pages * page and page % CMP_BLOCK == 0
    pool = cache_kv.reshape(cache_kv.shape[0], page * 4 * g, hd)
    kc_s, vc_s = compress_sample(pool, page_table, p, page=page, g=g)
    o_a_s = nsa_sample(hab, kc_s, vc_s, pool, cache_win.reshape(bs, wb * 2 * g, hd), page_table, slopes, cols,
                       n_p=n_p, bs=bs, ts=ts, page=page, g=g, r=r)

    o_b_p, st_p = retention(hab, p["gn"], jnp.zeros((b, nh_r, hd, hd), F32), cols, row0=0, b=b, t=t, nh=nh_r)
    o_b_s, st_s = retention(hab, p["gn"], state, cols, row0=n_p, b=bs, t=ts, nh=nh_r)

    o = jnp.concatenate([jnp.concatenate([o_a_p, o_b_p.astype(BF16)], axis=1),
                         jnp.concatenate([o_a_s.astype(BF16), o_b_s.astype(BF16)], axis=1)], axis=0)
    x = mm_res(o, p["w_out"].astype(BF16), x)
    return x, (new_rows_p, win_state_p, st_p, new_rows_s, win_state_s, st_s)


def _odd_layer(x, n_p, b, t, bs, ts, p, cache_kv, cache_logf, page_table):
    hd = HEAD_DIM
    d = x.shape[1]
    nh = d // hd
    g = nh // 4
    r = nh // g
    cols = {"q": 0, "k": nh * hd, "v": (nh + g) * hd, "f": (nh + 2 * g) * hd}
    ncol_src = p["w_in"].shape[1]
    tn = 640
    ncol = -(-(cols["f"] + LANES) // tn) * tn
    w_in = jnp.pad(p["w_in"], ((0, 0), (0, ncol - ncol_src))).astype(BF16)
    hc, _ = mm_norm(x, p["norm"], w_in, tn)

    new_rows_p = hc[:n_p, cols["k"]:cols["f"]].reshape(b, t, 2, g, hd)
    new_rows_s = hc[n_p:, cols["k"]:cols["f"]].reshape(bs, ts, 2, g, hd)
    b_pad = jnp.pad(p["b_f"].astype(F32), (0, LANES - nh)).reshape(1, LANES)
    logf = fox_logf(hc, b_pad, cols["f"])
    new_logf_p = logf[:n_p, :nh].reshape(b, t, nh)
    new_logf_s = logf[n_p:, :nh].reshape(bs, ts, nh)

    f_p, *f_split = [a[:, :nh].reshape(b, t, g, r) for a in cumsum_rows(logf, b=b, t=t)]
    o_p = fox_prompt(hc, f_p, f_split, cols, b=b, t=t, g=g, r=r)

    page = cache_kv.shape[1]
    new_t = jnp.pad(new_logf_s.transpose(0, 2, 1), ((0, 0), (0, 0), (0, page - ts)))
    fk, fn = fox_f_sample(cache_logf.transpose(0, 2, 1), new_t, page_table)
    fq = fn[:, :, :ts].reshape(bs, g, r * ts, 1)
    o_s = fox_sample(hc, fq, fk, fn, cache_kv.reshape(cache_kv.shape[0], page * 2 * g, hd), page_table, cols,
                     n_p=n_p, bs=bs, ts=ts, page=page, g=g, r=r)

    x = mm_res(jnp.concatenate([o_p, o_s.astype(BF16)], axis=0), p["w_out"].astype(BF16), x)
    return x, (new_rows_p, new_logf_p, new_rows_s, new_logf_s)


def _peer_layer(x, nw, wq, k1, k2, u_bf, vt_bf, layer):
    q, xn = mm_norm(x, nw, wq.astype(BF16), _pick(wq.shape[1], (1024, 512, 256, 128)))
    s1t, s2t, st = peer_score(q, k1, k2)
    return peer_dense(xn, u_bf, vt_bf, layer, s1t, s2t, st, x)


def kernel(x_prompt, x_sample, cache_nsa_kv, cache_nsa_win, state_ret, cache_fox_kv, cache_fox_logf,
           page_table, norm_mix, norm_ffn, norm_final, w_in_ab, w_out_ab, cmp_pe_k, cmp_w1_k, cmp_w2_k,
           cmp_pe_v, cmp_w1_v, cmp_w2_v, ret_gn, w_in_c, b_forget, w_out_c, peer_wq, peer_k1, peer_k2,
           peer_u, peer_v):
    b, t, d = x_prompt.shape
    bs, ts, _ = x_sample.shape
    n_p, n_s = b * t, bs * ts
    depth = norm_mix.shape[0]
    x = jnp.concatenate([x_prompt.reshape(n_p, d), x_sample.reshape(n_s, d)], axis=0)
    u_bf = peer_u.astype(BF16)
    vt_bf = jnp.swapaxes(peer_v, 1, 2).astype(BF16)
    even, odd = [], []
    for l in range(depth):
        if l % 2 == 0:
            e = l // 2
            p = dict(norm=norm_mix[l], w_in=w_in_ab[e], w_out=w_out_ab[e], pe_k=cmp_pe_k[e], w1_k=cmp_w1_k[e],
                     w2_k=cmp_w2_k[e], pe_v=cmp_pe_v[e], w1_v=cmp_w1_v[e], w2_v=cmp_w2_v[e], gn=ret_gn[e])
            x, outs = _even_layer(x, n_p, b, t, bs, ts, p, cache_nsa_kv[e], cache_nsa_win[e], state_ret[e],
                                  page_table)
            even.append(outs)
        else:
            o = l // 2
            p = dict(norm=norm_mix[l], w_in=w_in_c[o], b_f=b_forget[o], w_out=w_out_c[o])
            x, outs = _odd_layer(x, n_p, b, t, bs, ts, p, cache_fox_kv[o], cache_fox_logf[o], page_table)
            odd.append(outs)
        x = _peer_layer(x, norm_ffn[l], peer_wq[l], peer_k1[l], peer_k2[l], u_bf, vt_bf, l)
    y = rms_final(x, norm_final)
    y_prompt = y[:n_p].reshape(b, t, d)
    y_sample = y[n_p:].reshape(bs, ts, d)

    def stack(group, i):
        return jnp.stack([o[i] for o in group])

    return (y_prompt, y_sample, stack(even, 0), stack(even, 1), stack(even, 2), stack(odd, 0), stack(odd, 1),
            stack(even, 3), stack(even, 4), stack(even, 5), stack(odd, 2), stack(odd, 3))
```

```python
import functools
import math

import numpy as np
import jax
import jax.numpy as jnp
from jax import lax
from jax.experimental import pallas as pl
from jax.experimental.pallas import tpu as pltpu

F32 = jnp.float32
BF16 = jnp.bfloat16
I32 = jnp.int32

HEAD_DIM = 128
CMP_BLOCK = 64
SEL_BLOCK = 64
SEL_TOPK = 16
WINDOW = 512
CMP_HIDDEN = 256
RET_CHUNK = 128
Q_BLOCK = 128
PEER_HEADS = 8
PEER_DK = 256
PEER_TOPK = 16
EPS = 1e-6
GN_EPS = 1e-5
NEG = -1e30

LANES = 128
SUBLANES = 8
VMEM_LIMIT = 56 * 1024 * 1024

SDS = jax.ShapeDtypeStruct


def _cparams(sem):
    return pltpu.CompilerParams(dimension_semantics=sem, vmem_limit_bytes=VMEM_LIMIT)


def _dot(a, b):
    return jnp.dot(a, b, preferred_element_type=F32)


def _dot_nt(a, b):
    return lax.dot_general(a, b, (((1,), (1,)), ((), ())), preferred_element_type=F32)


def _dot_tn(a, b):
    return lax.dot_general(a, b, (((0,), (0,)), ((), ())), preferred_element_type=F32)


def _pick(n, cands):
    for c in cands:
        if c <= n and n % c == 0:
            return c
    raise ValueError(f"no tile for {n} in {cands}")


def _gelu(x):
    c = math.sqrt(2.0 / math.pi)
    return x * (0.5 * (1.0 + jnp.tanh(c * (x + 0.044715 * (x * x * x)))))


def _split3(x):
    hi = x.astype(BF16)
    r1 = x - hi.astype(F32)
    mid = r1.astype(BF16)
    lo = (r1 - mid.astype(F32)).astype(BF16)
    return hi, mid, lo


def _stack_heads(x, nh, hd):
    return jnp.concatenate([x[:, r * hd:(r + 1) * hd] for r in range(nh)], axis=0)


def _unstack_heads(x, nh, t):
    return jnp.concatenate([x[r * t:(r + 1) * t, :] for r in range(nh)], axis=1)


def _rep_rows(x, k):
    return jnp.concatenate([x] * k, axis=0)


def _mm_norm_body(x_ref, nw_ref, w_ref, o_ref, xn_ref):
    @pl.when(pl.program_id(1) == 0)
    def _():
        x = x_ref[...]
        ms = jnp.mean(x * x, axis=-1, keepdims=True)
        xn_ref[...] = (x * lax.rsqrt(ms + EPS) * nw_ref[...]).astype(BF16)

    o_ref[...] = _dot(xn_ref[...], w_ref[...])


def mm_norm(x, nw, w_bf, tn):
    n, d = x.shape
    nn = w_bf.shape[1]
    tm = _pick(n, (768, 512, 256, 128, 64, 32, 16))
    return pl.pallas_call(
        _mm_norm_body,
        grid=(n // tm, nn // tn),
        in_specs=[pl.BlockSpec((tm, d), lambda i, j: (i, 0)),
                  pl.BlockSpec((1, d), lambda i, j: (0, 0)),
                  pl.BlockSpec((d, tn), lambda i, j: (0, j))],
        out_specs=[pl.BlockSpec((tm, tn), lambda i, j: (i, j)),
                   pl.BlockSpec((tm, d), lambda i, j: (i, 0))],
        out_shape=[SDS((n, nn), F32), SDS((n, d), BF16)],
        compiler_params=_cparams(("parallel", "arbitrary")),
        name="mm_norm",
    )(x, nw.reshape(1, d), w_bf)


def _mm_res_body(a_ref, w_ref, r_ref, o_ref):
    o_ref[...] = r_ref[...] + _dot(a_ref[...], w_ref[...])


def mm_res(a_bf, w_bf, res):
    n, k = a_bf.shape
    nn = w_bf.shape[1]
    tm = _pick(n, (768, 512, 256, 128, 64, 32, 16))
    tn = _pick(nn, (1024, 512, 256, 128))
    return pl.pallas_call(
        _mm_res_body,
        grid=(n // tm, nn // tn),
        in_specs=[pl.BlockSpec((tm, k), lambda i, j: (i, 0)),
                  pl.BlockSpec((k, tn), lambda i, j: (0, j)),
                  pl.BlockSpec((tm, tn), lambda i, j: (i, j))],
        out_specs=pl.BlockSpec((tm, tn), lambda i, j: (i, j)),
        out_shape=SDS((n, nn), F32),
        compiler_params=_cparams(("parallel", "arbitrary")),
        name="mm_res",
    )(a_bf, w_bf, res)


def _rms_body(x_ref, nw_ref, o_ref):
    x = x_ref[...]
    ms = jnp.mean(x * x, axis=-1, keepdims=True)
    o_ref[...] = x * lax.rsqrt(ms + EPS) * nw_ref[...]


def rms_final(x, nw):
    n, d = x.shape
    tm = _pick(n, (768, 512, 256, 128, 64, 32, 16, 8))
    return pl.pallas_call(
        _rms_body,
        grid=(n // tm,),
        in_specs=[pl.BlockSpec((tm, d), lambda i: (i, 0)), pl.BlockSpec((1, d), lambda i: (0, 0))],
        out_specs=pl.BlockSpec((tm, d), lambda i: (i, 0)),
        out_shape=SDS((n, d), F32),
        compiler_params=_cparams(("parallel",)),
        name="rms_final",
    )(x, nw.reshape(1, d))


def _batcher_pairs(n):
    pairs = []
    p = 1
    while p < n:
        k = p
        while k >= 1:
            for j in range(k % p, n - k, 2 * k):
                for i in range(min(k, n - j - k)):
                    if (i + j) // (2 * p) == (i + j + k) // (2 * p):
                        pairs.append((i + j, i + j + k))
            k //= 2
        p *= 2
    return pairs


_SORT16 = _batcher_pairs(16)


def _sort16_desc(xs):
    xs = list(xs)
    for i, j in _SORT16:
        hi = jnp.maximum(xs[i], xs[j])
        lo = jnp.minimum(xs[i], xs[j])
        xs[i], xs[j] = hi, lo
    return xs


def _bitonic16_desc(c):
    c = list(c)
    for stride in (8, 4, 2, 1):
        for i in range(16):
            if i & stride == 0:
                hi = jnp.maximum(c[i], c[i + stride])
                lo = jnp.minimum(c[i], c[i + stride])
                c[i], c[i + stride] = hi, lo
    return c


def _merge16_desc(a, b):
    return _bitonic16_desc([jnp.maximum(a[i], b[15 - i]) for i in range(16)])


def _top16_sorted(s):
    cols = _sort16_desc([s[v * SUBLANES:(v + 1) * SUBLANES, :] for v in range(16)])
    for shift in (4, 2, 1):
        other = [pltpu.roll(x, shift, 0) for x in cols]
        cols = _merge16_desc(cols, other)
    return cols


def _peer_score_body(q_ref, k1_ref, k2_ref, s1_ref, s2_ref, st_ref):
    tm = q_ref.shape[0]
    half = PEER_DK // 2
    k1 = k1_ref[...].astype(BF16)
    k2 = k2_ref[...].astype(BF16)
    sub = lax.broadcasted_iota(I32, (SUBLANES, tm), 0)
    a_top = None
    b_top = None
    for h in range(PEER_HEADS):
        qh = q_ref[:, h * PEER_DK:(h + 1) * PEER_DK]
        s1 = _dot_nt(k1, qh[:, :half].astype(BF16))
        s2 = _dot_nt(k2, qh[:, half:].astype(BF16))
        s1_ref[h] = s1
        s2_ref[h] = s2
        a_h = _top16_sorted(s1)
        b_h = _top16_sorted(s2)
        if h == 0:
            a_top, b_top = a_h, b_h
        else:
            a_top = [jnp.where(sub == h, x, y) for x, y in zip(a_h, a_top)]
            b_top = [jnp.where(sub == h, x, y) for x, y in zip(b_h, b_top)]
    ninf = jnp.full((SUBLANES, tm), -jnp.inf, F32)
    row0 = [a_top[0] + b_top[b] for b in range(16)]
    col0 = [a_top[a] + b_top[0] for a in range(1, 16)] + [ninf]
    mid = ([a_top[1] + b_top[b] for b in range(1, 8)] + [a_top[a] + b_top[1] for a in range(2, 8)]
           + [a_top[2] + b_top[b] for b in range(2, 5)])
    mid = _sort16_desc(mid)
    v0 = a_top[3] + b_top[2]
    v1 = a_top[4] + b_top[2]
    v2 = a_top[3] + b_top[3]
    tail = [v0, jnp.maximum(v1, v2), jnp.minimum(v1, v2)] + [ninf] * 13
    top = _merge16_desc(_merge16_desc(_merge16_desc(row0, col0), mid), tail)
    z = jnp.zeros((SUBLANES, tm), F32)
    for i in range(16):
        z = z + jnp.exp(top[i] - top[0])
    st_ref[0] = top[15]
    st_ref[1] = a_top[0]
    st_ref[2] = b_top[0]
    st_ref[3] = z


def peer_score(q, k1, k2):
    n, d = q.shape
    nk = k1.shape[0]
    assert nk == 128 and d == PEER_HEADS * PEER_DK
    tm = _pick(n, (256, 128))
    return pl.pallas_call(
        _peer_score_body,
        grid=(n // tm,),
        in_specs=[pl.BlockSpec((tm, d), lambda i: (i, 0)),
                  pl.BlockSpec(k1.shape, lambda i: (0, 0)),
                  pl.BlockSpec(k2.shape, lambda i: (0, 0))],
        out_specs=[pl.BlockSpec((PEER_HEADS, nk, tm), lambda i: (0, 0, i)),
                   pl.BlockSpec((PEER_HEADS, nk, tm), lambda i: (0, 0, i)),
                   pl.BlockSpec((4, PEER_HEADS, tm), lambda i: (0, 0, i))],
        out_shape=[SDS((PEER_HEADS, nk, n), F32), SDS((PEER_HEADS, nk, n), F32),
                   SDS((4, PEER_HEADS, n), F32)],
        compiler_params=_cparams(("parallel",)),
        name="peer_score",
    )(q, k1, k2)


def _peer_dense_body(xn_ref, u_ref, vt_ref, s1_ref, s2_ref, st_ref, res_ref, o_ref,
                     acc_ref, e2_ref, h_ref, act_ref, *, nc, n_steps):
    c = pl.program_id(1)
    nk = s2_ref.shape[1]

    @pl.when(c == 0)
    def _():
        acc_ref[...] = jnp.zeros_like(acc_ref)
        for h in range(PEER_HEADS):
            e2_ref[h] = jnp.exp(s2_ref[h] - st_ref[2, h:h + 1, :])

    h_ref[...] = _dot_nt(u_ref[...], xn_ref[...])
    for s in range(nc):
        e1 = c * nc + s
        w = jnp.zeros((nk, xn_ref.shape[0]), F32)
        for h in range(PEER_HEADS):
            s1row = s1_ref[h, pl.ds(e1, 1), :]
            p1 = jnp.exp(s1row - st_ref[1, h:h + 1, :]) / st_ref[3, h:h + 1, :]
            a = s1row + s2_ref[h]
            w = w + jnp.where(a >= st_ref[0, h:h + 1, :], p1 * e2_ref[h], 0.0)
        act_ref[s * nk:(s + 1) * nk, :] = (_gelu(h_ref[s * nk:(s + 1) * nk, :]) * w).astype(BF16)
    acc_ref[...] += _dot(vt_ref[...], act_ref[...])

    @pl.when(c == n_steps - 1)
    def _():
        o_ref[...] = res_ref[...] + acc_ref[...].T


def peer_dense(xn_bf, u_bf, vt_bf, layer, s1t, s2t, st, res):
    n, d = xn_bf.shape
    ne = u_bf.shape[1]
    nk = s1t.shape[1]
    tm = _pick(n, (768, 512, 256, 128))
    nc = 8
    te = nc * nk
    n_steps = ne // te
    body = functools.partial(_peer_dense_body, nc=nc, n_steps=n_steps)
    once = pl.Buffered(1)
    return pl.pallas_call(
        body,
        grid=(n // tm, n_steps),
        in_specs=[pl.BlockSpec((tm, d), lambda i, c: (i, 0), pipeline_mode=once),
                  pl.BlockSpec((None, te, d), lambda i, c: (layer, c, 0)),
                  pl.BlockSpec((None, d, te), lambda i, c: (layer, 0, c)),
                  pl.BlockSpec((PEER_HEADS, nk, tm), lambda i, c: (0, 0, i), pipeline_mode=once),
                  pl.BlockSpec((PEER_HEADS, nk, tm), lambda i, c: (0, 0, i), pipeline_mode=once),
                  pl.BlockSpec((4, PEER_HEADS, tm), lambda i, c: (0, 0, i), pipeline_mode=once),
                  pl.BlockSpec((tm, d), lambda i, c: (i, 0), pipeline_mode=once)],
        out_specs=pl.BlockSpec((tm, d), lambda i, c: (i, 0), pipeline_mode=once),
        out_shape=SDS((n, d), F32),
        scratch_shapes=[pltpu.VMEM((d, tm), F32), pltpu.VMEM((PEER_HEADS, nk, tm), F32),
                        pltpu.VMEM((te, tm), F32), pltpu.VMEM((te, tm), BF16)],
        compiler_params=_cparams(("parallel", "arbitrary")),
        name="peer_dense",
    )(xn_bf, u_bf, vt_bf, s1t, s2t, st, res)


def _flash_init(m_ref, l_ref, acc_ref):
    m_ref[...] = jnp.full(m_ref.shape, NEG, F32)
    l_ref[...] = jnp.zeros(l_ref.shape, F32)
    acc_ref[...] = jnp.zeros(acc_ref.shape, F32)


def _flash_update(s, mask, v_tiles, m_ref, l_ref, acc_ref):
    if mask is not None:
        s = jnp.where(mask, s, NEG)
    m_prev = m_ref[...]
    m_new = jnp.maximum(m_prev, jnp.max(s, axis=-1, keepdims=True))
    alpha = jnp.exp(m_prev - m_new)
    p = jnp.exp(s - m_new)
    if mask is not None:
        p = jnp.where(mask, p, 0.0)
    l_ref[...] = alpha * l_ref[...] + jnp.sum(p, axis=-1, keepdims=True)
    pb = p.astype(BF16)
    pv = None
    off = 0
    for v in v_tiles:
        part = _dot(pb[:, off:off + v.shape[0]], v)
        pv = part if pv is None else pv + part
        off += v.shape[0]
    acc_ref[...] = alpha * acc_ref[...] + pv
    m_ref[...] = m_new


def _flash_final(l_ref, acc_ref):
    return acc_ref[...] / jnp.maximum(l_ref[...], 1e-30)


def _softmax_masked(s, mask):
    s = jnp.where(mask, s, NEG)
    m = jnp.max(s, axis=-1, keepdims=True)
    e = jnp.where(mask, jnp.exp(s - m), 0.0)
    return e / jnp.maximum(jnp.sum(e, axis=-1, keepdims=True), 1e-30)


def _topk_mask(score, k):
    n = score.shape[-1]
    lane = lax.broadcasted_iota(I32, score.shape, 1).astype(F32)
    sel = jnp.zeros(score.shape, F32)
    for _ in range(k):
        m = jnp.max(score, axis=-1, keepdims=True)
        idx = jnp.min(jnp.where(score == m, lane, float(n)), axis=-1, keepdims=True)
        hit = lane == idx
        sel = jnp.where(hit, jnp.where(m >= 0.0, 1.0, 0.0), sel)
        score = jnp.where(hit, -jnp.inf, score)
    return sel


def _compress_body(x_ref, pe_ref, w1_ref, w2_ref, o_ref):
    xb = (x_ref[...] + pe_ref[...]).astype(BF16)
    h = _gelu(_dot(xb, w1_ref[...]))
    o_ref[...] = _dot(h.astype(BF16), w2_ref[...])


def compress(x, pe, w1_bf, w2_bf):
    rows, k = x.shape
    tr = _pick(rows, (256, 128, 64, 32, 16, 8))
    hid = w1_bf.shape[1]
    hd = w2_bf.shape[1]
    return pl.pallas_call(
        _compress_body,
        grid=(rows // tr,),
        in_specs=[pl.BlockSpec((tr, k), lambda i: (i, 0)),
                  pl.BlockSpec((1, k), lambda i: (0, 0)),
                  pl.BlockSpec((k, hid), lambda i: (0, 0)),
                  pl.BlockSpec((hid, hd), lambda i: (0, 0))],
        out_specs=pl.BlockSpec((tr, hd), lambda i: (i, 0)),
        out_shape=SDS((rows, hd), F32),
        compiler_params=_cparams(("parallel",)),
        name="nsa_compress",
    )(x, pe.reshape(1, k), w1_bf, w2_bf)


def _nsa_cmp_and_select(qs, slope, t1, kc_ref, vc_ref, *, r, nb, n_sel, ns_pad, k_top):
    tq = t1.shape[0]
    t = _rep_rows(t1, r)
    blk_end = lax.broadcasted_iota(I32, (1, nb), 1) * CMP_BLOCK + (CMP_BLOCK - 1)
    d_c = t - blk_end
    s = _dot_nt(qs, kc_ref[...].astype(BF16)) - slope * d_c.astype(F32)
    p_c = _softmax_masked(s, d_c >= 0)
    o_c = _dot(p_c.astype(BF16), vc_ref[...].astype(BF16))
    imp = p_c[0:tq]
    for i in range(1, r):
        imp = imp + p_c[i * tq:(i + 1) * tq]
    if ns_pad > nb:
        imp = jnp.concatenate([imp, jnp.zeros((tq, ns_pad - nb), F32)], axis=1)
    jsel = lax.broadcasted_iota(I32, (1, ns_pad), 1)
    cur = jnp.right_shift(t1, 6)
    forced = (jsel == 0) | (jsel == cur) | (jsel == cur - 1)
    score = jnp.where(forced, r + 1.0, jnp.where(jsel <= cur, imp, -1.0))
    score = jnp.where(jsel < n_sel, score, -2.0)
    return o_c, _topk_mask(score, k_top)


def _gate_cols(gsig, c, r):
    return jnp.concatenate([gsig[:, c * r + i:c * r + i + 1] for i in range(r)], axis=0)


def _flash_update_t(s, mask, v, m_ref, l_ref, acc_ref):
    if mask is not None:
        s = jnp.where(mask, s, NEG)
    m_prev = m_ref[...]
    m_new = jnp.maximum(m_prev, jnp.max(s, axis=0, keepdims=True))
    alpha = jnp.exp(m_prev - m_new)
    p = jnp.exp(s - m_new)
    if mask is not None:
        p = jnp.where(mask, p, 0.0)
    l_ref[...] = alpha * l_ref[...] + jnp.sum(p, axis=0, keepdims=True)
    acc_ref[...] = alpha * acc_ref[...] + _dot_tn(v, p.astype(BF16))
    m_ref[...] = m_new


def _nsa_p_body(q_ref, gt_ref, kc_ref, vc_ref, ks_ref, vs_ref, kw_ref, vw_ref, sl_ref, kp_ref, o_ref,
                m_ref, l_ref, acc_ref, *, tq, tk, wk, nb, n_sel, k_top, scale, r):
    qi = pl.program_id(2)
    a = qi * tq
    rows = r * tq
    qs = (_stack_heads(q_ref[...], r, HEAD_DIM) * scale).astype(BF16)
    slope = sl_ref[...]
    t1 = a + lax.broadcasted_iota(I32, (tq, 1), 0)
    t = _rep_rows(t1, r)
    o_c, sel = _nsa_cmp_and_select(qs, slope, t1, kc_ref, vc_ref, r=r, nb=nb, n_sel=n_sel,
                                   ns_pad=n_sel, k_top=k_top)
    sel_bf = sel.astype(BF16)
    lane = lax.broadcasted_iota(I32, (rows, LANES), 1)
    t_hi = jnp.right_shift(t, 6).astype(F32)
    t_lo = jnp.bitwise_and(t, SEL_BLOCK - 1).astype(F32)
    qb = jnp.where(lane == 0, slope * 64.0,
                   jnp.where(lane == 1, slope,
                             jnp.where(lane == 2, -(slope * 64.0) * t_hi,
                                       jnp.where(lane == 3, -slope * t_lo, 0.0)))).astype(BF16)
    qa = jnp.concatenate([qs, qb], axis=1)
    t_row = a + jnp.concatenate([lax.broadcasted_iota(I32, (1, tq), 1)] * r, axis=1)

    def attend(k_ref_, v_ref_, start, width, window):
        ka = jnp.concatenate([k_ref_[pl.ds(start, width), :].astype(BF16), kp_ref[pl.ds(start, width), :]], axis=1)
        v = v_ref_[pl.ds(start, width), :].astype(BF16)
        s = _dot_nt(ka, qa)
        pos = start + lax.broadcasted_iota(I32, (width, 1), 0)
        mask = pos <= t_row
        if window:
            mask = mask & (pos >= t_row - WINDOW)
        else:
            blk = lax.broadcasted_iota(I32, (width, n_sel), 1)
            key_blk = jnp.right_shift(start + lax.broadcasted_iota(I32, (width, n_sel), 0), 6)
            bm = _dot_nt(jnp.where(blk == key_blk, 1.0, 0.0).astype(BF16), sel_bf)
            mask = mask & (jnp.concatenate([bm] * r, axis=1) > 0.5)
        _flash_update_t(s, mask, v, m_ref, l_ref, acc_ref)

    def sel_step(j, carry):
        attend(ks_ref, vs_ref, pl.multiple_of(j * tk, tk), tk, False)
        return carry

    _flash_init(m_ref, l_ref, acc_ref)
    lax.fori_loop(0, (a + tq - 1) // tk + 1, sel_step, 0)
    o_s = (acc_ref[...] / jnp.maximum(l_ref[...], 1e-30)).T

    _flash_init(m_ref, l_ref, acc_ref)
    attend(kw_ref, vw_ref, pl.multiple_of(jnp.maximum(a + tq - wk, 0), SUBLANES), wk, True)
    o_w = (acc_ref[...] / jnp.maximum(l_ref[...], 1e-30)).T

    gsig = 1.0 / (1.0 + jnp.exp(-gt_ref[...]))
    o = _gate_cols(gsig, 0, r) * o_c + _gate_cols(gsig, 1, r) * o_s + _gate_cols(gsig, 2, r) * o_w
    o_ref[...] = _unstack_heads(o, r, tq).astype(BF16)


def nsa_prompt(hab, kc, vc, slopes, cols, *, b, t, g, r):
    hd = HEAD_DIM
    nb = kc.shape[2]
    n_sel = -(-t // SEL_BLOCK)
    assert n_sel == nb and t % SEL_BLOCK == 0
    tq = _pick(t, (4 * Q_BLOCK, 2 * Q_BLOCK, Q_BLOCK, t))
    tk = _pick(t, (512, 256, 128, t))
    wk = min(WINDOW + tq, t)
    nq = t // tq
    qw = r * hd
    body = functools.partial(_nsa_p_body, tq=tq, tk=tk, wk=wk, nb=nb, n_sel=n_sel, k_top=min(SEL_TOPK, n_sel),
                             scale=hd ** -0.5, r=r)
    assert t <= 256 * SEL_BLOCK
    pos = np.arange(t)
    kp = np.zeros((t, LANES), np.float32)
    kp[:, 0], kp[:, 1], kp[:, 2], kp[:, 3] = pos >> 6, pos & (SEL_BLOCK - 1), 1.0, 1.0
    kpos = jnp.asarray(kp, BF16)

    def seq_spec(col0):
        return pl.BlockSpec((t, hd), lambda bb, gg, qi: (bb, col0 // hd + gg))

    return pl.pallas_call(
        body,
        grid=(b, g, nq),
        in_specs=[pl.BlockSpec((tq, qw), lambda bb, gg, qi: (bb * nq + qi, gg)),
                  pl.BlockSpec((tq, LANES), lambda bb, gg, qi: (bb * nq + qi, cols["gt"] // LANES + gg)),
                  pl.BlockSpec((None, None, nb, hd), lambda bb, gg, qi: (bb, gg, 0, 0)),
                  pl.BlockSpec((None, None, nb, hd), lambda bb, gg, qi: (bb, gg, 0, 0)),
                  seq_spec(cols["ks"]), seq_spec(cols["vs"]), seq_spec(cols["kw"]), seq_spec(cols["vw"]),
                  pl.BlockSpec((None, r * tq, 1), lambda bb, gg, qi: (gg, 0, 0)),
                  pl.BlockSpec((t, LANES), lambda bb, gg, qi: (0, 0))],
        out_specs=pl.BlockSpec((tq, qw), lambda bb, gg, qi: (bb * nq + qi, gg)),
        out_shape=SDS((b * t, g * qw), BF16),
        scratch_shapes=[pltpu.VMEM((1, r * tq), F32), pltpu.VMEM((1, r * tq), F32),
                        pltpu.VMEM((hd, r * tq), F32)],
        compiler_params=_cparams(("parallel", "parallel", "arbitrary")),
        name="nsa_prompt",
    )(hab, hab, kc, vc, hab, hab, hab, hab, jnp.repeat(slopes, tq, axis=1).reshape(g, r * tq, 1), kpos)


def _pad_rows(x, rows):
    return jnp.concatenate([x, jnp.zeros((rows - x.shape[0], x.shape[1]), x.dtype)], axis=0)


def _block_expand(ns_pad, first_pos, width):
    blk = lax.broadcasted_iota(I32, (ns_pad, width), 0)
    key_blk = jnp.right_shift(first_pos + lax.broadcasted_iota(I32, (ns_pad, width), 1), 6)
    return jnp.where(blk == key_blk, 1.0, 0.0).astype(BF16)


def _nsa_s_body(pt_ref, q_ref, gt_ref, ksn_ref, vsn_ref, kwn_ref, vwn_ref, kc_ref, vc_ref, win_ref,
                sl_ref, *rest, pp, n_chunks, ts, page, offset, wb, nb, n_sel, ns_pad, k_top, scale, g, r):
    pages = rest[:pp]
    o_ref = rest[pp]
    qs_ref, oc_ref, sel_ref, m_ref, l_ref, acc_ref = rest[pp + 1:]
    del pt_ref
    c = pl.program_id(1)
    hd = HEAD_DIM
    t1 = offset + lax.broadcasted_iota(I32, (ts, 1), 0)
    t = _rep_rows(t1, r)
    lane = lax.broadcasted_iota(I32, (1, page), 1)

    @pl.when(c == 0)
    def _():
        for gg in range(g):
            qs = (_stack_heads(q_ref[:, gg * r * hd:(gg + 1) * r * hd], r, hd) * scale).astype(BF16)
            qs_ref[gg] = qs
            o_c, sel = _nsa_cmp_and_select(qs, sl_ref[gg], t1, kc_ref.at[gg], vc_ref.at[gg], r=r, nb=nb,
                                           n_sel=n_sel, ns_pad=ns_pad, k_top=k_top)
            oc_ref[gg] = o_c
            sel_ref[gg] = _rep_rows(sel, r).astype(BF16)
            _flash_init(m_ref.at[gg], l_ref.at[gg], acc_ref.at[gg])

    width = pp * page
    first = c * width
    d = t - (first + lax.broadcasted_iota(I32, (1, width), 1))
    expand = _block_expand(ns_pad, first, width)
    for gg in range(g):
        ks = [pages[i][0, pl.ds(2 * g + gg, page, stride=4 * g), :].astype(BF16) for i in range(pp)]
        vs = [pages[i][0, pl.ds(3 * g + gg, page, stride=4 * g), :].astype(BF16) for i in range(pp)]
        s = jnp.concatenate([_dot_nt(qs_ref[gg], k) for k in ks], axis=1) - sl_ref[gg] * d.astype(F32)
        bm = _dot(sel_ref[gg], expand)
        _flash_update(s, (bm > 0.5) & (d >= 0), vs, m_ref.at[gg], l_ref.at[gg], acc_ref.at[gg])

    @pl.when(c == n_chunks - 1)
    def _():
        gsig = 1.0 / (1.0 + jnp.exp(-gt_ref[...]))
        dn = t - (offset + lane)
        expand_n = _block_expand(ns_pad, offset, page)
        outs = []
        for gg in range(g):
            k = _pad_rows(ksn_ref[:, gg * hd:(gg + 1) * hd], page).astype(BF16)
            v = _pad_rows(vsn_ref[:, gg * hd:(gg + 1) * hd], page).astype(BF16)
            s = _dot_nt(qs_ref[gg], k) - sl_ref[gg] * dn.astype(F32)
            bm = _dot(sel_ref[gg], expand_n)
            _flash_update(s, (bm > 0.5) & (dn >= 0) & (lane < ts), [v], m_ref.at[gg], l_ref.at[gg],
                          acc_ref.at[gg])
            o_s = _flash_final(l_ref.at[gg], acc_ref.at[gg])
            _flash_init(m_ref.at[gg], l_ref.at[gg], acc_ref.at[gg])
            kw = win_ref[pl.ds(gg, wb, stride=2 * g), :].astype(BF16)
            vw = win_ref[pl.ds(g + gg, wb, stride=2 * g), :].astype(BF16)
            kn = _pad_rows(kwn_ref[:, gg * hd:(gg + 1) * hd], page).astype(BF16)
            vn = _pad_rows(vwn_ref[:, gg * hd:(gg + 1) * hd], page).astype(BF16)
            dw = t - (offset - wb + lax.broadcasted_iota(I32, (1, wb), 1))
            s = jnp.concatenate([_dot_nt(qs_ref[gg], kw) - sl_ref[gg] * dw.astype(F32),
                                 _dot_nt(qs_ref[gg], kn) - sl_ref[gg] * dn.astype(F32)], axis=1)
            mask = jnp.concatenate([(dw >= 0) & (dw <= WINDOW), (dn >= 0) & (dn <= WINDOW) & (lane < ts)], axis=1)
            _flash_update(s, mask, [vw, vn], m_ref.at[gg], l_ref.at[gg], acc_ref.at[gg])
            o_w = _flash_final(l_ref.at[gg], acc_ref.at[gg])
            gs = gsig[:, gg * LANES:(gg + 1) * LANES]
            o = _gate_cols(gs, 0, r) * oc_ref[gg] + _gate_cols(gs, 1, r) * o_s + _gate_cols(gs, 2, r) * o_w
            outs.append(_unstack_heads(o, r, ts))
        o_ref[...] = jnp.concatenate(outs, axis=1)


def nsa_sample(hab, kc, vc, pool, win, page_table, slopes, cols, *, n_p, bs, ts, page, g, r):
    hd = HEAD_DIM
    n_pages = page_table.shape[1]
    offset = n_pages * page
    wb = win.shape[1] // (2 * g)
    assert offset % SEL_BLOCK == 0 and offset - wb >= 0 and wb == WINDOW
    nb = kc.shape[2]
    n_sel = -(-(offset + ts) // SEL_BLOCK)
    ns_pad = -(-n_sel // LANES) * LANES
    pp = _pick(n_pages, (32, 16, 8, 4, 2, 1))
    n_chunks = n_pages // pp
    rb = n_p // ts
    body = functools.partial(_nsa_s_body, pp=pp, n_chunks=n_chunks, ts=ts, page=page, offset=offset, wb=wb,
                             nb=nb, n_sel=n_sel, ns_pad=ns_pad, k_top=min(SEL_TOPK, n_sel),
                             scale=hd ** -0.5, g=g, r=r)

    def row_spec(width, col0):
        return pl.BlockSpec((ts, width), lambda b, c, pt: (rb + b, col0 // width))

    def page_spec(i):
        return pl.BlockSpec((1, page * 4 * g, hd), lambda b, c, pt: (pt[b, c * pp + i], 0, 0))

    in_specs = [row_spec(g * r * hd, cols["q"]), row_spec(g * LANES, cols["gt"]),
                row_spec(g * hd, cols["ks"]), row_spec(g * hd, cols["vs"]),
                row_spec(g * hd, cols["kw"]), row_spec(g * hd, cols["vw"]),
                pl.BlockSpec((None, g, nb, hd), lambda b, c, pt: (b, 0, 0, 0)),
                pl.BlockSpec((None, g, nb, hd), lambda b, c, pt: (b, 0, 0, 0)),
                pl.BlockSpec((None, wb * 2 * g, hd), lambda b, c, pt: (b, 0, 0)),
                pl.BlockSpec((g, r * ts, 1), lambda b, c, pt: (0, 0, 0))]
    in_specs += [page_spec(i) for i in range(pp)]
    grid_spec = pltpu.PrefetchScalarGridSpec(
        num_scalar_prefetch=1, grid=(bs, n_chunks), in_specs=in_specs,
        out_specs=pl.BlockSpec((ts, g * r * hd), lambda b, c, pt: (b, 0)),
        scratch_shapes=[pltpu.VMEM((g, r * ts, hd), BF16), pltpu.VMEM((g, r * ts, hd), F32),
                        pltpu.VMEM((g, r * ts, ns_pad), BF16), pltpu.VMEM((g, r * ts, 1), F32),
                        pltpu.VMEM((g, r * ts, 1), F32), pltpu.VMEM((g, r * ts, hd), F32)])
    return pl.pallas_call(
        body, grid_spec=grid_spec, out_shape=SDS((bs * ts, g * r * hd), F32),
        compiler_params=_cparams(("parallel", "arbitrary")), name="nsa_sample",
    )(page_table, hab, hab, hab, hab, hab, hab, kc, vc, win,
      jnp.repeat(slopes, ts, axis=1).reshape(g, r * ts, 1), *([pool] * pp))


def _cmp_s_body(pt_ref, pe_ref, w1k_ref, w2k_ref, w1v_ref, w2v_ref, *rest, pp, page, g):
    pages = rest[:pp]
    kc_ref, vc_ref, x_ref, hk_ref, hv_ref = rest[pp:]
    del pt_ref
    hd = HEAD_DIM
    tr = 4 * g
    per_page = page // CMP_BLOCK
    blocks = pp * per_page
    for i in range(pp):
        for l in range(CMP_BLOCK):
            tiles = [pages[i][0, (hb * CMP_BLOCK + l) * tr:(hb * CMP_BLOCK + l + 1) * tr, :] + pe_ref[l]
                     for hb in range(per_page)]
            x_ref[i * per_page * tr:(i + 1) * per_page * tr, l * hd:(l + 1) * hd] = (
                jnp.concatenate(tiles, axis=0).astype(BF16))
    x = x_ref[...]
    hk_ref[...] = _dot(_gelu(_dot(x, w1k_ref[...])).astype(BF16), w2k_ref[...])
    hv_ref[...] = _dot(_gelu(_dot(x, w1v_ref[...])).astype(BF16), w2v_ref[...])
    for gg in range(g):
        kc_ref[gg] = hk_ref[pl.ds(gg, blocks, stride=tr), :]
        vc_ref[gg] = hv_ref[pl.ds(g + gg, blocks, stride=tr), :]


def compress_sample(pool, page_table, p, *, page, g):
    hd = HEAD_DIM
    bs, n_pages = page_table.shape
    pp = _pick(n_pages, (16, 8, 4))
    rows = pp * (page // CMP_BLOCK)
    nb = n_pages * (page // CMP_BLOCK)
    hid = p["w1_k"].shape[1]
    body = functools.partial(_cmp_s_body, pp=pp, page=page, g=g)
    once = pl.Buffered(1)
    tr = 4 * g
    pe = jnp.concatenate([jnp.repeat(p["pe_k"][:, None, :], g, axis=1), jnp.repeat(p["pe_v"][:, None, :], g, axis=1),
                          jnp.zeros((CMP_BLOCK, tr - 2 * g, hd), F32)], axis=1)

    def wspecs():
        return [pl.BlockSpec((CMP_BLOCK * hd, hid), lambda b, c, pt: (0, 0), pipeline_mode=once),
                pl.BlockSpec((hid, hd), lambda b, c, pt: (0, 0))]

    def page_spec(i):
        return pl.BlockSpec((1, page * tr, hd), lambda b, c, pt: (pt[b, c * pp + i], 0, 0))

    grid_spec = pltpu.PrefetchScalarGridSpec(
        num_scalar_prefetch=1, grid=(bs, n_pages // pp),
        in_specs=[pl.BlockSpec((CMP_BLOCK, tr, hd), lambda b, c, pt: (0, 0, 0))] + wspecs() + wspecs()
        + [page_spec(i) for i in range(pp)],
        out_specs=[pl.BlockSpec((None, g, rows, hd), lambda b, c, pt: (b, 0, c, 0)),
                   pl.BlockSpec((None, g, rows, hd), lambda b, c, pt: (b, 0, c, 0))],
        scratch_shapes=[pltpu.VMEM((rows * tr, CMP_BLOCK * hd), BF16), pltpu.VMEM((rows * tr, hd), F32),
                        pltpu.VMEM((rows * tr, hd), F32)])
    return pl.pallas_call(
        body, grid_spec=grid_spec, out_shape=[SDS((bs, g, nb, hd), F32), SDS((bs, g, nb, hd), F32)],
        compiler_params=_cparams(("parallel", "arbitrary")), name="nsa_compress_sample",
    )(page_table, pe, p["w1_k"].astype(BF16), p["w2_k"].astype(BF16),
      p["w1_v"].astype(BF16), p["w2_v"].astype(BF16), *([pool] * pp))


def _ret_body(q_ref, k_ref, v_ref, g_ref, gn_ref, s0_ref, dm_ref, cr_ref, kd_ref, cd_ref, o_ref, s_ref,
              st_ref, *, scale, n_chunks, nh):
    c = pl.program_id(1)
    dk = HEAD_DIM

    @pl.when(c == 0)
    def _():
        st_ref[...] = s0_ref[...]

    for h in range(nh):
        cs = slice(h * dk, (h + 1) * dk)
        qb = q_ref[:, cs].astype(BF16)
        ks = k_ref[:, cs] * scale
        kb = ks.astype(BF16)
        vb = v_ref[:, cs].astype(BF16)
        st = st_ref[h]
        att = _dot_nt(qb, kb) * dm_ref[h]
        o = _dot(att.astype(BF16), vb) + _dot(qb, st.astype(BF16)) * cr_ref[h]
        st_ref[h] = st * cd_ref[h] + _dot_tn((ks * kd_ref[h]).astype(BF16), vb)
        mu = jnp.mean(o, axis=-1, keepdims=True)
        var = jnp.mean(jnp.square(o - mu), axis=-1, keepdims=True)
        on = (o - mu) * lax.rsqrt(var + GN_EPS) * gn_ref[:, cs]
        gate = g_ref[:, cs]
        o_ref[:, cs] = ((gate * (1.0 / (1.0 + jnp.exp(-gate)))) * on).astype(o_ref.dtype)

    @pl.when(c == n_chunks - 1)
    def _():
        s_ref[...] = st_ref[...]


def retention(hab, gn_w, s0, cols, *, row0, b, t, nh):
    dk = HEAD_DIM
    w = nh * dk
    ch = RET_CHUNK if t % RET_CHUNK == 0 else t
    n_chunks = t // ch
    lg = jnp.log1p(-jnp.exp2(-5.0 - jnp.arange(nh, dtype=F32)))
    i = jnp.arange(ch, dtype=F32)
    diff = i[:, None] - i[None, :]
    dmask = jnp.where(diff >= 0, jnp.exp(jnp.maximum(diff, 0.0)[None] * lg[:, None, None]), 0.0)
    cross = jnp.exp((i + 1.0)[None, :] * lg[:, None]).reshape(nh, ch, 1)
    kdec = jnp.exp((ch - 1.0 - i)[None, :] * lg[:, None]).reshape(nh, ch, 1)
    cdec = jnp.exp(ch * lg).reshape(nh, 1, 1)
    rb = row0 // ch
    body = functools.partial(_ret_body, scale=dk ** -0.5, n_chunks=n_chunks, nh=nh)

    def col_spec(col0):
        assert col0 % w == 0
        return pl.BlockSpec((ch, w), lambda bb, c: (rb + bb * n_chunks + c, col0 // w))

    def whole(shape):
        return pl.BlockSpec(shape, lambda bb, c: (0,) * len(shape))

    return pl.pallas_call(
        body,
        grid=(b, n_chunks),
        in_specs=[col_spec(cols["rq"]), col_spec(cols["rk"]), col_spec(cols["rv"]), col_spec(cols["rg"]),
                  whole((1, w)),
                  pl.BlockSpec((None, nh, dk, dk), lambda bb, c: (bb, 0, 0, 0)),
                  whole((nh, ch, ch)), whole((nh, ch, 1)), whole((nh, ch, 1)), whole((nh, 1, 1))],
        out_specs=[pl.BlockSpec((ch, w), lambda bb, c: (bb * n_chunks + c, 0)),
                   pl.BlockSpec((None, nh, dk, dk), lambda bb, c: (bb, 0, 0, 0))],
        out_shape=[SDS((b * t, w), BF16 if ch % 16 == 0 else F32), SDS((b, nh, dk, dk), F32)],
        scratch_shapes=[pltpu.VMEM((nh, dk, dk), F32)],
        compiler_params=_cparams(("parallel", "arbitrary")),
        name="retention",
    )(hab, hab, hab, hab, gn_w.reshape(1, w), s0, dmask, cross, kdec, cdec)


def _logf_body(x_ref, b_ref, o_ref):
    x = x_ref[...] + b_ref[...]
    o_ref[...] = -(jnp.maximum(-x, 0.0) + jnp.log1p(jnp.exp(-jnp.abs(x))))


def fox_logf(hc, b_pad, col0):
    n = hc.shape[0]
    tm = _pick(n, (768, 512, 256, 128, 64, 32, 16, 8))
    return pl.pallas_call(
        _logf_body,
        grid=(n // tm,),
        in_specs=[pl.BlockSpec((tm, LANES), lambda i: (i, col0 // LANES)),
                  pl.BlockSpec((1, LANES), lambda i: (0, 0))],
        out_specs=pl.BlockSpec((tm, LANES), lambda i: (i, 0)),
        out_shape=SDS((n, LANES), F32),
        compiler_params=_cparams(("parallel",)),
        name="fox_logf",
    )(hc, b_pad)


def _cumsum_rows_body(x_ref, o_ref, hi_ref, mid_ref, lo_ref, carry_ref):
    @pl.when(pl.program_id(1) == 0)
    def _():
        carry_ref[...] = jnp.zeros_like(carry_ref)

    tc = x_ref.shape[0]
    tri = jnp.where(lax.broadcasted_iota(I32, (tc, tc), 1) <= lax.broadcasted_iota(I32, (tc, tc), 0),
                    1.0, 0.0).astype(BF16)
    hi, mid, lo = _split3(x_ref[...])
    f = (_dot(tri, hi) + _dot(tri, mid)) + _dot(tri, lo) + carry_ref[...]
    o_ref[...] = f
    hi_ref[...], mid_ref[...], lo_ref[...] = _split3(f)
    carry_ref[...] = f[tc - 1:tc, :]


def cumsum_rows(x, *, b, t):
    tc = 128 if t % 128 == 0 else t
    nc = t // tc
    spec = pl.BlockSpec((tc, LANES), lambda bb, c: (bb * nc + c, 0))
    return pl.pallas_call(
        _cumsum_rows_body,
        grid=(b, nc),
        in_specs=[spec],
        out_specs=[spec, spec, spec, spec],
        out_shape=[SDS((b * t, LANES), F32)] + [SDS((b * t, LANES), BF16)] * 3,
        scratch_shapes=[pltpu.VMEM((1, LANES), F32)],
        compiler_params=_cparams(("parallel", "arbitrary")),
        name="fox_cumsum",
    )(x)


def _fox_p_body(q_ref, k_ref, v_ref, fq_ref, fa_ref, bs_ref, o_ref, m_ref, l_ref, acc_ref, *, tq, tk, scale, r):
    qi = pl.program_id(2)
    hd = HEAD_DIM
    rows = r * tq
    qs = (_stack_heads(q_ref[...], r, hd) * scale).astype(BF16)
    fq = jnp.concatenate([fq_ref[:, i:i + 1] for i in range(r)], axis=0)
    hi, mid, lo = _split3(fq)
    lane = lax.broadcasted_iota(I32, (rows, LANES), 1)
    qb = (bs_ref[...] + jnp.where(lane == 3 * r, hi.astype(F32), 0.0) + jnp.where(lane == 3 * r + 1, mid.astype(F32), 0.0)
          + jnp.where(lane == 3 * r + 2, lo.astype(F32), 0.0)).astype(BF16)
    qa = jnp.concatenate([qs, qb], axis=1)
    t_row = qi * tq + jnp.concatenate([lax.broadcasted_iota(I32, (1, tq), 1)] * r, axis=1)

    def tile(j, masked):
        start = pl.multiple_of(j * tk, tk)
        ka = jnp.concatenate([k_ref[pl.ds(start, tk), :].astype(BF16), fa_ref[pl.ds(start, tk), :]], axis=1)
        v = v_ref[pl.ds(start, tk), :].astype(BF16)
        s = _dot_nt(ka, qa)
        if masked:
            mask = start + lax.broadcasted_iota(I32, (tk, 1), 0) <= t_row
            s = jnp.where(mask, s, NEG)
        m_prev = m_ref[...]
        m_new = jnp.maximum(m_prev, jnp.max(s, axis=0, keepdims=True))
        alpha = jnp.exp(m_prev - m_new)
        p = jnp.exp(s - m_new)
        if masked:
            p = jnp.where(mask, p, 0.0)
        l_ref[...] = alpha * l_ref[...] + jnp.sum(p, axis=0, keepdims=True)
        acc_ref[...] = alpha * acc_ref[...] + _dot_tn(v, p.astype(BF16))
        m_ref[...] = m_new

    def full_tile(j, carry):
        tile(j, False)
        return carry

    _flash_init(m_ref, l_ref, acc_ref)
    n_full = (qi * tq) // tk
    lax.fori_loop(0, n_full, full_tile, 0)
    for jj in range(tq // tk):
        tile(n_full + jj, True)
    o = acc_ref[...] / jnp.maximum(l_ref[...], 1e-30)
    o_ref[...] = jnp.concatenate([o[:, i * tq:(i + 1) * tq].T for i in range(r)], axis=1).astype(BF16)


def fox_prompt(hc, f, f_split, cols, *, b, t, g, r):
    hd = HEAD_DIM
    tq = _pick(t, (512, 256, 128, t))
    tk = _pick(tq, (512, 256, 128, tq))
    nq = t // tq
    qw = r * hd
    fq = f.transpose(0, 2, 1, 3)
    fa = jnp.concatenate([piece.transpose(0, 2, 1, 3) for piece in f_split]
                         + [jnp.ones((b, g, t, 3), BF16), jnp.zeros((b, g, t, LANES - 3 * r - 3), BF16)],
                         axis=-1)
    rows = np.arange(r * tq)[:, None] // tq
    lanes = np.arange(LANES)[None, :]
    bsel = jnp.asarray(np.where((lanes < 3 * r) & (lanes % r == rows), -1.0, 0.0), F32)
    body = functools.partial(_fox_p_body, tq=tq, tk=tk, scale=hd ** -0.5, r=r)
    return pl.pallas_call(
        body,
        grid=(b, g, nq),
        in_specs=[pl.BlockSpec((tq, qw), lambda bb, gg, qi: (bb * nq + qi, gg)),
                  pl.BlockSpec((t, hd), lambda bb, gg, qi: (bb, cols["k"] // hd + gg)),
                  pl.BlockSpec((t, hd), lambda bb, gg, qi: (bb, cols["v"] // hd + gg)),
                  pl.BlockSpec((None, None, tq, r), lambda bb, gg, qi: (bb, gg, qi, 0)),
                  pl.BlockSpec((None, None, t, LANES), lambda bb, gg, qi: (bb, gg, 0, 0)),
                  pl.BlockSpec((r * tq, LANES), lambda bb, gg, qi: (0, 0))],
        out_specs=pl.BlockSpec((tq, qw), lambda bb, gg, qi: (bb * nq + qi, gg)),
        out_shape=SDS((b * t, g * qw), BF16),
        scratch_shapes=[pltpu.VMEM((1, r * tq), F32), pltpu.VMEM((1, r * tq), F32),
                        pltpu.VMEM((hd, r * tq), F32)],
        compiler_params=_cparams(("parallel", "parallel", "arbitrary")),
        name="fox_prompt",
    )(hc, hc, hc, fq, fa, bsel)


def _fox_f_body(pt_ref, new_ref, *rest, pp, n_chunks):
    pages = rest[:pp]
    fk_ref, fn_ref, carry_ref = rest[pp:]
    del pt_ref
    c = pl.program_id(1)
    page = pages[0].shape[2]

    @pl.when(c == 0)
    def _():
        carry_ref[...] = jnp.zeros_like(carry_ref)

    ut = jnp.where(lax.broadcasted_iota(I32, (page, page), 0) <= lax.broadcasted_iota(I32, (page, page), 1),
                   1.0, 0.0).astype(BF16)

    def csum(x, carry):
        hi, mid, lo = _split3(x)
        return (_dot(hi, ut) + _dot(mid, ut)) + _dot(lo, ut) + carry

    carry = carry_ref[...]
    for i in range(pp):
        f = csum(pages[i][0], carry)
        fk_ref[:, i * page:(i + 1) * page] = f
        carry = f[:, page - 1:page]
    carry_ref[...] = carry

    @pl.when(c == n_chunks - 1)
    def _():
        fn_ref[...] = csum(new_ref[...], carry)


def fox_f_sample(logf_pool_t, new_t, page_table):
    bs, n_pages = page_table.shape
    _, nh, page = logf_pool_t.shape
    pp = _pick(n_pages, (64, 32, 16, 8, 4, 2, 1))
    n_chunks = n_pages // pp
    body = functools.partial(_fox_f_body, pp=pp, n_chunks=n_chunks)
    in_specs = [pl.BlockSpec((None, nh, page), lambda b, c, pt: (b, 0, 0))]
    in_specs += [pl.BlockSpec((1, nh, page), functools.partial(lambda b, c, pt, i: (pt[b, c * pp + i], 0, 0), i=i))
                 for i in range(pp)]
    grid_spec = pltpu.PrefetchScalarGridSpec(
        num_scalar_prefetch=1, grid=(bs, n_chunks), in_specs=in_specs,
        out_specs=[pl.BlockSpec((None, nh, pp * page), lambda b, c, pt: (b, 0, c)),
                   pl.BlockSpec((None, nh, page), lambda b, c, pt: (b, 0, 0))],
        scratch_shapes=[pltpu.VMEM((nh, 1), F32)])
    return pl.pallas_call(
        body, grid_spec=grid_spec,
        out_shape=[SDS((bs, nh, n_pages * page), F32), SDS((bs, nh, page), F32)],
        compiler_params=_cparams(("parallel", "arbitrary")), name="fox_f_sample",
    )(page_table, new_t, *([logf_pool_t] * pp))


def _fox_s_body(pt_ref, q_ref, kn_ref, vn_ref, fq_ref, fk_ref, fn_ref, *rest, pp, n_chunks, ts, page, offset,
                scale, g, r):
    pages = rest[:pp]
    o_ref = rest[pp]
    qs_ref, m_ref, l_ref, acc_ref = rest[pp + 1:]
    del pt_ref
    c = pl.program_id(1)
    hd = HEAD_DIM

    @pl.when(c == 0)
    def _():
        for gg in range(g):
            qs_ref[gg] = (_stack_heads(q_ref[:, gg * r * hd:(gg + 1) * r * hd], r, hd) * scale).astype(BF16)
            _flash_init(m_ref.at[gg], l_ref.at[gg], acc_ref.at[gg])

    def fk_rows(f, gg):
        return jnp.concatenate([jnp.broadcast_to(f[gg * r + i:gg * r + i + 1, :], (ts, f.shape[1]))
                                for i in range(r)], axis=0)

    fk = fk_ref[...]
    for gg in range(g):
        ks = [pages[i][0, pl.ds(gg, page, stride=2 * g), :].astype(BF16) for i in range(pp)]
        vs = [pages[i][0, pl.ds(g + gg, page, stride=2 * g), :].astype(BF16) for i in range(pp)]
        s = jnp.concatenate([_dot_nt(qs_ref[gg], k) for k in ks], axis=1) + fq_ref[gg] - fk_rows(fk, gg)
        _flash_update(s, None, vs, m_ref.at[gg], l_ref.at[gg], acc_ref.at[gg])

    @pl.when(c == n_chunks - 1)
    def _():
        t = _rep_rows(offset + lax.broadcasted_iota(I32, (ts, 1), 0), r)
        lane = lax.broadcasted_iota(I32, (1, page), 1)
        outs = []
        for gg in range(g):
            k = _pad_rows(kn_ref[:, gg * hd:(gg + 1) * hd], page).astype(BF16)
            v = _pad_rows(vn_ref[:, gg * hd:(gg + 1) * hd], page).astype(BF16)
            s = _dot_nt(qs_ref[gg], k) + fq_ref[gg] - fk_rows(fn_ref[...], gg)
            _flash_update(s, (offset + lane <= t) & (lane < ts), [v], m_ref.at[gg], l_ref.at[gg], acc_ref.at[gg])
            outs.append(_unstack_heads(_flash_final(l_ref.at[gg], acc_ref.at[gg]), r, ts))
        o_ref[...] = jnp.concatenate(outs, axis=1)


def fox_sample(hc, fq, fk, fn, pool, page_table, cols, *, n_p, bs, ts, page, g, r):
    hd = HEAD_DIM
    n_pages = page_table.shape[1]
    nh = g * r
    offset = n_pages * page
    pp = _pick(n_pages, (32, 16, 8, 4, 2, 1))
    n_chunks = n_pages // pp
    rb = n_p // ts
    body = functools.partial(_fox_s_body, pp=pp, n_chunks=n_chunks, ts=ts, page=page, offset=offset,
                             scale=hd ** -0.5, g=g, r=r)

    def row_spec(width, col0):
        return pl.BlockSpec((ts, width), lambda b, c, pt: (rb + b, col0 // width))

    def page_spec(i):
        return pl.BlockSpec((1, page * 2 * g, hd), lambda b, c, pt: (pt[b, c * pp + i], 0, 0))

    in_specs = [row_spec(nh * hd, cols["q"]), row_spec(g * hd, cols["k"]), row_spec(g * hd, cols["v"]),
                pl.BlockSpec((None, g, r * ts, 1), lambda b, c, pt: (b, 0, 0, 0)),
                pl.BlockSpec((None, nh, pp * page), lambda b, c, pt: (b, 0, c)),
                pl.BlockSpec((None, nh, page), lambda b, c, pt: (b, 0, 0))]
    in_specs += [page_spec(i) for i in range(pp)]
    grid_spec = pltpu.PrefetchScalarGridSpec(
        num_scalar_prefetch=1, grid=(bs, n_chunks), in_specs=in_specs,
        out_specs=pl.BlockSpec((ts, nh * hd), lambda b, c, pt: (b, 0)),
        scratch_shapes=[pltpu.VMEM((g, r * ts, hd), BF16), pltpu.VMEM((g, r * ts, 1), F32),
                        pltpu.VMEM((g, r * ts, 1), F32), pltpu.VMEM((g, r * ts, hd), F32)])
    return pl.pallas_call(
        body, grid_spec=grid_spec, out_shape=SDS((bs * ts, nh * hd), F32),
        compiler_params=_cparams(("parallel", "arbitrary")), name="fox_sample",
    )(page_table, hc, hc, hc, fq, fk, fn, *([pool] * pp))


def _ab_layout(nh_a, g, r, nh_r):
    hd = HEAD_DIM
    qa, kv = nh_a * hd, g * hd
    sizes = [("q", qa), ("kc", kv), ("vc", kv), ("ks", kv), ("vs", kv), ("kw", kv), ("vw", kv),
             ("gt_src", nh_a * 3), ("rq", nh_r * hd), ("rk", nh_r * hd), ("rv", nh_r * hd), ("rg", nh_r * hd)]
    src = {}
    pos = 0
    for name, w in sizes:
        src[name] = pos
        pos += w
    order = ["q", "rq", "rk", "rv", "rg", "kc", "vc", "ks", "vs", "kw", "vw"]
    widths = dict(sizes)
    idx = []
    cols = {}
    for name in order:
        cols[name] = len(idx)
        idx += list(range(src[name], src[name] + widths[name]))
    cols["gt"] = len(idx)
    for gg in range(g):
        blk = [-1] * LANES
        for c in range(3):
            for i in range(r):
                blk[c * r + i] = src["gt_src"] + (gg * r + i) * 3 + c
        idx += blk
    return np.array(idx, np.int32), cols


def _gather_cols(w, idx):
    parts = []
    i, n = 0, len(idx)
    while i < n:
        j = i + 1
        if idx[i] < 0:
            while j < n and idx[j] < 0:
                j += 1
            parts.append(jnp.zeros((w.shape[0], j - i), w.dtype))
        else:
            while j < n and idx[j] == idx[j - 1] + 1:
                j += 1
            parts.append(w[:, int(idx[i]):int(idx[i]) + (j - i)])
        i = j
    return jnp.concatenate(parts, axis=1)


def _even_layer(x, n_p, b, t, bs, ts, p, cache_kv, cache_win, state, page_table):
    hd = HEAD_DIM
    d = x.shape[1]
    nh_a = d // (2 * hd)
    g = nh_a // 4
    r = nh_a // g
    nh_r = d // (2 * hd)
    idx, cols = _ab_layout(nh_a, g, r, nh_r)
    tn = 768
    ncol = -(-len(idx) // tn) * tn
    idx = np.concatenate([idx, np.full(ncol - len(idx), -1, np.int32)])
    w_in = _gather_cols(p["w_in"], idx).astype(BF16)
    hab, _ = mm_norm(x, p["norm"], w_in, tn)

    kv4 = 4 * g * hd
    c_rows = cols["kc"]
    c_win = cols["kw"]
    new_rows_p = hab[:n_p, c_rows:c_rows + kv4].reshape(b, t, 4, g, hd)
    new_rows_s = hab[n_p:, c_rows:c_rows + kv4].reshape(bs, ts, 4, g, hd)
    new_win_p = hab[:n_p, c_win:c_win + 2 * g * hd].reshape(b, t, 2, g, hd)
    new_win_s = hab[n_p:, c_win:c_win + 2 * g * hd].reshape(bs, ts, 2, g, hd)
    win_state_p = new_win_p[:, -min(WINDOW, t):]
    wb = cache_win.shape[1]
    win_state_s = jnp.concatenate([cache_win, new_win_s], axis=1)[:, -wb:]

    slopes = jnp.exp2(-8.0 * (jnp.arange(nh_a, dtype=F32) + 1.0) / nh_a).reshape(g, r)
    w1k, w2k = p["w1_k"].astype(BF16), p["w2_k"].astype(BF16)
    w1v, w2v = p["w1_v"].astype(BF16), p["w2_v"].astype(BF16)

    def cmp_pair(blocks):
        bb, nb = blocks.shape[:2]
        flat = blocks.transpose(3, 0, 1, 4, 2, 5).reshape(2, bb * nb * g, CMP_BLOCK * hd)
        kc = compress(flat[0], p["pe_k"].reshape(-1), w1k, w2k).reshape(bb, nb, g, hd).transpose(0, 2, 1, 3)
        vc = compress(flat[1], p["pe_v"].reshape(-1), w1v, w2v).reshape(bb, nb, g, hd).transpose(0, 2, 1, 3)
        return kc, vc

    nb_p = t // CMP_BLOCK
    kc_p, vc_p = cmp_pair(new_rows_p[:, :nb_p * CMP_BLOCK, 0:2].reshape(b, nb_p, CMP_BLOCK, 2, g, hd))
    o_a_p = nsa_prompt(hab, kc_p, vc_p, slopes, cols, b=b, t=t, g=g, r=r)

    page = cache_kv.shape[1]
    n_pages = page_table.shape[1]
    nb_s = (n_pages * page + ts) // CMP_BLOCK
    assert nb_s * CMP_BLOCK == n_pages * page and page % CMP_BLOCK == 0
    pool = cache_kv.reshape(cache_kv.shape[0], page * 4 * g, hd)
    kc_s, vc_s = compress_sample(pool, page_table, p, page=page, g=g)
    o_a_s = nsa_sample(hab, kc_s, vc_s, pool, cache_win.reshape(bs, wb * 2 * g, hd), page_table, slopes, cols,
                       n_p=n_p, bs=bs, ts=ts, page=page, g=g, r=r)

    o_b_p, st_p = retention(hab, p["gn"], jnp.zeros((b, nh_r, hd, hd), F32), cols, row0=0, b=b, t=t, nh=nh_r)
    o_b_s, st_s = retention(hab, p["gn"], state, cols, row0=n_p, b=bs, t=ts, nh=nh_r)

    o = jnp.concatenate([jnp.concatenate([o_a_p, o_b_p.astype(BF16)], axis=1),
                         jnp.concatenate([o_a_s.astype(BF16), o_b_s.astype(BF16)], axis=1)], axis=0)
    x = mm_res(o, p["w_out"].astype(BF16), x)
    return x, (new_rows_p, win_state_p, st_p, new_rows_s, win_state_s, st_s)


def _odd_layer(x, n_p, b, t, bs, ts, p, cache_kv, cache_logf, page_table):
    hd = HEAD_DIM
    d = x.shape[1]
    nh = d // hd
    g = nh // 4
    r = nh // g
    cols = {"q": 0, "k": nh * hd, "v": (nh + g) * hd, "f": (nh + 2 * g) * hd}
    ncol_src = p["w_in"].shape[1]
    tn = 640
    ncol = -(-(cols["f"] + LANES) // tn) * tn
    w_in = jnp.pad(p["w_in"], ((0, 0), (0, ncol - ncol_src))).astype(BF16)
    hc, _ = mm_norm(x, p["norm"], w_in, tn)

    new_rows_p = hc[:n_p, cols["k"]:cols["f"]].reshape(b, t, 2, g, hd)
    new_rows_s = hc[n_p:, cols["k"]:cols["f"]].reshape(bs, ts, 2, g, hd)
    b_pad = jnp.pad(p["b_f"].astype(F32), (0, LANES - nh)).reshape(1, LANES)
    logf = fox_logf(hc, b_pad, cols["f"])
    new_logf_p = logf[:n_p, :nh].reshape(b, t, nh)
    new_logf_s = logf[n_p:, :nh].reshape(bs, ts, nh)

    f_p, *f_split = [a[:, :nh].reshape(b, t, g, r) for a in cumsum_rows(logf, b=b, t=t)]
    o_p = fox_prompt(hc, f_p, f_split, cols, b=b, t=t, g=g, r=r)

    page = cache_kv.shape[1]
    new_t = jnp.pad(new_logf_s.transpose(0, 2, 1), ((0, 0), (0, 0), (0, page - ts)))
    fk, fn = fox_f_sample(cache_logf.transpose(0, 2, 1), new_t, page_table)
    fq = fn[:, :, :ts].reshape(bs, g, r * ts, 1)
    o_s = fox_sample(hc, fq, fk, fn, cache_kv.reshape(cache_kv.shape[0], page * 2 * g, hd), page_table, cols,
                     n_p=n_p, bs=bs, ts=ts, page=page, g=g, r=r)

    x = mm_res(jnp.concatenate([o_p, o_s.astype(BF16)], axis=0), p["w_out"].astype(BF16), x)
    return x, (new_rows_p, new_logf_p, new_rows_s, new_logf_s)


def _peer_layer(x, nw, wq, k1, k2, u_bf, vt_bf, layer):
    q, xn = mm_norm(x, nw, wq.astype(BF16), _pick(wq.shape[1], (1024, 512, 256, 128)))
    s1t, s2t, st = peer_score(q, k1, k2)
    return peer_dense(xn, u_bf, vt_bf, layer, s1t, s2t, st, x)


def kernel(x_prompt, x_sample, cache_nsa_kv, cache_nsa_win, state_ret, cache_fox_kv, cache_fox_logf,
           page_table, norm_mix, norm_ffn, norm_final, w_in_ab, w_out_ab, cmp_pe_k, cmp_w1_k, cmp_w2_k,
           cmp_pe_v, cmp_w1_v, cmp_w2_v, ret_gn, w_in_c, b_forget, w_out_c, peer_wq, peer_k1, peer_k2,
           peer_u, peer_v):
    b, t, d = x_prompt.shape
    bs, ts, _ = x_sample.shape
    n_p, n_s = b * t, bs * ts
    depth = norm_mix.shape[0]
    x = jnp.concatenate([x_prompt.reshape(n_p, d), x_sample.reshape(n_s, d)], axis=0)
    u_bf = peer_u.astype(BF16)
    vt_bf = jnp.swapaxes(peer_v, 1, 2).astype(BF16)
    even, odd = [], []
    for l in range(depth):
        if l % 2 == 0:
            e = l // 2
            p = dict(norm=norm_mix[l], w_in=w_in_ab[e], w_out=w_out_ab[e], pe_k=cmp_pe_k[e], w1_k=cmp_w1_k[e],
                     w2_k=cmp_w2_k[e], pe_v=cmp_pe_v[e], w1_v=cmp_w1_v[e], w2_v=cmp_w2_v[e], gn=ret_gn[e])
            x, outs = _even_layer(x, n_p, b, t, bs, ts, p, cache_nsa_kv[e], cache_nsa_win[e], state_ret[e],
                                  page_table)
            even.append(outs)
        else:
            o = l // 2
            p = dict(norm=norm_mix[l], w_in=w_in_c[o], b_f=b_forget[o], w_out=w_out_c[o])
            x, outs = _odd_layer(x, n_p, b, t, bs, ts, p, cache_fox_kv[o], cache_fox_logf[o], page_table)
            odd.append(outs)
        x = _peer_layer(x, norm_ffn[l], peer_wq[l], peer_k1[l], peer_k2[l], u_bf, vt_bf, l)
    y = rms_final(x, norm_final)
    y_prompt = y[:n_p].reshape(b, t, d)
    y_sample = y[n_p:].reshape(bs, ts, d)

    def stack(group, i):
        return jnp.stack([o[i] for o in group])

    return (y_prompt, y_sample, stack(even, 0), stack(even, 1), stack(even, 2), stack(odd, 0), stack(odd, 1),
            stack(even, 3), stack(even, 4), stack(even, 5), stack(odd, 2), stack(odd, 3))
```
